```python
import math
import jax, jax.numpy as jnp
from jax import lax
import numpy as np

D_MODEL = 4096
BATCH = 4
SEQ = 2048
DEPTH = 2
DEC_BATCH = 8
DEC_SEQ = 4
PAST_LEN = 16384
PAGE_SIZE = 128

SB_HEADS = 16
SB_HEAD_DIM = 128
SB_WIDTH = SB_HEADS * SB_HEAD_DIM
Q_BLOCK = 128
SSM_HEADS = 32
SSM_HEAD_DIM = 64
SSM_WIDTH = SSM_HEADS * SSM_HEAD_DIM
SSM_GROUPS = 8
SSM_STATE = 128
SSM_CONV = 4
SSM_CHUNK = 128
BC_WIDTH = SSM_GROUPS * SSM_STATE
CONV_CH = SSM_WIDTH + 2 * BC_WIDTH
POOL_WINDOWS = (2, 4, 8, 16)
POOL_GROUP_DIM = 512
POOL_WIDTH = len(POOL_WINDOWS) * POOL_GROUP_DIM
POOL_HIST = max(POOL_WINDOWS) - 1
GLA_HEADS = 8
GLA_DK = 128
GLA_DV = 256
GLA_KW = GLA_HEADS * GLA_DK
GLA_VW = GLA_HEADS * GLA_DV
GLA_RANK = 16
GLA_TAU = 16.0
GLA_CHUNK = 64
SPLIT0 = (SB_WIDTH, SB_WIDTH, SB_WIDTH, SSM_WIDTH, CONV_CH, SSM_HEADS)
IN0_WIDTH = sum(SPLIT0)
MIX0_WIDTH = SB_WIDTH + SSM_WIDTH
SPLIT1 = (POOL_WIDTH, GLA_KW, GLA_KW, GLA_VW, GLA_VW, GLA_RANK)
IN1_WIDTH = sum(SPLIT1)
MIX1_WIDTH = POOL_WIDTH + GLA_VW
D_FF = int(math.ceil(8 * D_MODEL / 3 / 256)) * 256
LN_EPS = 1e-5
RMS_EPS = 1e-5
DEEPNORM_ALPHA = (2 * DEPTH) ** 0.25
DEEPNORM_BETA = (8 * DEPTH) ** -0.25

kernel_name = 'hybrid_sb_ssd_pool_gla_step'


def _split(t, sizes):
    offs = np.cumsum(sizes)[:-1].tolist()
    return jnp.split(t, offs, axis=-1)


def layer_norm(x, g, b):
    xf = x.astype(jnp.float32)
    mu = jnp.mean(xf, -1, keepdims=True)
    var = jnp.mean(jnp.square(xf - mu), -1, keepdims=True)
    y = (xf - mu) * lax.rsqrt(var + LN_EPS) * g.astype(jnp.float32) + b.astype(jnp.float32)
    return y.astype(x.dtype)


def rms_norm_groups(x, w, group):
    shp = x.shape
    xg = x.reshape(shp[:-1] + (shp[-1] // group, group))
    xg = xg * lax.rsqrt(jnp.mean(jnp.square(xg), -1, keepdims=True) + RMS_EPS)
    return xg.reshape(shp) * w.astype(jnp.float32)


def swiglu(h, w_gu, w_down):
    g, u = jnp.split(h @ w_gu, 2, axis=-1)
    return (jax.nn.silu(g) * u) @ w_down


def sb_block(q, q_pos, k, v, k_pos, sb_bias):
    z = (jnp.einsum('bqhd,bkhd->bhqk', q, k).astype(jnp.float32) * (SB_HEAD_DIM ** -0.5)
         + sb_bias.astype(jnp.float32)[None, :, None, None])
    mask = k_pos[None, :] < q_pos[:, None]
    log_keep = jnp.where(mask, jax.nn.log_sigmoid(-z), 0.0)
    suffix = lax.cumsum(log_keep, axis=3, reverse=True) - log_keep
    w = jnp.where(mask, jnp.exp(jax.nn.log_sigmoid(z) + suffix), 0.0)
    return jnp.einsum('bhqk,bkhd->bqhd', w.astype(v.dtype), v)


def sb_prompt(q, k, v, sb_bias):
    bn, L = q.shape[:2]
    nb = L // Q_BLOCK
    pos = jnp.arange(L, dtype=jnp.int32)
    qb = q.reshape(bn, nb, Q_BLOCK, SB_HEADS, SB_HEAD_DIM).swapaxes(0, 1)
    pb = pos.reshape(nb, Q_BLOCK)
    ob = lax.map(lambda a: sb_block(a[0], a[1], k, v, pos, sb_bias), (qb, pb))
    return ob.swapaxes(0, 1).reshape(bn, L, SB_WIDTH)


def causal_dwconv_silu(xbc, prev, conv_w, conv_b):
    L = xbc.shape[1]
    ext = jnp.concatenate([prev.astype(xbc.dtype), xbc], axis=1)
    out = conv_b + sum(ext[:, i:i + L] * conv_w[i] for i in range(SSM_CONV))
    return jax.nn.silu(out), ext[:, L:]


def ssd_scan(x, dt, a, bm, cm, h0, chunk):
    bn, L = x.shape[:2]
    nc = L // chunk
    rep = SSM_HEADS // SSM_GROUPS

    def blocks(t):
        return t.reshape((bn, nc, chunk) + t.shape[2:]).swapaxes(0, 1)

    xs = blocks(x.reshape(bn, L, SSM_GROUPS, rep, SSM_HEAD_DIM))
    dts = blocks(dt.reshape(bn, L, SSM_GROUPS, rep))
    bs, cs = blocks(bm), blocks(cm)
    ag = a.reshape(SSM_GROUPS, rep)
    idx = jnp.arange(chunk)
    causal = (idx[:, None] >= idx[None, :])[None, :, :, None, None]

    def step(h, inp):
        xc, dtc, bc, cc = inp
        cum = jnp.cumsum(dtc * ag, axis=1)
        decay = jnp.exp(jnp.where(causal, cum[:, :, None] - cum[:, None], -jnp.inf))
        w = jnp.einsum('btgn,bsgn->btsg', cc, bc)[..., None] * decay * dtc[:, None]
        y = jnp.einsum('btsgr,bsgrp->btgrp', w, xc)
        y = y + jnp.einsum('btgn,bgrpn->btgrp', cc, h) * jnp.exp(cum)[..., None]
        tail = jnp.exp(cum[:, -1:] - cum) * dtc
        h = h * jnp.exp(cum[:, -1])[..., None, None] + jnp.einsum('bsgr,bsgn,bsgrp->bgrpn', tail, bc, xc)
        return h, y

    h0g = h0.astype(jnp.float32).reshape(bn, SSM_GROUPS, rep, SSM_HEAD_DIM, SSM_STATE)
    h, ys = lax.scan(step, h0g, (xs, dts, bs, cs))
    y = ys.swapaxes(0, 1).reshape(bn, L, SSM_HEADS, SSM_HEAD_DIM)
    return y, h.reshape(bn, SSM_HEADS, SSM_HEAD_DIM, SSM_STATE)


def even_mixer(h, kv_past, conv_prev, ssm_prev, chunk, w_in0, sb_bias, conv_w, conv_b, dt_bias, a_log, d_skip,
               ssm_norm_w, w_out0):
    bn, L, _ = h.shape
    q, k, v, z, xbc, dt_raw = _split(h @ w_in0, SPLIT0)
    hd = (bn, L, SB_HEADS, SB_HEAD_DIM)
    q, k, v = q.reshape(hd), k.reshape(hd), v.reshape(hd)
    if kv_past is None:
        o_a = sb_prompt(q, k, v, sb_bias)
    else:
        past = kv_past[0].shape[1]
        k_all = jnp.concatenate([kv_past[0].astype(k.dtype), k], axis=1)
        v_all = jnp.concatenate([kv_past[1].astype(v.dtype), v], axis=1)
        o_a = sb_block(q, past + jnp.arange(L), k_all, v_all, jnp.arange(past + L), sb_bias)
    o_a = o_a.reshape(bn, L, SB_WIDTH)
    xbc_c, conv_new = causal_dwconv_silu(xbc, conv_prev, conv_w, conv_b)
    x_ssm, bm, cm = _split(xbc_c.astype(jnp.float32), (SSM_WIDTH, BC_WIDTH, BC_WIDTH))
    dt = jax.nn.softplus(dt_raw.astype(jnp.float32) + dt_bias.astype(jnp.float32))
    a = -jnp.exp(a_log.astype(jnp.float32))
    x_ssm = x_ssm.reshape(bn, L, SSM_HEADS, SSM_HEAD_DIM)
    y, h_new = ssd_scan(x_ssm, dt, a, bm.reshape(bn, L, SSM_GROUPS, SSM_STATE),
                        cm.reshape(bn, L, SSM_GROUPS, SSM_STATE), ssm_prev, chunk)
    y = y + x_ssm * d_skip.astype(jnp.float32)[:, None]
    y = rms_norm_groups(y.reshape(bn, L, SSM_WIDTH) * jax.nn.silu(z.astype(jnp.float32)),
                        ssm_norm_w, SSM_WIDTH // SSM_GROUPS)
    o = jnp.concatenate([o_a, y.astype(h.dtype)], axis=-1) @ w_out0
    return o, k, v, conv_new, h_new.astype(h.dtype)


def pool_mixer(u, prev, pos0, w_pool, pool_scale):
    bn, L, _ = u.shape
    ext = jnp.concatenate([prev.astype(u.dtype), u], axis=1)
    cs = jnp.cumsum(ext.astype(jnp.float32), axis=1)
    cs = jnp.concatenate([jnp.zeros_like(cs[:, :1]), cs], axis=1)
    pos = (pos0 + jnp.arange(L)).astype(jnp.float32)
    uf = u.astype(jnp.float32)
    diffs = []
    for g, win in enumerate(POOL_WINDOWS):
        ch = slice(g * POOL_GROUP_DIM, (g + 1) * POOL_GROUP_DIM)
        total = (cs[:, POOL_HIST + 1:POOL_HIST + 1 + L, ch]
                 - cs[:, POOL_HIST + 1 - win:POOL_HIST + 1 - win + L, ch])
        count = jnp.minimum(pos + 1.0, float(win))[None, :, None]
        diffs.append(total / count - uf[:, :, ch])
    d = jnp.stack(diffs, axis=2)
    y = jnp.einsum('blgc,gcd->blgd', d, w_pool.astype(jnp.float32)).reshape(bn, L, POOL_WIDTH)
    y = y * pool_scale.astype(jnp.float32)
    return y.astype(u.dtype), ext[:, L:]


def gla_scan(q, k, v, lg, s0, chunk):
    bn, L = q.shape[:2]
    nc = L // chunk

    def blocks(t):
        return t.reshape((bn, nc, chunk) + t.shape[2:]).swapaxes(0, 1)

    idx = jnp.arange(chunk)
    causal = (idx[:, None] >= idx[None, :])[None, :, :, None, None]

    def step(s, inp):
        qc, kc, vc, gc = inp
        b = jnp.cumsum(gc, axis=1)
        decay = jnp.exp(jnp.where(causal, b[:, :, None] - b[:, None], -jnp.inf))
        att = jnp.einsum('bthd,bshd,btshd->bhts', qc, kc, decay)
        o = jnp.einsum('bhts,bshv->bthv', att, vc) + jnp.einsum('bthd,bhdv->bthv', qc * jnp.exp(b), s)
        s = s * jnp.exp(b[:, -1])[..., None] + jnp.einsum('bshd,bshv->bhdv', kc * jnp.exp(b[:, -1:] - b), vc)
        return s, o

    s, os_ = lax.scan(step, s0.astype(jnp.float32), (blocks(q), blocks(k), blocks(v), blocks(lg)))
    return os_.swapaxes(0, 1).reshape(bn, L, GLA_HEADS, GLA_DV), s


def odd_mixer(h, pool_prev, gla_prev, pos0, chunk, w_in1, w_pool, pool_scale, gla_w2, gla_b2,
              gla_norm_w, w_out1):
    bn, L, _ = h.shape
    u, q, k, v, r, g_lr = _split(h @ w_in1, SPLIT1)
    o_c, pool_new = pool_mixer(u, pool_prev, pos0, w_pool, pool_scale)
    kd = (bn, L, GLA_HEADS, GLA_DK)
    lg = (jax.nn.log_sigmoid((g_lr @ gla_w2 + gla_b2).astype(jnp.float32)) / GLA_TAU).reshape(kd)
    qf = q.astype(jnp.float32).reshape(kd) * (GLA_DK ** -0.5)
    o_d, s_new = gla_scan(qf, k.astype(jnp.float32).reshape(kd),
                          v.astype(jnp.float32).reshape(bn, L, GLA_HEADS, GLA_DV), lg, gla_prev, chunk)
    o_d = rms_norm_groups(o_d.reshape(bn, L, GLA_VW), gla_norm_w, GLA_DV) * jax.nn.silu(r.astype(jnp.float32))
    o = jnp.concatenate([o_c, o_d.astype(h.dtype)], axis=-1) @ w_out1
    return o, pool_new, s_new.astype(h.dtype)


def setup_inputs(seed: int = 0) -> dict:
    key = jax.random.key(seed)
    ks = iter(jax.random.split(key, 40))

    def nrm(shape, scale):
        return jax.random.normal(next(ks), shape, jnp.float32) * scale

    n_pages = PAST_LEN // PAGE_SIZE
    n_used = DEC_BATCH * n_pages
    n_pool = n_used + n_used // 4
    kv_shape = (n_pool, PAGE_SIZE, SB_HEADS, SB_HEAD_DIM)
    x_prompt = nrm((BATCH, SEQ, D_MODEL), 1.0)
    x_sample = nrm((DEC_BATCH, DEC_SEQ, D_MODEL), 1.0)
    cache_k = nrm(kv_shape, 1.0)
    cache_v = nrm(kv_shape, 1.0)
    page_table = jax.random.permutation(next(ks), n_pool)[:n_used].reshape(DEC_BATCH, n_pages).astype(jnp.int32)
    state_conv = nrm((DEC_BATCH, SSM_CONV - 1, CONV_CH), 1.0)
    state_ssm = nrm((DEC_BATCH, SSM_HEADS, SSM_HEAD_DIM, SSM_STATE), 0.1)
    state_pool = nrm((DEC_BATCH, POOL_HIST, POOL_WIDTH), 1.0)
    state_gla = nrm((DEC_BATCH, GLA_HEADS, GLA_DK, GLA_DV), 0.1)
    w_in0 = nrm((D_MODEL, IN0_WIDTH), D_MODEL ** -0.5)
    sb_bias = jnp.linspace(-4.0, -10.0, SB_HEADS, dtype=jnp.float32) + nrm((SB_HEADS,), 0.1)
    conv_w = nrm((SSM_CONV, CONV_CH), SSM_CONV ** -0.5)
    conv_b = nrm((CONV_CH,), 0.02)
    dt0 = jnp.exp(jax.random.uniform(next(ks), (SSM_HEADS,), jnp.float32,
                                     minval=math.log(1e-3), maxval=math.log(1e-1)))
    dt_bias = dt0 + jnp.log(-jnp.expm1(-dt0))
    a_log = jnp.log(jax.random.uniform(next(ks), (SSM_HEADS,), jnp.float32, minval=1.0, maxval=16.0))
    d_skip = 1.0 + nrm((SSM_HEADS,), 0.1)
    ssm_norm_w = 1.0 + nrm((SSM_WIDTH,), 0.02)
    w_out0 = nrm((MIX0_WIDTH, D_MODEL), MIX0_WIDTH ** -0.5 * DEEPNORM_BETA)
    w_in1 = nrm((D_MODEL, IN1_WIDTH), D_MODEL ** -0.5)
    w_pool = nrm((len(POOL_WINDOWS), POOL_GROUP_DIM, POOL_GROUP_DIM), POOL_GROUP_DIM ** -0.5)
    pool_scale = 1.0 + nrm((POOL_WIDTH,), 0.02)
    gla_w2 = nrm((GLA_RANK, GLA_KW), GLA_RANK ** -0.5)
    gla_b2 = nrm((GLA_KW,), 0.02)
    gla_norm_w = 1.0 + nrm((GLA_VW,), 0.02)
    w_out1 = nrm((MIX1_WIDTH, D_MODEL), MIX1_WIDTH ** -0.5 * DEEPNORM_BETA)
    ln_g = 1.0 + nrm((DEPTH, 2, D_MODEL), 0.02)
    ln_b = nrm((DEPTH, 2, D_MODEL), 0.02)
    w_gu = nrm((DEPTH, D_MODEL, 2 * D_FF), D_MODEL ** -0.5)
    w_down = nrm((DEPTH, D_FF, D_MODEL), D_FF ** -0.5 * DEEPNORM_BETA)
    return {'x_prompt': x_prompt, 'x_sample': x_sample, 'cache_k': cache_k, 'cache_v': cache_v,
            'page_table': page_table, 'state_conv': state_conv, 'state_ssm': state_ssm,
            'state_pool': state_pool, 'state_gla': state_gla,
            'w_in0': w_in0, 'sb_bias': sb_bias, 'conv_w': conv_w, 'conv_b': conv_b, 'dt_bias': dt_bias,
            'a_log': a_log, 'd_skip': d_skip, 'ssm_norm_w': ssm_norm_w, 'w_out0': w_out0,
            'w_in1': w_in1, 'w_pool': w_pool, 'pool_scale': pool_scale, 'gla_w2': gla_w2, 'gla_b2': gla_b2,
            'gla_norm_w': gla_norm_w, 'w_out1': w_out1,
            'ln_g': ln_g, 'ln_b': ln_b, 'w_gu': w_gu, 'w_down': w_down}


def reference(x_prompt, x_sample, cache_k, cache_v, page_table, state_conv, state_ssm, state_pool, state_gla,
              w_in0, sb_bias, conv_w, conv_b, dt_bias, a_log, d_skip, ssm_norm_w, w_out0,
              w_in1, w_pool, pool_scale, gla_w2, gla_b2, gla_norm_w, w_out1,
              ln_g, ln_b, w_gu, w_down):
    n_pages = PAST_LEN // PAGE_SIZE
    dec_b = x_sample.shape[0]
    k_past = cache_k[page_table].reshape(dec_b, n_pages * PAGE_SIZE, SB_HEADS, SB_HEAD_DIM)
    v_past = cache_v[page_table].reshape(dec_b, n_pages * PAGE_SIZE, SB_HEADS, SB_HEAD_DIM)
    bp = x_prompt.shape[0]
    conv0 = jnp.zeros((bp, SSM_CONV - 1, CONV_CH), x_prompt.dtype)
    ssm0 = jnp.zeros((bp, SSM_HEADS, SSM_HEAD_DIM, SSM_STATE), jnp.float32)
    pool0 = jnp.zeros((bp, POOL_HIST, POOL_WIDTH), x_prompt.dtype)
    gla0 = jnp.zeros((bp, GLA_HEADS, GLA_DK, GLA_DV), jnp.float32)
    xp, xs = x_prompt, x_sample
    for layer in range(DEPTH):
        if layer % 2 == 0:
            mp, k_p, v_p, conv_p, ssm_p = even_mixer(
                xp, None, conv0, ssm0, SSM_CHUNK, w_in0, sb_bias, conv_w, conv_b, dt_bias, a_log, d_skip,
                ssm_norm_w, w_out0)
            ms, k_s, v_s, conv_s, ssm_s = even_mixer(
                xs, (k_past, v_past), state_conv, state_ssm, xs.shape[1], w_in0, sb_bias, conv_w, conv_b,
                dt_bias, a_log, d_skip, ssm_norm_w, w_out0)
        else:
            mp, pool_p, gla_p = odd_mixer(
                xp, pool0, gla0, 0, GLA_CHUNK, w_in1, w_pool, pool_scale, gla_w2, gla_b2, gla_norm_w, w_out1)
            ms, pool_s, gla_s = odd_mixer(
                xs, state_pool, state_gla, PAST_LEN, xs.shape[1], w_in1, w_pool, pool_scale, gla_w2, gla_b2,
                gla_norm_w, w_out1)
        xp = layer_norm(DEEPNORM_ALPHA * xp + mp, ln_g[layer, 0], ln_b[layer, 0])
        xs = layer_norm(DEEPNORM_ALPHA * xs + ms, ln_g[layer, 0], ln_b[layer, 0])
        xp = layer_norm(DEEPNORM_ALPHA * xp + swiglu(xp, w_gu[layer], w_down[layer]), ln_g[layer, 1], ln_b[layer, 1])
        xs = layer_norm(DEEPNORM_ALPHA * xs + swiglu(xs, w_gu[layer], w_down[layer]), ln_g[layer, 1], ln_b[layer, 1])
    return (xp, xs, k_p, v_p, k_s, v_s, conv_p, conv_s, ssm_p, ssm_s, pool_p, pool_s, gla_p, gla_s)
```

```python
import functools

import jax
import jax.numpy as jnp
from jax import lax
from jax.experimental import pallas as pl
from jax.experimental.pallas import tpu as pltpu

F32 = jnp.float32
BF16 = jnp.bfloat16

LANE = 128
SUBLANE = 8
VMEM_LIMIT = 56 * 1024 * 1024

LN_EPS = 1e-5
RMS_EPS = 1e-5
GLA_TAU = 16.0
PAGE = 128
POOL_WINDOWS = (2, 4, 8, 16)
POOL_HIST = 16
CONV_TAPS = 4
CHUNK = 128

NT_DIMS = (((1,), (1,)), ((), ()))


def _params(*sem):
    return pltpu.CompilerParams(dimension_semantics=sem, vmem_limit_bytes=VMEM_LIMIT)


def _sigmoid(x):
    return 1.0 / (1.0 + jnp.exp(-x))


def _silu(x):
    return x * _sigmoid(x)


def _softplus_neg_abs(x):
    return jnp.log1p(jnp.exp(-jnp.abs(x)))


def _iota(shape, dim):
    return lax.broadcasted_iota(jnp.int32, shape, dim)


def _div(x, c):
    return lax.shift_right_logical(x, (c.bit_length() - 1)) if c & (c - 1) == 0 else x // c


def _mod(x, c):
    return (x & (c - 1)) if c & (c - 1) == 0 else x % c


def _split_dot_right(x, m01, terms):
    acc = None
    rem = x
    for _ in range(terms):
        piece = rem.astype(BF16)
        rem = rem - piece.astype(F32)
        d = jnp.dot(piece, m01, preferred_element_type=F32)
        acc = d if acc is None else acc + d
    return acc


def _split_dot_left(m01, x, terms):
    acc = None
    rem = x
    for _ in range(terms):
        piece = rem.astype(BF16)
        rem = rem - piece.astype(F32)
        d = jnp.dot(m01, piece, preferred_element_type=F32)
        acc = d if acc is None else acc + d
    return acc


def _mm_kernel(x_ref, w_ref, o_ref):
    o_ref[...] = jnp.dot(x_ref[...], w_ref[...], preferred_element_type=F32).astype(o_ref.dtype)


def _mm(x, w, out_dtype, tm, tn):
    m, k = x.shape
    n = w.shape[1]
    tm = min(tm, m)
    tn = min(tn, n)
    assert m % tm == 0 and n % tn == 0
    return pl.pallas_call(
        _mm_kernel,
        grid=(m // tm, n // tn),
        in_specs=[pl.BlockSpec((tm, k), lambda i, j: (i, 0)),
                  pl.BlockSpec((k, tn), lambda i, j: (0, j))],
        out_specs=pl.BlockSpec((tm, tn), lambda i, j: (i, j)),
        out_shape=jax.ShapeDtypeStruct((m, n), out_dtype),
        compiler_params=_params("arbitrary", "arbitrary"),
    )(x, w)


def _gu_kernel(x_ref, wg_ref, wu_ref, o_ref):
    x = x_ref[...]
    g = jnp.dot(x, wg_ref[...], preferred_element_type=F32)
    u = jnp.dot(x, wu_ref[...], preferred_element_type=F32)
    o_ref[...] = (_silu(g) * u).astype(o_ref.dtype)


def _gate_up(x, w_gu, tm, tn):
    m, k = x.shape
    f = w_gu.shape[1] // 2
    tm = min(tm, m)
    tn = min(tn, f)
    assert m % tm == 0 and f % tn == 0
    nb = f // tn
    return pl.pallas_call(
        _gu_kernel,
        grid=(m // tm, nb),
        in_specs=[pl.BlockSpec((tm, k), lambda i, j: (i, 0)),
                  pl.BlockSpec((k, tn), lambda i, j: (0, j)),
                  pl.BlockSpec((k, tn), lambda i, j: (0, j + nb))],
        out_specs=pl.BlockSpec((tm, tn), lambda i, j: (i, j)),
        out_shape=jax.ShapeDtypeStruct((m, f), BF16),
        compiler_params=_params("arbitrary", "arbitrary"),
    )(x, w_gu, w_gu)


def _ln_kernel(x_ref, m_ref, g_ref, b_ref, o_ref, ob_ref, *, alpha):
    y = alpha * x_ref[...] + m_ref[...]
    mu = jnp.mean(y, axis=-1, keepdims=True)
    yc = y - mu
    var = jnp.mean(yc * yc, axis=-1, keepdims=True)
    out = yc * lax.rsqrt(var + LN_EPS) * g_ref[...] + b_ref[...]
    o_ref[...] = out
    ob_ref[...] = out.astype(BF16)


def _res_ln(x, m, g, b, alpha, tm):
    rows, d = x.shape
    tm = min(tm, rows)
    assert rows % tm == 0
    row = pl.BlockSpec((tm, d), lambda i: (i, 0))
    vec = pl.BlockSpec((1, d), lambda i: (0, 0))
    return pl.pallas_call(
        functools.partial(_ln_kernel, alpha=alpha),
        grid=(rows // tm,),
        in_specs=[row, row, vec, vec],
        out_specs=[row, row],
        out_shape=[jax.ShapeDtypeStruct((rows, d), F32), jax.ShapeDtypeStruct((rows, d), BF16)],
        compiler_params=_params("arbitrary"),
    )(x, m, g.reshape(1, d), b.reshape(1, d))


def _sb_tile(z, mask, u01, carry):
    l = _softplus_neg_abs(z)
    log_beta = jnp.minimum(z, 0.0) - l
    log_keep = -jnp.maximum(z, 0.0) - l
    if mask is not None:
        log_keep = jnp.where(mask, log_keep, 0.0)
    suffix = _split_dot_right(log_keep, u01, 2) + carry
    w = jnp.exp(log_beta + suffix)
    if mask is not None:
        w = jnp.where(mask, w, 0.0)
    return w, carry + jnp.sum(log_keep, axis=1, keepdims=True)


def _sb_prompt_kernel(bias_ref, q_ref, k_ref, v_ref, o_ref, *, t, scale):
    h = pl.program_id(1)
    i = pl.program_id(2)
    bias = bias_ref[h]
    q = q_ref[...].astype(BF16)
    rows = _iota((t, t), 0)
    cols = _iota((t, t), 1)
    u01 = (rows > cols).astype(BF16)

    def tile(j, mask, carry, acc):
        kb = k_ref[pl.ds(j * t, t), :].astype(BF16)
        vb = v_ref[pl.ds(j * t, t), :].astype(BF16)
        z = lax.dot_general(q, kb, NT_DIMS, preferred_element_type=F32) * scale + bias
        w, carry = _sb_tile(z, mask, u01, carry)
        return carry, acc + jnp.dot(w.astype(BF16), vb, preferred_element_type=F32)

    carry0 = jnp.zeros((t, 1), F32)
    acc0 = jnp.zeros((t, q_ref.shape[1]), F32)
    carry, acc = tile(i, cols < rows, carry0, acc0)

    def body(it, state):
        return tile(i - 1 - it, None, *state)

    _, acc = lax.fori_loop(0, i, body, (carry, acc))
    o_ref[...] = acc.astype(o_ref.dtype)


def _sb_prompt(q, k, v, sb_bias, nseq, t):
    rows, width = q.shape
    heads = sb_bias.shape[0]
    d = width // heads
    seq = rows // nseq
    t = min(t, seq)
    nq = seq // t
    assert seq % t == 0 and d == LANE
    qspec = pl.BlockSpec((t, d), lambda b, h, i, bias: (b * nq + i, h))
    kvspec = pl.BlockSpec((seq, d), lambda b, h, i, bias: (b, h))
    return pl.pallas_call(
        functools.partial(_sb_prompt_kernel, t=t, scale=d ** -0.5),
        grid_spec=pltpu.PrefetchScalarGridSpec(
            num_scalar_prefetch=1,
            grid=(nseq, heads, nq),
            in_specs=[qspec, kvspec, kvspec],
            out_specs=qspec),
        out_shape=jax.ShapeDtypeStruct((rows, width), BF16),
        compiler_params=_params("arbitrary", "arbitrary", "arbitrary"),
    )(sb_bias, q, k, v)


def _sb_decode_kernel(pt_ref, qbd_ref, bias_ref, knew_ref, vnew_ref, kc_ref, vc_ref, o_ref,
                      carry_scr, acc_scr, *, heads, steps, scale):
    p = pl.program_id(1)
    nq = qbd_ref.shape[1]
    nq_per_head = nq // heads
    width = qbd_ref.shape[2]
    d = width // heads
    rows = _iota((PAGE, PAGE), 0)
    cols = _iota((PAGE, PAGE), 1)
    u01 = (rows > cols).astype(BF16)

    @pl.when(p == 0)
    def _():
        carry_scr[...] = jnp.zeros_like(carry_scr)
        acc_scr[...] = jnp.zeros_like(acc_scr)

    def step(kcat, vcat, mask):
        z = lax.dot_general(qbd_ref[0], kcat, NT_DIMS, preferred_element_type=F32) * scale + bias_ref[...]
        w, carry = _sb_tile(z, mask, u01, carry_scr[...])
        carry_scr[...] = carry
        acc_scr[...] += jnp.dot(w.astype(BF16), vcat, preferred_element_type=F32)

    @pl.when(p == 0)
    def _():
        qi = _div(_iota((nq, PAGE), 0), heads)
        kj = _iota((nq, PAGE), 1)
        step(knew_ref[0].astype(BF16), vnew_ref[0].astype(BF16), kj < qi)

    @pl.when(p > 0)
    def _():
        kcat = jnp.concatenate(
            [kc_ref[pl.ds(h, PAGE, stride=heads), :].astype(BF16) for h in range(heads)], axis=1)
        vcat = jnp.concatenate(
            [vc_ref[pl.ds(h, PAGE, stride=heads), :].astype(BF16) for h in range(heads)], axis=1)
        step(kcat, vcat, None)

    @pl.when(p == steps - 1)
    def _():
        acc = acc_scr[...]
        keep = _mod(_iota((nq, width), 0), heads) == _div(_iota((nq, width), 1), d)
        sel = jnp.where(keep, acc, 0.0)
        out_row = _iota((o_ref.shape[1], width), 0)
        out = jnp.zeros((o_ref.shape[1], width), F32)
        for t in range(nq_per_head):
            head_sum = jnp.sum(sel[t * heads:(t + 1) * heads], axis=0, keepdims=True)
            out = jnp.where(out_row == t, head_sum, out)
        o_ref[0] = out


def _sb_decode(q, k, v, cache_k, cache_v, page_table, sb_bias):
    nseq, n_pages = page_table.shape
    heads = sb_bias.shape[0]
    width = q.shape[1]
    d = width // heads
    n_new = q.shape[0] // nseq
    nq = n_new * heads
    assert cache_k.shape[1] == PAGE and d == LANE and n_new <= SUBLANE
    q4 = q.reshape(nseq, n_new, heads, 1, d)
    eye = jnp.eye(heads, dtype=F32).reshape(1, 1, heads, heads, 1)
    qbd = (q4 * eye).reshape(nseq, nq, width).astype(BF16)
    bias_col = jnp.tile(sb_bias.astype(F32), n_new).reshape(nq, 1)
    pad = ((0, 0), (0, PAGE - n_new), (0, 0))
    knew = jnp.pad(k.reshape(nseq, n_new, width), pad)
    vnew = jnp.pad(v.reshape(nseq, n_new, width), pad)
    kc = cache_k.reshape(-1, d)
    vc = cache_v.reshape(-1, d)
    steps = n_pages + 1

    def page_map(b, p, pt):
        return (pt[b, jnp.minimum(n_pages - p, n_pages - 1)], 0)

    seq3 = lambda b, p, pt: (b, 0, 0)
    out = pl.pallas_call(
        functools.partial(_sb_decode_kernel, heads=heads, steps=steps, scale=d ** -0.5),
        grid_spec=pltpu.PrefetchScalarGridSpec(
            num_scalar_prefetch=1,
            grid=(nseq, steps),
            in_specs=[pl.BlockSpec((1, nq, width), seq3),
                      pl.BlockSpec((nq, 1), lambda b, p, pt: (0, 0)),
                      pl.BlockSpec((1, PAGE, width), seq3),
                      pl.BlockSpec((1, PAGE, width), seq3),
                      pl.BlockSpec((PAGE * heads, d), page_map),
                      pl.BlockSpec((PAGE * heads, d), page_map)],
            out_specs=pl.BlockSpec((1, SUBLANE, width), seq3),
            scratch_shapes=[pltpu.VMEM((nq, 1), F32), pltpu.VMEM((nq, width), F32)]),
        out_shape=jax.ShapeDtypeStruct((nseq, SUBLANE, width), F32),
        compiler_params=_params("arbitrary", "arbitrary"),
    )(page_table, qbd, bias_col, knew, vnew, kc, vc)
    return out[:, :n_new].reshape(nseq * n_new, width).astype(BF16)


def _ssd_kernel(x_ref, b_ref, c_ref, px_ref, pb_ref, pc_ref, wx_ref, wb_ref, wc_ref,
                bx_ref, bb_ref, bc_ref, dt_ref, dtb_ref, alog_ref, dskip_ref, z_ref, nw_ref, h0_ref,
                y_ref, hout_ref, h_scr, ex_scr, eb_scr, ec_scr, *, valid_len, n_chunks, rep, hd):
    c = pl.program_id(2)
    cl = x_ref.shape[0]
    xw = rep * hd

    @pl.when(c == 0)
    def _():
        h_scr[...] = h0_ref[0].reshape(xw, h_scr.shape[1])
        ex_scr[0:SUBLANE, :] = px_ref[0]
        eb_scr[0:SUBLANE, :] = pb_ref[0]
        ec_scr[0:SUBLANE, :] = pc_ref[0]

    def conv_silu(e_scr, raw_ref, w_ref, bias_ref):
        e_scr[SUBLANE:SUBLANE + cl, :] = raw_ref[...]
        acc = bias_ref[...]
        for i in range(CONV_TAPS):
            lo = SUBLANE - (CONV_TAPS - 1) + i
            acc = acc + e_scr[lo:lo + cl, :] * w_ref[i:i + 1, :]
        e_scr[0:SUBLANE, :] = e_scr[cl:cl + SUBLANE, :]
        return _silu(acc)

    xc = conv_silu(ex_scr, x_ref, wx_ref, bx_ref)
    bm = conv_silu(eb_scr, b_ref, wb_ref, bb_ref)
    cm = conv_silu(ec_scr, c_ref, wc_ref, bc_ref)
    bm16 = bm.astype(BF16)
    cm16 = cm.astype(BF16)

    row = c * cl + _iota((cl, 1), 0)
    dtr = dt_ref[0, 0] + dtb_ref[0]
    dt = jnp.where(row < valid_len, jnp.maximum(dtr, 0.0) + _softplus_neg_abs(dtr), 0.0)
    a = -jnp.exp(alog_ref[0])
    t_idx = _iota((cl, cl), 0)
    s_idx = _iota((cl, cl), 1)
    tril = t_idx >= s_idx
    cum = _split_dot_left(tril.astype(BF16), dt * a, 3)
    cum_t = cum.T
    dt_t = dt.T
    cum_last = cum[cl - 1:cl, :]
    decay_last = jnp.exp(cum_last)

    g = lax.dot_general(cm16, bm16, NT_DIMS, preferred_element_type=F32)
    lane = _iota((1, xw), 1)
    ws, xrs, hdecay = [], [], []
    e_all = jnp.zeros((cl, xw), F32)
    tail_all = jnp.zeros((cl, xw), F32)
    for r in range(rep):
        in_head = (lane >= r * hd) & (lane < (r + 1) * hd)
        ccol = cum[:, r:r + 1]
        decay = jnp.exp(jnp.where(tril, ccol - cum_t[r:r + 1, :], -jnp.inf))
        ws.append((g * decay * dt_t[r:r + 1, :]).astype(BF16))
        xrs.append(jnp.where(in_head, xc, 0.0).astype(BF16))
        e_all = jnp.where(in_head, jnp.exp(ccol), e_all)
        tail_all = jnp.where(in_head, jnp.exp(cum_last[:, r:r + 1] - ccol) * dt[:, r:r + 1], tail_all)
        hdecay.append(jnp.broadcast_to(decay_last[:, r:r + 1], (hd, h_scr.shape[1])))
    h = h_scr[...]
    intra = jnp.dot(jnp.concatenate(ws, axis=1), jnp.concatenate(xrs, axis=0), preferred_element_type=F32)
    inter = lax.dot_general(cm16, h.astype(BF16), NT_DIMS, preferred_element_type=F32)
    y = intra + inter * e_all + xc * dskip_ref[0]

    xt = xc * tail_all
    xt_t = jnp.concatenate([xt[:, j * LANE:(j + 1) * LANE].T for j in range(xw // LANE)], axis=0)
    h_new = h * jnp.concatenate(hdecay, axis=0) + jnp.dot(xt_t.astype(BF16), bm16, preferred_element_type=F32)
    h_scr[...] = h_new

    gated = y * _silu(z_ref[...])
    ms = jnp.mean(gated * gated, axis=-1, keepdims=True)
    y_ref[...] = (gated * lax.rsqrt(ms + RMS_EPS) * nw_ref[...]).astype(y_ref.dtype)

    @pl.when(c == n_chunks - 1)
    def _():
        hout_ref[0] = h_new.reshape(hout_ref.shape[1:])


def _ssd(xbc, dt_raw, z, conv_prev, h0, valid_len, conv_w, conv_b, dt_bias, a_log, d_skip, norm_w, groups):
    nseq, heads, hd, n = h0.shape
    rows, ch = xbc.shape
    seq = rows // nseq
    rep = heads // groups
    xw = rep * hd
    ssm_w = heads * hd
    cl = CHUNK
    nc = seq // cl
    assert seq % cl == 0 and n == LANE and xw % LANE == 0 and ch == ssm_w + 2 * groups * n
    b_off = ssm_w // n
    c_off = b_off + groups

    prev = jnp.pad(conv_prev, ((0, 0), (SUBLANE - (CONV_TAPS - 1), 0), (0, 0)))
    dt_g = dt_raw.reshape(nseq, seq, groups, rep).transpose(0, 2, 1, 3)
    dt_g = jnp.pad(dt_g, ((0, 0), (0, 0), (0, 0), (0, LANE - rep)))

    def lane_pad(p):
        return jnp.pad(p.astype(F32).reshape(groups, 1, rep), ((0, 0), (0, 0), (0, LANE - rep)))

    dskip_rep = jnp.repeat(d_skip.astype(F32), hd).reshape(groups, 1, xw)
    cb = conv_b.reshape(1, ch)

    def rows_at(width, off):
        return pl.BlockSpec((cl, width), lambda b, g, c: (b * nc + c, off + g))

    def prev_at(width, off):
        return pl.BlockSpec((1, SUBLANE, width), lambda b, g, c: (b, 0, off + g))

    def taps_at(nrows, width, off):
        return pl.BlockSpec((nrows, width), lambda b, g, c: (0, off + g))

    par = pl.BlockSpec((1, 1, LANE), lambda b, g, c: (g, 0, 0))
    state = pl.BlockSpec((1, rep, hd, n), lambda b, g, c: (b, g, 0, 0))
    y, h = pl.pallas_call(
        functools.partial(_ssd_kernel, valid_len=valid_len, n_chunks=nc, rep=rep, hd=hd),
        grid=(nseq, groups, nc),
        in_specs=[rows_at(xw, 0), rows_at(n, b_off), rows_at(n, c_off),
                  prev_at(xw, 0), prev_at(n, b_off), prev_at(n, c_off),
                  taps_at(CONV_TAPS, xw, 0), taps_at(CONV_TAPS, n, b_off), taps_at(CONV_TAPS, n, c_off),
                  taps_at(1, xw, 0), taps_at(1, n, b_off), taps_at(1, n, c_off),
                  pl.BlockSpec((1, 1, cl, LANE), lambda b, g, c: (b, g, c, 0)),
                  par, par,
                  pl.BlockSpec((1, 1, xw), lambda b, g, c: (g, 0, 0)),
                  rows_at(xw, 0),
                  taps_at(1, xw, 0),
                  state],
        out_specs=[rows_at(xw, 0), state],
        out_shape=[jax.ShapeDtypeStruct((rows, ssm_w), BF16), jax.ShapeDtypeStruct(h0.shape, F32)],
        scratch_shapes=[pltpu.VMEM((xw, n), F32),
                        pltpu.VMEM((cl + SUBLANE, xw), F32),
                        pltpu.VMEM((cl + SUBLANE, n), F32),
                        pltpu.VMEM((cl + SUBLANE, n), F32)],
        compiler_params=_params("arbitrary", "arbitrary", "arbitrary"),
    )(xbc, xbc, xbc, prev, prev, prev, conv_w, conv_w, conv_w, cb, cb, cb,
      dt_g, lane_pad(dt_bias), lane_pad(a_log), dskip_rep, z, norm_w.reshape(1, ssm_w), h0)
    return y, h


def _pool_kernel(u_ref, prev_ref, w_ref, s_ref, o_ref, ext_scr, *, pos0, gd):
    i = pl.program_id(1)
    tr = u_ref.shape[0]

    @pl.when(i == 0)
    def _():
        ext_scr[0:POOL_HIST, :] = prev_ref[0]

    ext_scr[POOL_HIST:POOL_HIST + tr, :] = u_ref[...]
    pos = (pos0 + i * tr + _iota((tr, 1), 0)).astype(F32)
    for g, win in enumerate(POOL_WINDOWS):
        lanes = slice(g * gd, (g + 1) * gd)
        total = ext_scr[POOL_HIST:POOL_HIST + tr, lanes]
        for j in range(1, win):
            total = total + ext_scr[POOL_HIST - j:POOL_HIST - j + tr, lanes]
        count = jnp.minimum(pos + 1.0, float(win))
        dlt = total / count - u_ref[:, lanes]
        y = jnp.dot(dlt.astype(BF16), w_ref[g], preferred_element_type=F32)
        o_ref[:, lanes] = (y * s_ref[:, lanes]).astype(o_ref.dtype)
    ext_scr[0:POOL_HIST, :] = ext_scr[tr:tr + POOL_HIST, :]


def _pool(u, prev, pos0, w_pool16, pool_scale, tr):
    nseq = prev.shape[0]
    rows, width = u.shape
    seq = rows // nseq
    tr = min(tr, seq)
    nt = seq // tr
    ng, gd, _ = w_pool16.shape
    assert seq % tr == 0 and ng == len(POOL_WINDOWS) and ng * gd == width
    prev16 = jnp.pad(prev, ((0, 0), (POOL_HIST - prev.shape[1], 0), (0, 0)))
    return pl.pallas_call(
        functools.partial(_pool_kernel, pos0=pos0, gd=gd),
        grid=(nseq, nt),
        in_specs=[pl.BlockSpec((tr, width), lambda b, i: (b * nt + i, 0)),
                  pl.BlockSpec((1, POOL_HIST, width), lambda b, i: (b, 0, 0)),
                  pl.BlockSpec((ng, gd, gd), lambda b, i: (0, 0, 0)),
                  pl.BlockSpec((1, width), lambda b, i: (0, 0))],
        out_specs=pl.BlockSpec((tr, width), lambda b, i: (b * nt + i, 0)),
        out_shape=jax.ShapeDtypeStruct((rows, width), BF16),
        scratch_shapes=[pltpu.VMEM((tr + POOL_HIST, width), F32)],
        compiler_params=_params("arbitrary", "arbitrary"),
    )(u, prev16, w_pool16, pool_scale.reshape(1, width))


def _gla_kernel(q_ref, k_ref, v_ref, r_ref, glr_ref, w2_ref, b2_ref, nw_ref, s0_ref,
                o_ref, sout_ref, s_scr, *, valid_len, n_chunks, scale):
    c = pl.program_id(2)
    cl = q_ref.shape[0]

    @pl.when(c == 0)
    def _():
        s_scr[...] = s0_ref[0, 0]

    row = c * cl + _iota((cl, 1), 0)
    valid = row < valid_len
    pre = jnp.dot(glr_ref[...], w2_ref[...], preferred_element_type=F32) + b2_ref[...]
    lg = (jnp.minimum(pre, 0.0) - _softplus_neg_abs(pre)) / GLA_TAU
    lg = jnp.where(valid, lg, 0.0)
    qs = q_ref[...] * scale
    k = jnp.where(valid, k_ref[...], 0.0)
    v16 = v_ref[...].astype(BF16)

    t_idx = _iota((cl, cl), 0)
    s_idx = _iota((cl, cl), 1)
    r_idx = _iota((cl, 1), 0)
    att = jnp.where(t_idx == s_idx,
                    lax.dot_general(qs.astype(BF16), k.astype(BF16), NT_DIMS, preferred_element_type=F32), 0.0)
    pre_b = lg
    suf_b = jnp.zeros_like(lg)
    tot_b = lg
    bsz = 1
    while bsz < cl:
        qd = (qs * jnp.exp(pre_b)).astype(BF16)
        kd = (k * jnp.exp(suf_b)).astype(BF16)
        siblings = ((_div(t_idx, 2 * bsz) == _div(s_idx, 2 * bsz))
                    & (_mod(_div(t_idx, bsz), 2) == 1) & (_mod(_div(s_idx, bsz), 2) == 0))
        att = att + jnp.where(siblings, lax.dot_general(qd, kd, NT_DIMS, preferred_element_type=F32), 0.0)
        is_right = _mod(_div(r_idx, bsz), 2) == 1
        left_tot = pltpu.roll(tot_b, bsz, axis=0)
        right_tot = pltpu.roll(tot_b, cl - bsz, axis=0)
        pre_b = pre_b + jnp.where(is_right, left_tot, 0.0)
        suf_b = suf_b + jnp.where(is_right, 0.0, right_tot)
        tot_b = tot_b + jnp.where(is_right, left_tot, right_tot)
        bsz *= 2

    s = s_scr[...]
    qd = (qs * jnp.exp(pre_b)).astype(BF16)
    kd = k * jnp.exp(suf_b)
    o = (jnp.dot(att.astype(BF16), v16, preferred_element_type=F32)
         + jnp.dot(qd, s.astype(BF16), preferred_element_type=F32))
    chunk_decay = jnp.exp(tot_b.T)
    s_new = (s * jnp.concatenate([chunk_decay] * (s.shape[1] // cl), axis=1)
             + jnp.dot(kd.T.astype(BF16), v16, preferred_element_type=F32))
    s_scr[...] = s_new

    ms = jnp.mean(o * o, axis=-1, keepdims=True)
    o_ref[...] = (o * lax.rsqrt(ms + RMS_EPS) * nw_ref[...] * _silu(r_ref[...])).astype(o_ref.dtype)

    @pl.when(c == n_chunks - 1)
    def _():
        sout_ref[0, 0] = s_new


def _gla(q, k, v, r, glr, w2, b2, norm_w, s0, valid_len):
    nseq, heads, dk, dv = s0.shape
    rows = q.shape[0]
    seq = rows // nseq
    cl = CHUNK
    nc = seq // cl
    assert seq % cl == 0 and dk == cl and dv % cl == 0 and glr.shape[1] == LANE

    def rows_at(width, col):
        return pl.BlockSpec((cl, width), (lambda b, h, c: (b * nc + c, h)) if col else (lambda b, h, c: (b * nc + c, 0)))

    def head_at(nrows, width):
        return pl.BlockSpec((nrows, width), lambda b, h, c: (0, h))

    state = pl.BlockSpec((1, 1, dk, dv), lambda b, h, c: (b, h, 0, 0))
    return pl.pallas_call(
        functools.partial(_gla_kernel, valid_len=valid_len, n_chunks=nc, scale=dk ** -0.5),
        grid=(nseq, heads, nc),
        in_specs=[rows_at(dk, True), rows_at(dk, True), rows_at(dv, True), rows_at(dv, True),
                  rows_at(LANE, False), head_at(LANE, dk), head_at(1, dk), head_at(1, dv), state],
        out_specs=[rows_at(dv, True), state],
        out_shape=[jax.ShapeDtypeStruct((rows, heads * dv), BF16), jax.ShapeDtypeStruct(s0.shape, F32)],
        scratch_shapes=[pltpu.VMEM((dk, dv), F32)],
        compiler_params=_params("arbitrary", "arbitrary", "arbitrary"),
    )(q, k, v, r, glr, w2, b2.reshape(1, heads * dk), norm_w.reshape(1, heads * dv), s0)


def _pad_seq(a, nseq, seq_pad):
    n = a.shape[0] // nseq
    a3 = jnp.pad(a.reshape(nseq, n, a.shape[1]), ((0, 0), (0, seq_pad - n), (0, 0)))
    return a3.reshape(nseq * seq_pad, a.shape[1])


def _unpad_seq(a, nseq, n):
    return a.reshape(nseq, -1, a.shape[1])[:, :n].reshape(nseq * n, a.shape[1])


def _project(xs, w, sizes, tm, tn):
    outs = [[] for _ in xs]
    off = 0
    for size in sizes:
        ws = w[:, off:off + size].astype(BF16)
        off += size
        if size % LANE:
            ws = jnp.pad(ws, ((0, 0), (0, LANE - size % LANE)))
        for o, x in zip(outs, xs):
            o.append(_mm(x, ws, F32, tm, tn))
    return outs


def kernel(x_prompt, x_sample, cache_k, cache_v, page_table, state_conv, state_ssm, state_pool, state_gla, w_in0, sb_bias, conv_w, conv_b, dt_bias, a_log, d_skip, ssm_norm_w, w_out0, w_in1, w_pool, pool_scale, gla_w2, gla_b2, gla_norm_w, w_out1, ln_g, ln_b, w_gu, w_down):
    bp, seq, dm = x_prompt.shape
    bs, n_new, _ = x_sample.shape
    depth = ln_g.shape[0]
    alpha = (2 * depth) ** 0.25
    sb_heads, sb_d = cache_k.shape[2], cache_k.shape[3]
    sb_w = sb_heads * sb_d
    ssm_heads, ssm_hd, ssm_n = state_ssm.shape[1:]
    ssm_w = ssm_heads * ssm_hd
    conv_ch = state_conv.shape[2]
    groups = (conv_ch - ssm_w) // (2 * ssm_n)
    pool_w = state_pool.shape[2]
    gla_heads, gla_dk, gla_dv = state_gla.shape[1:]
    gla_kw, gla_vw = gla_heads * gla_dk, gla_heads * gla_dv
    gla_rank = gla_w2.shape[0]
    past_len = page_table.shape[1] * cache_k.shape[1]
    seq_pad = CHUNK

    xp = x_prompt.reshape(bp * seq, dm)
    xs = x_sample.reshape(bs * n_new, dm)
    xp16, xs16 = xp.astype(BF16), xs.astype(BF16)

    def ffn_ln(x, x16, layer):
        hmid = _gate_up(x16, w_gu[layer].astype(BF16), 1024, 256)
        f = _mm(hmid, w_down[layer].astype(BF16), F32, 512, 512)
        return _res_ln(x, f, ln_g[layer, 1], ln_b[layer, 1], alpha, 128)

    split0 = (sb_w, sb_w, sb_w, ssm_w, conv_ch, ssm_heads)
    (qp, kp, vp, zp, xbcp, dtp), (qs, ks, vs, zs, xbcs, dts) = _project([xp16, xs16], w_in0, split0, 1024, 1024)
    dtp, dts = dtp[:, :ssm_heads], dts[:, :ssm_heads]

    oap = _sb_prompt(qp, kp, vp, sb_bias, bp, 256)
    oas = _sb_decode(qs, ks, vs, cache_k, cache_v, page_table, sb_bias)

    ssd_w = (conv_w, conv_b, dt_bias, a_log, d_skip, ssm_norm_w, groups)
    yp, ssm_p = _ssd(xbcp, dtp, zp, jnp.zeros((bp, CONV_TAPS - 1, conv_ch), F32),
                     jnp.zeros((bp,) + state_ssm.shape[1:], F32), seq, *ssd_w)
    ys, ssm_s = _ssd(_pad_seq(xbcs, bs, seq_pad), _pad_seq(dts, bs, seq_pad), _pad_seq(zs, bs, seq_pad),
                     state_conv, state_ssm, n_new, *ssd_w)
    ys = _unpad_seq(ys, bs, n_new)

    w_out0_16 = w_out0.astype(BF16)
    mp = _mm(jnp.concatenate([oap, yp], axis=1), w_out0_16, F32, 1024, 1024)
    ms = _mm(jnp.concatenate([oas, ys], axis=1), w_out0_16, F32, 1024, 1024)
    xp, xp16 = _res_ln(xp, mp, ln_g[0, 0], ln_b[0, 0], alpha, 128)
    xs, xs16 = _res_ln(xs, ms, ln_g[0, 0], ln_b[0, 0], alpha, 128)
    xp, xp16 = ffn_ln(xp, xp16, 0)
    xs, xs16 = ffn_ln(xs, xs16, 0)

    split1 = (pool_w, gla_kw, gla_kw, gla_vw, gla_vw, gla_rank)
    (up, gqp, gkp, gvp, grp, glrp), (us, gqs, gks, gvs, grs, glrs) = _project([xp16, xs16], w_in1, split1, 1024, 1024)

    w_pool16 = w_pool.astype(BF16)
    ocp = _pool(up, jnp.zeros((bp, POOL_HIST - 1, pool_w), F32), 0, w_pool16, pool_scale, 256)
    us8 = _pad_seq(us, bs, SUBLANE)
    ocs = _unpad_seq(_pool(us8, state_pool, past_len, w_pool16, pool_scale, SUBLANE), bs, n_new)

    w2_16 = jnp.pad(gla_w2, ((0, LANE - gla_rank), (0, 0))).astype(BF16)
    gla_w = (w2_16, gla_b2, gla_norm_w)
    odp, gla_p = _gla(gqp, gkp, gvp, grp, glrp.astype(BF16), *gla_w,
                      jnp.zeros((bp,) + state_gla.shape[1:], F32), seq)
    pad = lambda a: _pad_seq(a, bs, seq_pad)
    ods, gla_s = _gla(pad(gqs), pad(gks), pad(gvs), pad(grs), pad(glrs).astype(BF16), *gla_w, state_gla, n_new)
    ods = _unpad_seq(ods, bs, n_new)

    w_out1_16 = w_out1.astype(BF16)
    mp = _mm(jnp.concatenate([ocp, odp], axis=1), w_out1_16, F32, 1024, 1024)
    ms = _mm(jnp.concatenate([ocs, ods], axis=1), w_out1_16, F32, 1024, 1024)
    xp, xp16 = _res_ln(xp, mp, ln_g[1, 0], ln_b[1, 0], alpha, 128)
    xs, xs16 = _res_ln(xs, ms, ln_g[1, 0], ln_b[1, 0], alpha, 128)
    xp, xp16 = ffn_ln(xp, xp16, 1)
    xs, xs16 = ffn_ln(xs, xs16, 1)

    hd4 = lambda a, nseq: a.reshape(nseq, -1, sb_heads, sb_d)
    conv_p = xbcp.reshape(bp, seq, conv_ch)[:, seq - (CONV_TAPS - 1):]
    conv_s = jnp.concatenate([state_conv, xbcs.reshape(bs, n_new, conv_ch)], axis=1)[:, n_new:]
    pool_p = up.reshape(bp, seq, pool_w)[:, seq - (POOL_HIST - 1):]
    pool_s = jnp.concatenate([state_pool, us.reshape(bs, n_new, pool_w)], axis=1)[:, n_new:]
    return (xp.reshape(bp, seq, dm), xs.reshape(bs, n_new, dm),
            hd4(kp, bp), hd4(vp, bp), hd4(ks, bs), hd4(vs, bs),
            conv_p, conv_s, ssm_p, ssm_s, pool_p, pool_s, gla_p, gla_s)
```

```python
import functools

import jax
import jax.numpy as jnp
from jax import lax
from jax.experimental import pallas as pl
from jax.experimental.pallas import tpu as pltpu

F32 = jnp.float32
BF16 = jnp.bfloat16

LANE = 128
SUBLANE = 8
VMEM_LIMIT = 56 * 1024 * 1024

LN_EPS = 1e-5
RMS_EPS = 1e-5
GLA_TAU = 16.0
PAGE = 128
POOL_WINDOWS = (2, 4, 8, 16)
POOL_HIST = 16
CONV_TAPS = 4
CHUNK = 128

PROJ_TM, PROJ_TN = 1024, 512
GU_TN = 256
DOWN_TM, DOWN_TN = 512, 512
LN_TM = 128
SB_T, SB_HEADS_PER_STEP = 256, 4
POOL_TM = 256

NT_DIMS = (((1,), (1,)), ((), ()))


def _params(*sem):
    return pltpu.CompilerParams(dimension_semantics=sem, vmem_limit_bytes=VMEM_LIMIT)


def _sigmoid(x):
    return 1.0 / (1.0 + jnp.exp(-x))


def _silu(x):
    return x * _sigmoid(x)


def _softplus_neg_abs(x):
    return jnp.log1p(jnp.exp(-jnp.abs(x)))


def _iota(shape, dim):
    return lax.broadcasted_iota(jnp.int32, shape, dim)


def _div(x, c):
    return lax.shift_right_logical(x, (c.bit_length() - 1)) if c & (c - 1) == 0 else x // c


def _mod(x, c):
    return (x & (c - 1)) if c & (c - 1) == 0 else x % c


def _bf16_pieces(x, terms):
    pieces = [x.astype(BF16)]
    for _ in range(terms - 1):
        x = x - pieces[-1].astype(F32)
        pieces.append(x.astype(BF16))
    return pieces


def _add_all(terms):
    return functools.reduce(lambda a, b: a + b, terms)


def _split_dot_right(x, m01, terms):
    return _add_all([jnp.dot(p, m01, preferred_element_type=F32) for p in _bf16_pieces(x, terms)])


def _split_dot_left(m01, x, terms):
    return _add_all([jnp.dot(m01, p, preferred_element_type=F32) for p in _bf16_pieces(x, terms)])


def _mm_kernel(x_ref, w_ref, o_ref):
    o_ref[...] = jnp.dot(x_ref[...], w_ref[...], preferred_element_type=F32).astype(o_ref.dtype)


def _mm(x, w, out_dtype, tm, tn):
    m, k = x.shape
    n = w.shape[1]
    tm = min(tm, m)
    tn = min(tn, n)
    assert m % tm == 0 and n % tn == 0
    return pl.pallas_call(
        _mm_kernel,
        grid=(m // tm, n // tn),
        in_specs=[pl.BlockSpec((tm, k), lambda i, j: (i, 0)),
                  pl.BlockSpec((k, tn), lambda i, j: (0, j))],
        out_specs=pl.BlockSpec((tm, tn), lambda i, j: (i, j)),
        out_shape=jax.ShapeDtypeStruct((m, n), out_dtype),
        compiler_params=_params("arbitrary", "arbitrary"),
        name="down",
    )(x, w)


def _dot_parts(x_refs, w16, w_is_nk):
    terms, off = [], 0
    for x_ref in x_refs:
        k = x_ref.shape[1]
        if w_is_nk:
            terms.append(lax.dot_general(x_ref[...], w16[:, off:off + k], NT_DIMS, preferred_element_type=F32))
        else:
            terms.append(jnp.dot(x_ref[...], w16[off:off + k, :], preferred_element_type=F32))
        off += k
    return _add_all(terms)


def _proj_kernel(*refs, n_parts, w_is_nk):
    xps, xss = refs[:n_parts], refs[n_parts:2 * n_parts]
    w_ref, op_ref, os_ref, w16 = refs[2 * n_parts:]

    @pl.when(pl.program_id(1) == 0)
    def _():
        w16[...] = w_ref[...].astype(BF16)
        os_ref[...] = _dot_parts(xss, w16, w_is_nk).astype(os_ref.dtype)

    op_ref[...] = _dot_parts(xps, w16, w_is_nk).astype(op_ref.dtype)


def _col_tile(col0, ncols, tn_max):
    if ncols < LANE:
        assert col0 % LANE == 0
        return LANE
    tn = tn_max
    while ncols % tn or col0 % tn:
        tn //= 2
    assert tn >= LANE
    return tn


def _proj(xps, xss, w, w_is_nk, col0, ncols, out_dtype, tm, tn_max, name):
    m = xps[0].shape[0]
    s = xss[0].shape[0]
    k = w.shape[1] if w_is_nk else w.shape[0]
    tm = min(tm, m)
    tn = _col_tile(col0, ncols, tn_max)
    nb = pl.cdiv(ncols, tn)
    cb0 = col0 // tn
    assert m % tm == 0 and sum(x.shape[1] for x in xps) == k
    if w_is_nk:
        w_spec = pl.BlockSpec((tn, k), lambda n, i: (cb0 + n, 0))
        w16 = pltpu.VMEM((tn, k), BF16)
    else:
        w_spec = pl.BlockSpec((k, tn), lambda n, i: (0, cb0 + n))
        w16 = pltpu.VMEM((k, tn), BF16)
    return pl.pallas_call(
        functools.partial(_proj_kernel, n_parts=len(xps), w_is_nk=w_is_nk),
        grid=(nb, m // tm),
        in_specs=([pl.BlockSpec((tm, x.shape[1]), lambda n, i: (i, 0)) for x in xps]
                  + [pl.BlockSpec((s, x.shape[1]), lambda n, i: (0, 0)) for x in xss]
                  + [w_spec]),
        out_specs=[pl.BlockSpec((tm, tn), lambda n, i: (i, n)),
                   pl.BlockSpec((s, tn), lambda n, i: (0, n))],
        out_shape=[jax.ShapeDtypeStruct((m, nb * tn), out_dtype), jax.ShapeDtypeStruct((s, nb * tn), out_dtype)],
        scratch_shapes=[w16],
        compiler_params=_params("arbitrary", "arbitrary"),
        name=name,
    )(*xps, *xss, w)


def _gu_kernel(xp_ref, xs_ref, wg_ref, wu_ref, op_ref, os_ref, wg16, wu16):
    def act(x_ref):
        x = x_ref[...]
        g = jnp.dot(x, wg16[...], preferred_element_type=F32)
        u = jnp.dot(x, wu16[...], preferred_element_type=F32)
        return (_silu(g) * u).astype(op_ref.dtype)

    @pl.when(pl.program_id(1) == 0)
    def _():
        wg16[...] = wg_ref[...].astype(BF16)
        wu16[...] = wu_ref[...].astype(BF16)
        os_ref[...] = act(xs_ref)

    op_ref[...] = act(xp_ref)


def _gate_up(xp, xs, w_gu, layer, tm, tn):
    m, k = xp.shape
    s = xs.shape[0]
    f = w_gu.shape[2] // 2
    tm = min(tm, m)
    tn = min(tn, f)
    assert m % tm == 0 and f % tn == 0
    nb = f // tn
    return pl.pallas_call(
        _gu_kernel,
        grid=(nb, m // tm),
        in_specs=[pl.BlockSpec((tm, k), lambda n, i: (i, 0)),
                  pl.BlockSpec((s, k), lambda n, i: (0, 0)),
                  pl.BlockSpec((None, k, tn), lambda n, i: (layer, 0, n)),
                  pl.BlockSpec((None, k, tn), lambda n, i: (layer, 0, n + nb))],
        out_specs=[pl.BlockSpec((tm, tn), lambda n, i: (i, n)),
                   pl.BlockSpec((s, tn), lambda n, i: (0, n))],
        out_shape=[jax.ShapeDtypeStruct((m, f), BF16), jax.ShapeDtypeStruct((s, f), BF16)],
        scratch_shapes=[pltpu.VMEM((k, tn), BF16), pltpu.VMEM((k, tn), BF16)],
        compiler_params=_params("arbitrary", "arbitrary"),
        name="gate_up",
    )(xp, xs, w_gu, w_gu)


def _ln_kernel(x_ref, m_ref, g_ref, b_ref, o_ref, ob_ref, *, alpha):
    y = alpha * x_ref[...] + m_ref[...]
    mu = jnp.mean(y, axis=-1, keepdims=True)
    yc = y - mu
    var = jnp.mean(yc * yc, axis=-1, keepdims=True)
    out = yc * lax.rsqrt(var + LN_EPS) * g_ref[...] + b_ref[...]
    o_ref[...] = out
    ob_ref[...] = out.astype(BF16)


def _res_ln(x, m, g, b, alpha, tm):
    rows, d = x.shape
    tm = min(tm, rows)
    assert rows % tm == 0
    row = pl.BlockSpec((tm, d), lambda i: (i, 0))
    vec = pl.BlockSpec((1, d), lambda i: (0, 0))
    return pl.pallas_call(
        functools.partial(_ln_kernel, alpha=alpha),
        grid=(rows // tm,),
        in_specs=[row, row, vec, vec],
        out_specs=[row, row],
        out_shape=[jax.ShapeDtypeStruct((rows, d), F32), jax.ShapeDtypeStruct((rows, d), BF16)],
        compiler_params=_params("arbitrary"),
        name="res_ln",
    )(x, m, g.reshape(1, d), b.reshape(1, d))


def _sb_tile(z, mask, u01, carry):
    neg_log_keep = jnp.maximum(z, 0.0) + jnp.log(1.0 + jnp.exp(-jnp.abs(z)))
    if mask is not None:
        neg_log_keep = jnp.where(mask, neg_log_keep, 0.0)
    right = _split_dot_right(neg_log_keep, u01, 2) + carry
    w = jnp.exp(z - neg_log_keep - right)
    if mask is not None:
        w = jnp.where(mask, w, 0.0)
    return w, carry + jnp.sum(neg_log_keep, axis=1, keepdims=True)


def _sb_prompt_kernel(bias_ref, q_ref, k_ref, v_ref, o_ref, *, t, hp, scale):
    g = pl.program_id(1)
    i = pl.program_id(2)
    d = q_ref.shape[1] // hp
    rows = _iota((t, t), 0)
    cols = _iota((t, t), 1)
    u01 = (rows > cols).astype(BF16)
    lanes = [slice(s * d, (s + 1) * d) for s in range(hp)]
    qs = [q_ref[:, ln].astype(BF16) for ln in lanes]
    biases = [bias_ref[g * hp + s] for s in range(hp)]

    def tiles(j, mask, state):
        keys = pl.ds(pl.multiple_of(j * t, t), t)
        zs = [lax.dot_general(qs[s], k_ref[keys, lanes[s]].astype(BF16), NT_DIMS, preferred_element_type=F32)
              * scale + biases[s] for s in range(hp)]
        ws = [_sb_tile(zs[s], mask, u01, state[s][0]) for s in range(hp)]
        return tuple(
            (ws[s][1], state[s][1] + jnp.dot(ws[s][0].astype(BF16), v_ref[keys, lanes[s]].astype(BF16),
                                             preferred_element_type=F32))
            for s in range(hp))

    state = tuple((jnp.zeros((t, 1), F32), jnp.zeros((t, d), F32)) for _ in range(hp))
    state = tiles(i, cols < rows, state)
    state = lax.fori_loop(0, i, lambda it, st: tiles(i - 1 - it, None, st), state)
    for s in range(hp):
        o_ref[:, lanes[s]] = state[s][1].astype(o_ref.dtype)


def _sb_prompt(q, k, v, sb_bias, nseq, t, hp):
    rows, width = q.shape
    heads = sb_bias.shape[0]
    d = width // heads
    seq = rows // nseq
    t = min(t, seq)
    nq = seq // t
    assert seq % t == 0 and d == LANE and heads % hp == 0
    qspec = pl.BlockSpec((t, hp * d), lambda b, g, i, bias: (b * nq + i, g))
    kvspec = pl.BlockSpec((seq, hp * d), lambda b, g, i, bias: (b, g))
    return pl.pallas_call(
        functools.partial(_sb_prompt_kernel, t=t, hp=hp, scale=d ** -0.5),
        grid_spec=pltpu.PrefetchScalarGridSpec(
            num_scalar_prefetch=1,
            grid=(nseq, heads // hp, nq),
            in_specs=[qspec, kvspec, kvspec],
            out_specs=qspec),
        out_shape=jax.ShapeDtypeStruct((rows, width), BF16),
        compiler_params=_params("arbitrary", "arbitrary", "arbitrary"),
        name="sb_prompt",
    )(sb_bias, q, k, v)


def _sb_decode_kernel(pt_ref, qbd_ref, bias_ref, knew_ref, vnew_ref, kc_ref, vc_ref, o_ref,
                      carry_scr, acc_scr, *, heads, steps, scale):
    p = pl.program_id(1)
    nq = qbd_ref.shape[1]
    nq_per_head = nq // heads
    width = qbd_ref.shape[2]
    d = width // heads
    rows = _iota((PAGE, PAGE), 0)
    cols = _iota((PAGE, PAGE), 1)
    u01 = (rows > cols).astype(BF16)

    @pl.when(p == 0)
    def _():
        carry_scr[...] = jnp.zeros_like(carry_scr)
        acc_scr[...] = jnp.zeros_like(acc_scr)

    def step(kcat, vcat, mask):
        z = lax.dot_general(qbd_ref[0], kcat, NT_DIMS, preferred_element_type=F32) * scale + bias_ref[...]
        w, carry = _sb_tile(z, mask, u01, carry_scr[...])
        carry_scr[...] = carry
        acc_scr[...] += jnp.dot(w.astype(BF16), vcat, preferred_element_type=F32)

    @pl.when(p == 0)
    def _():
        qi = _div(_iota((nq, PAGE), 0), heads)
        kj = _iota((nq, PAGE), 1)
        step(knew_ref[0].astype(BF16), vnew_ref[0].astype(BF16), kj < qi)

    @pl.when(p > 0)
    def _():
        kcat = jnp.concatenate(
            [kc_ref[pl.ds(h, PAGE, stride=heads), :].astype(BF16) for h in range(heads)], axis=1)
        vcat = jnp.concatenate(
            [vc_ref[pl.ds(h, PAGE, stride=heads), :].astype(BF16) for h in range(heads)], axis=1)
        step(kcat, vcat, None)

    @pl.when(p == steps - 1)
    def _():
        acc = acc_scr[...]
        keep = _mod(_iota((nq, width), 0), heads) == _div(_iota((nq, width), 1), d)
        sel = jnp.where(keep, acc, 0.0)
        out_row = _iota((o_ref.shape[1], width), 0)
        out = jnp.zeros((o_ref.shape[1], width), F32)
        for t in range(nq_per_head):
            head_sum = jnp.sum(sel[t * heads:(t + 1) * heads], axis=0, keepdims=True)
            out = jnp.where(out_row == t, head_sum, out)
        o_ref[0] = out


def _sb_decode(q, k, v, cache_k, cache_v, page_table, sb_bias):
    nseq, n_pages = page_table.shape
    heads = sb_bias.shape[0]
    width = q.shape[1]
    d = width // heads
    n_new = q.shape[0] // nseq
    nq = n_new * heads
    assert cache_k.shape[1] == PAGE and d == LANE and n_new <= SUBLANE
    q4 = q.reshape(nseq, n_new, heads, 1, d)
    eye = jnp.eye(heads, dtype=F32).reshape(1, 1, heads, heads, 1)
    qbd = (q4 * eye).reshape(nseq, nq, width).astype(BF16)
    bias_col = jnp.tile(sb_bias.astype(F32), n_new).reshape(nq, 1)
    pad = ((0, 0), (0, PAGE - n_new), (0, 0))
    knew = jnp.pad(k.reshape(nseq, n_new, width), pad)
    vnew = jnp.pad(v.reshape(nseq, n_new, width), pad)
    kc = cache_k.reshape(-1, d)
    vc = cache_v.reshape(-1, d)
    steps = n_pages + 1

    def page_map(b, p, pt):
        return (pt[b, jnp.minimum(n_pages - p, n_pages - 1)], 0)

    seq3 = lambda b, p, pt: (b, 0, 0)
    out = pl.pallas_call(
        functools.partial(_sb_decode_kernel, heads=heads, steps=steps, scale=d ** -0.5),
        grid_spec=pltpu.PrefetchScalarGridSpec(
            num_scalar_prefetch=1,
            grid=(nseq, steps),
            in_specs=[pl.BlockSpec((1, nq, width), seq3),
                      pl.BlockSpec((nq, 1), lambda b, p, pt: (0, 0)),
                      pl.BlockSpec((1, PAGE, width), seq3),
                      pl.BlockSpec((1, PAGE, width), seq3),
                      pl.BlockSpec((PAGE * heads, d), page_map),
                      pl.BlockSpec((PAGE * heads, d), page_map)],
            out_specs=pl.BlockSpec((1, SUBLANE, width), seq3),
            scratch_shapes=[pltpu.VMEM((nq, 1), F32), pltpu.VMEM((nq, width), F32)]),
        out_shape=jax.ShapeDtypeStruct((nseq, SUBLANE, width), F32),
        compiler_params=_params("arbitrary", "arbitrary"),
        name="sb_decode",
    )(page_table, qbd, bias_col, knew, vnew, kc, vc)
    return out[:, :n_new].reshape(nseq * n_new, width).astype(BF16)


def _ssd_kernel(x_ref, b_ref, c_ref, px_ref, pb_ref, pc_ref, wx_ref, wb_ref, wc_ref,
                bx_ref, bb_ref, bc_ref, dt_ref, dtb_ref, alog_ref, dskip_ref, z_ref, nw_ref, h0_ref,
                y_ref, hout_ref, h_scr, ex_scr, eb_scr, ec_scr, *, valid_len, n_chunks, rep, hd):
    c = pl.program_id(2)
    cl = x_ref.shape[0]
    xw = rep * hd

    @pl.when(c == 0)
    def _():
        h_scr[...] = h0_ref[0].reshape(xw, h_scr.shape[1])
        ex_scr[0:SUBLANE, :] = px_ref[0]
        eb_scr[0:SUBLANE, :] = pb_ref[0]
        ec_scr[0:SUBLANE, :] = pc_ref[0]

    def conv_silu(e_scr, raw_ref, w_ref, bias_ref):
        e_scr[SUBLANE:SUBLANE + cl, :] = raw_ref[...]
        acc = bias_ref[...]
        for i in range(CONV_TAPS):
            lo = SUBLANE - (CONV_TAPS - 1) + i
            acc = acc + e_scr[lo:lo + cl, :] * w_ref[i:i + 1, :]
        e_scr[0:SUBLANE, :] = e_scr[cl:cl + SUBLANE, :]
        return _silu(acc)

    xc = conv_silu(ex_scr, x_ref, wx_ref, bx_ref)
    bm = conv_silu(eb_scr, b_ref, wb_ref, bb_ref)
    cm = conv_silu(ec_scr, c_ref, wc_ref, bc_ref)
    bm16 = bm.astype(BF16)
    cm16 = cm.astype(BF16)

    row = c * cl + _iota((cl, 1), 0)
    dtr = dt_ref[0, 0] + dtb_ref[0]
    dt = jnp.where(row < valid_len, jnp.maximum(dtr, 0.0) + _softplus_neg_abs(dtr), 0.0)
    a = -jnp.exp(alog_ref[0])
    t_idx = _iota((cl, cl), 0)
    s_idx = _iota((cl, cl), 1)
    tril = t_idx >= s_idx
    cum = _split_dot_left(tril.astype(BF16), dt * a, 3)
    cum_t = cum.T
    dt_t = dt.T
    cum_last = cum[cl - 1:cl, :]
    decay_last = jnp.exp(cum_last)

    g = lax.dot_general(cm16, bm16, NT_DIMS, preferred_element_type=F32)
    lane = _iota((1, xw), 1)
    ws, xrs, hdecay = [], [], []
    e_all = jnp.zeros((cl, xw), F32)
    tail_all = jnp.zeros((cl, xw), F32)
    for r in range(rep):
        in_head = (lane >= r * hd) & (lane < (r + 1) * hd)
        ccol = cum[:, r:r + 1]
        decay = jnp.exp(jnp.where(tril, ccol - cum_t[r:r + 1, :], -jnp.inf))
        ws.append((g * decay * dt_t[r:r + 1, :]).astype(BF16))
        xrs.append(jnp.where(in_head, xc, 0.0).astype(BF16))
        e_all = jnp.where(in_head, jnp.exp(ccol), e_all)
        tail_all = jnp.where(in_head, jnp.exp(cum_last[:, r:r + 1] - ccol) * dt[:, r:r + 1], tail_all)
        hdecay.append(jnp.broadcast_to(decay_last[:, r:r + 1], (hd, h_scr.shape[1])))
    h = h_scr[...]
    intra = jnp.dot(jnp.concatenate(ws, axis=1), jnp.concatenate(xrs, axis=0), preferred_element_type=F32)
    inter = lax.dot_general(cm16, h.astype(BF16), NT_DIMS, preferred_element_type=F32)
    y = intra + inter * e_all + xc * dskip_ref[0]

    xt = xc * tail_all
    xt_t = jnp.concatenate([xt[:, j * LANE:(j + 1) * LANE].T for j in range(xw // LANE)], axis=0)
    h_new = h * jnp.concatenate(hdecay, axis=0) + jnp.dot(xt_t.astype(BF16), bm16, preferred_element_type=F32)
    h_scr[...] = h_new

    gated = y * _silu(z_ref[...])
    ms = jnp.mean(gated * gated, axis=-1, keepdims=True)
    y_ref[...] = (gated * lax.rsqrt(ms + RMS_EPS) * nw_ref[...]).astype(y_ref.dtype)

    @pl.when(c == n_chunks - 1)
    def _():
        hout_ref[0] = h_new.reshape(hout_ref.shape[1:])


def _ssd(xbc, dt_raw, z, conv_prev, h0, valid_len, conv_w, conv_b, dt_bias, a_log, d_skip, norm_w, groups):
    nseq, heads, hd, n = h0.shape
    rows, ch = xbc.shape
    seq = rows // nseq
    rep = heads // groups
    xw = rep * hd
    ssm_w = heads * hd
    cl = CHUNK
    nc = seq // cl
    assert seq % cl == 0 and n == LANE and xw % LANE == 0 and ch == ssm_w + 2 * groups * n
    b_off = ssm_w // n
    c_off = b_off + groups

    prev = jnp.pad(conv_prev, ((0, 0), (SUBLANE - (CONV_TAPS - 1), 0), (0, 0)))
    dt_g = dt_raw.reshape(nseq, seq, groups, rep).transpose(0, 2, 1, 3)
    dt_g = jnp.pad(dt_g, ((0, 0), (0, 0), (0, 0), (0, LANE - rep)))

    def lane_pad(p):
        return jnp.pad(p.astype(F32).reshape(groups, 1, rep), ((0, 0), (0, 0), (0, LANE - rep)))

    dskip_rep = jnp.repeat(d_skip.astype(F32), hd).reshape(groups, 1, xw)
    cb = conv_b.reshape(1, ch)

    def rows_at(width, off):
        return pl.BlockSpec((cl, width), lambda b, g, c: (b * nc + c, off + g))

    def prev_at(width, off):
        return pl.BlockSpec((1, SUBLANE, width), lambda b, g, c: (b, 0, off + g))

    def taps_at(nrows, width, off):
        return pl.BlockSpec((nrows, width), lambda b, g, c: (0, off + g))

    par = pl.BlockSpec((1, 1, LANE), lambda b, g, c: (g, 0, 0))
    state = pl.BlockSpec((1, rep, hd, n), lambda b, g, c: (b, g, 0, 0))
    y, h = pl.pallas_call(
        functools.partial(_ssd_kernel, valid_len=valid_len, n_chunks=nc, rep=rep, hd=hd),
        grid=(nseq, groups, nc),
        in_specs=[rows_at(xw, 0), rows_at(n, b_off), rows_at(n, c_off),
                  prev_at(xw, 0), prev_at(n, b_off), prev_at(n, c_off),
                  taps_at(CONV_TAPS, xw, 0), taps_at(CONV_TAPS, n, b_off), taps_at(CONV_TAPS, n, c_off),
                  taps_at(1, xw, 0), taps_at(1, n, b_off), taps_at(1, n, c_off),
                  pl.BlockSpec((1, 1, cl, LANE), lambda b, g, c: (b, g, c, 0)),
                  par, par,
                  pl.BlockSpec((1, 1, xw), lambda b, g, c: (g, 0, 0)),
                  rows_at(xw, 0),
                  taps_at(1, xw, 0),
                  state],
        out_specs=[rows_at(xw, 0), state],
        out_shape=[jax.ShapeDtypeStruct((rows, ssm_w), BF16), jax.ShapeDtypeStruct(h0.shape, F32)],
        scratch_shapes=[pltpu.VMEM((xw, n), F32),
                        pltpu.VMEM((cl + SUBLANE, xw), F32),
                        pltpu.VMEM((cl + SUBLANE, n), F32),
                        pltpu.VMEM((cl + SUBLANE, n), F32)],
        compiler_params=_params("arbitrary", "arbitrary", "arbitrary"),
        name="ssd",
    )(xbc, xbc, xbc, prev, prev, prev, conv_w, conv_w, conv_w, cb, cb, cb,
      dt_g, lane_pad(dt_bias), lane_pad(a_log), dskip_rep, z, norm_w.reshape(1, ssm_w), h0)
    return y, h


def _pool_kernel(u_ref, prev_ref, w_ref, s_ref, o_ref, ext_scr, *, pos0, gd):
    i = pl.program_id(1)
    tr = u_ref.shape[0]

    @pl.when(i == 0)
    def _():
        ext_scr[0:POOL_HIST, :] = prev_ref[0]

    ext_scr[POOL_HIST:POOL_HIST + tr, :] = u_ref[...]
    pos = (pos0 + i * tr + _iota((tr, 1), 0)).astype(F32)
    for g, win in enumerate(POOL_WINDOWS):
        lanes = slice(g * gd, (g + 1) * gd)
        total = ext_scr[POOL_HIST:POOL_HIST + tr, lanes]
        for j in range(1, win):
            total = total + ext_scr[POOL_HIST - j:POOL_HIST - j + tr, lanes]
        count = jnp.minimum(pos + 1.0, float(win))
        dlt = total / count - u_ref[:, lanes]
        y = jnp.dot(dlt.astype(BF16), w_ref[g], preferred_element_type=F32)
        o_ref[:, lanes] = (y * s_ref[:, lanes]).astype(o_ref.dtype)
    ext_scr[0:POOL_HIST, :] = ext_scr[tr:tr + POOL_HIST, :]


def _pool(u, prev, pos0, w_pool16, pool_scale, tr):
    nseq = prev.shape[0]
    rows, width = u.shape
    seq = rows // nseq
    tr = min(tr, seq)
    nt = seq // tr
    ng, gd, _ = w_pool16.shape
    assert seq % tr == 0 and ng == len(POOL_WINDOWS) and ng * gd == width
    prev16 = jnp.pad(prev, ((0, 0), (POOL_HIST - prev.shape[1], 0), (0, 0)))
    return pl.pallas_call(
        functools.partial(_pool_kernel, pos0=pos0, gd=gd),
        grid=(nseq, nt),
        in_specs=[pl.BlockSpec((tr, width), lambda b, i: (b * nt + i, 0)),
                  pl.BlockSpec((1, POOL_HIST, width), lambda b, i: (b, 0, 0)),
                  pl.BlockSpec((ng, gd, gd), lambda b, i: (0, 0, 0)),
                  pl.BlockSpec((1, width), lambda b, i: (0, 0))],
        out_specs=pl.BlockSpec((tr, width), lambda b, i: (b * nt + i, 0)),
        out_shape=jax.ShapeDtypeStruct((rows, width), BF16),
        scratch_shapes=[pltpu.VMEM((tr + POOL_HIST, width), F32)],
        compiler_params=_params("arbitrary", "arbitrary"),
        name="pool",
    )(u, prev16, w_pool16, pool_scale.reshape(1, width))


def _gla_kernel(q_ref, k_ref, v_ref, r_ref, glr_ref, w2_ref, b2_ref, nw_ref, s0_ref,
                o_ref, sout_ref, s_scr, *, valid_len, n_chunks, scale):
    c = pl.program_id(2)
    cl = q_ref.shape[0]

    @pl.when(c == 0)
    def _():
        s_scr[...] = s0_ref[0, 0]

    row = c * cl + _iota((cl, 1), 0)
    valid = row < valid_len
    pre = jnp.dot(glr_ref[...], w2_ref[...], preferred_element_type=F32) + b2_ref[...]
    lg = (jnp.minimum(pre, 0.0) - _softplus_neg_abs(pre)) / GLA_TAU
    lg = jnp.where(valid, lg, 0.0)
    qs = q_ref[...] * scale
    k = jnp.where(valid, k_ref[...], 0.0)
    v16 = v_ref[...].astype(BF16)

    t_idx = _iota((cl, cl), 0)
    s_idx = _iota((cl, cl), 1)
    r_idx = _iota((cl, 1), 0)
    att = jnp.where(t_idx == s_idx,
                    lax.dot_general(qs.astype(BF16), k.astype(BF16), NT_DIMS, preferred_element_type=F32), 0.0)
    pre_b = lg
    suf_b = jnp.zeros_like(lg)
    tot_b = lg
    bsz = 1
    while bsz < cl:
        qd = (qs * jnp.exp(pre_b)).astype(BF16)
        kd = (k * jnp.exp(suf_b)).astype(BF16)
        siblings = ((_div(t_idx, 2 * bsz) == _div(s_idx, 2 * bsz))
                    & (_mod(_div(t_idx, bsz), 2) == 1) & (_mod(_div(s_idx, bsz), 2) == 0))
        att = att + jnp.where(siblings, lax.dot_general(qd, kd, NT_DIMS, preferred_element_type=F32), 0.0)
        is_right = _mod(_div(r_idx, bsz), 2) == 1
        left_tot = pltpu.roll(tot_b, bsz, axis=0)
        right_tot = pltpu.roll(tot_b, cl - bsz, axis=0)
        pre_b = pre_b + jnp.where(is_right, left_tot, 0.0)
        suf_b = suf_b + jnp.where(is_right, 0.0, right_tot)
        tot_b = tot_b + jnp.where(is_right, left_tot, right_tot)
        bsz *= 2

    s = s_scr[...]
    qd = (qs * jnp.exp(pre_b)).astype(BF16)
    kd = k * jnp.exp(suf_b)
    o = (jnp.dot(att.astype(BF16), v16, preferred_element_type=F32)
         + jnp.dot(qd, s.astype(BF16), preferred_element_type=F32))
    chunk_decay = jnp.exp(tot_b.T)
    s_new = (s * jnp.concatenate([chunk_decay] * (s.shape[1] // cl), axis=1)
             + jnp.dot(kd.T.astype(BF16), v16, preferred_element_type=F32))
    s_scr[...] = s_new

    ms = jnp.mean(o * o, axis=-1, keepdims=True)
    o_ref[...] = (o * lax.rsqrt(ms + RMS_EPS) * nw_ref[...] * _silu(r_ref[...])).astype(o_ref.dtype)

    @pl.when(c == n_chunks - 1)
    def _():
        sout_ref[0, 0] = s_new


def _gla(q, k, v, r, glr, w2, b2, norm_w, s0, valid_len):
    nseq, heads, dk, dv = s0.shape
    rows = q.shape[0]
    seq = rows // nseq
    cl = CHUNK
    nc = seq // cl
    assert seq % cl == 0 and dk == cl and dv % cl == 0 and glr.shape[1] == LANE

    def rows_at(width, col):
        return pl.BlockSpec((cl, width), (lambda b, h, c: (b * nc + c, h)) if col else (lambda b, h, c: (b * nc + c, 0)))

    def head_at(nrows, width):
        return pl.BlockSpec((nrows, width), lambda b, h, c: (0, h))

    state = pl.BlockSpec((1, 1, dk, dv), lambda b, h, c: (b, h, 0, 0))
    return pl.pallas_call(
        functools.partial(_gla_kernel, valid_len=valid_len, n_chunks=nc, scale=dk ** -0.5),
        grid=(nseq, heads, nc),
        in_specs=[rows_at(dk, True), rows_at(dk, True), rows_at(dv, True), rows_at(dv, True),
                  rows_at(LANE, False), head_at(LANE, dk), head_at(1, dk), head_at(1, dv), state],
        out_specs=[rows_at(dv, True), state],
        out_shape=[jax.ShapeDtypeStruct((rows, heads * dv), BF16), jax.ShapeDtypeStruct(s0.shape, F32)],
        scratch_shapes=[pltpu.VMEM((dk, dv), F32)],
        compiler_params=_params("arbitrary", "arbitrary", "arbitrary"),
        name="gla",
    )(q, k, v, r, glr, w2, b2.reshape(1, heads * dk), norm_w.reshape(1, heads * dv), s0)


def _pad_seq(a, nseq, seq_pad):
    n = a.shape[0] // nseq
    a3 = jnp.pad(a.reshape(nseq, n, a.shape[1]), ((0, 0), (0, seq_pad - n), (0, 0)))
    return a3.reshape(nseq * seq_pad, a.shape[1])


def _unpad_seq(a, nseq, n):
    return a.reshape(nseq, -1, a.shape[1])[:, :n].reshape(nseq * n, a.shape[1])


def _project(xp16, xs16, w, sizes, name):
    outs_p, outs_s = [], []
    off = 0
    w_nk = w.T
    for idx, size in enumerate(sizes):
        op, os_ = _proj([xp16], [xs16], w_nk, True, off, size, F32, PROJ_TM, PROJ_TN, f"{name}_{idx}")
        off += size
        outs_p.append(op[:, :size] if size < op.shape[1] else op)
        outs_s.append(os_[:, :size] if size < os_.shape[1] else os_)
    return outs_p, outs_s


def kernel(x_prompt, x_sample, cache_k, cache_v, page_table, state_conv, state_ssm, state_pool, state_gla, w_in0, sb_bias, conv_w, conv_b, dt_bias, a_log, d_skip, ssm_norm_w, w_out0, w_in1, w_pool, pool_scale, gla_w2, gla_b2, gla_norm_w, w_out1, ln_g, ln_b, w_gu, w_down):
    bp, seq, dm = x_prompt.shape
    bs, n_new, _ = x_sample.shape
    depth = ln_g.shape[0]
    alpha = (2 * depth) ** 0.25
    sb_heads, sb_d = cache_k.shape[2], cache_k.shape[3]
    sb_w = sb_heads * sb_d
    ssm_heads, ssm_hd, ssm_n = state_ssm.shape[1:]
    ssm_w = ssm_heads * ssm_hd
    conv_ch = state_conv.shape[2]
    groups = (conv_ch - ssm_w) // (2 * ssm_n)
    pool_w = state_pool.shape[2]
    gla_heads, gla_dk, gla_dv = state_gla.shape[1:]
    gla_kw, gla_vw = gla_heads * gla_dk, gla_heads * gla_dv
    gla_rank = gla_w2.shape[0]
    past_len = page_table.shape[1] * cache_k.shape[1]
    seq_pad = CHUNK

    xp = x_prompt.reshape(bp * seq, dm)
    xs = x_sample.reshape(bs * n_new, dm)
    xp16, xs16 = xp.astype(BF16), xs.astype(BF16)

    def mix_ffn_ln(xp, xs, mixed_p, mixed_s, w_out, layer):
        mp, ms = _proj(mixed_p, mixed_s, w_out, False, 0, dm, F32, PROJ_TM, PROJ_TN, f"out{layer}")
        xp, xp16 = _res_ln(xp, mp, ln_g[layer, 0], ln_b[layer, 0], alpha, LN_TM)
        xs, xs16 = _res_ln(xs, ms, ln_g[layer, 0], ln_b[layer, 0], alpha, LN_TM)
        hp, hs = _gate_up(xp16, xs16, w_gu, layer, PROJ_TM, GU_TN)
        w_down16 = w_down[layer].astype(BF16)
        fp = _mm(hp, w_down16, F32, DOWN_TM, DOWN_TN)
        fs = _mm(hs, w_down16, F32, DOWN_TM, DOWN_TN)
        xp, xp16 = _res_ln(xp, fp, ln_g[layer, 1], ln_b[layer, 1], alpha, LN_TM)
        xs, xs16 = _res_ln(xs, fs, ln_g[layer, 1], ln_b[layer, 1], alpha, LN_TM)
        return xp, xp16, xs, xs16

    split0 = (sb_w, sb_w, sb_w, ssm_w, conv_ch, ssm_heads)
    (qp, kp, vp, zp, xbcp, dtp), (qs, ks, vs, zs, xbcs, dts) = _project(xp16, xs16, w_in0, split0, "in0")

    oap = _sb_prompt(qp, kp, vp, sb_bias, bp, SB_T, SB_HEADS_PER_STEP)
    oas = _sb_decode(qs, ks, vs, cache_k, cache_v, page_table, sb_bias)

    ssd_w = (conv_w, conv_b, dt_bias, a_log, d_skip, ssm_norm_w, groups)
    yp, ssm_p = _ssd(xbcp, dtp, zp, jnp.zeros((bp, CONV_TAPS - 1, conv_ch), F32),
                     jnp.zeros((bp,) + state_ssm.shape[1:], F32), seq, *ssd_w)
    ys, ssm_s = _ssd(_pad_seq(xbcs, bs, seq_pad), _pad_seq(dts, bs, seq_pad), _pad_seq(zs, bs, seq_pad),
                     state_conv, state_ssm, n_new, *ssd_w)
    ys = _unpad_seq(ys, bs, n_new)

    xp, xp16, xs, xs16 = mix_ffn_ln(xp, xs, [oap, yp], [oas, ys], w_out0, 0)

    split1 = (pool_w, gla_kw, gla_kw, gla_vw, gla_vw, gla_rank)
    (up, gqp, gkp, gvp, grp, glrp), (us, gqs, gks, gvs, grs, glrs) = _project(xp16, xs16, w_in1, split1, "in1")
    lane_pad = ((0, 0), (0, LANE - gla_rank))
    glrp, glrs = jnp.pad(glrp, lane_pad), jnp.pad(glrs, lane_pad)

    w_pool16 = w_pool.astype(BF16)
    ocp = _pool(up, jnp.zeros((bp, POOL_HIST - 1, pool_w), F32), 0, w_pool16, pool_scale, POOL_TM)
    us8 = _pad_seq(us, bs, SUBLANE)
    ocs = _unpad_seq(_pool(us8, state_pool, past_len, w_pool16, pool_scale, SUBLANE), bs, n_new)

    w2_16 = jnp.pad(gla_w2, ((0, LANE - gla_rank), (0, 0))).astype(BF16)
    gla_w = (w2_16, gla_b2, gla_norm_w)
    odp, gla_p = _gla(gqp, gkp, gvp, grp, glrp.astype(BF16), *gla_w,
                      jnp.zeros((bp,) + state_gla.shape[1:], F32), seq)
    pad = lambda a: _pad_seq(a, bs, seq_pad)
    ods, gla_s = _gla(pad(gqs), pad(gks), pad(gvs), pad(grs), pad(glrs).astype(BF16), *gla_w, state_gla, n_new)
    ods = _unpad_seq(ods, bs, n_new)

    xp, xp16, xs, xs16 = mix_ffn_ln(xp, xs, [ocp, odp], [ocs, ods], w_out1, 1)

    hd4 = lambda a, nseq: a.reshape(nseq, -1, sb_heads, sb_d)
    conv_p = xbcp.reshape(bp, seq, conv_ch)[:, seq - (CONV_TAPS - 1):]
    conv_s = jnp.concatenate([state_conv, xbcs.reshape(bs, n_new, conv_ch)], axis=1)[:, n_new:]
    pool_p = up.reshape(bp, seq, pool_w)[:, seq - (POOL_HIST - 1):]
    pool_s = jnp.concatenate([state_pool, us.reshape(bs, n_new, pool_w)], axis=1)[:, n_new:]
    return (xp.reshape(bp, seq, dm), xs.reshape(bs, n_new, dm),
            hd4(kp, bp), hd4(vp, bp), hd4(ks, bs), hd4(vs, bs),
            conv_p, conv_s, ssm_p, ssm_s, pool_p, pool_s, gla_p, gla_s)
```

```python
import functools

import jax
import jax.numpy as jnp
from jax import lax
from jax.experimental import pallas as pl
from jax.experimental.pallas import tpu as pltpu

F32 = jnp.float32
BF16 = jnp.bfloat16

LANE = 128
SUBLANE = 8
VMEM_LIMIT = 56 * 1024 * 1024

LN_EPS = 1e-5
RMS_EPS = 1e-5
GLA_TAU = 16.0
PAGE = 128
POOL_WINDOWS = (2, 4, 8, 16)
POOL_HIST = 16
CONV_TAPS = 4
CHUNK = 128

PROJ_TM, PROJ_TN = 1024, 512
GU_TN = 256
DOWN_TM, DOWN_TN = 512, 512
LN_TM = 128
SB_T, SB_HEADS_PER_STEP = 256, 4
POOL_TM = 256
DECODE_PAGES_PER_STEP = 4
SSD_GROUPS_PER_STEP = 2
GLA_HEADS_PER_STEP = 2

NT_DIMS = (((1,), (1,)), ((), ()))


def _params(*sem):
    return pltpu.CompilerParams(dimension_semantics=sem, vmem_limit_bytes=VMEM_LIMIT)


def _sigmoid(x):
    return 1.0 / (1.0 + jnp.exp(-x))


def _silu(x):
    return x * _sigmoid(x)


def _softplus_neg_abs(x):
    return jnp.log1p(jnp.exp(-jnp.abs(x)))


def _iota(shape, dim):
    return lax.broadcasted_iota(jnp.int32, shape, dim)


def _div(x, c):
    return lax.shift_right_logical(x, (c.bit_length() - 1)) if c & (c - 1) == 0 else x // c


def _mod(x, c):
    return (x & (c - 1)) if c & (c - 1) == 0 else x % c


def _bf16_pieces(x, terms):
    pieces = [x.astype(BF16)]
    for _ in range(terms - 1):
        x = x - pieces[-1].astype(F32)
        pieces.append(x.astype(BF16))
    return pieces


def _add_all(terms):
    return functools.reduce(lambda a, b: a + b, terms)


def _split_dot_right(x, m01, terms):
    return _add_all([jnp.dot(p, m01, preferred_element_type=F32) for p in _bf16_pieces(x, terms)])


def _split_dot_left(m01, x, terms):
    return _add_all([jnp.dot(m01, p, preferred_element_type=F32) for p in _bf16_pieces(x, terms)])


def _mm_kernel(x_ref, w_ref, o_ref):
    o_ref[...] = jnp.dot(x_ref[...], w_ref[...], preferred_element_type=F32).astype(o_ref.dtype)


def _mm(x, w, out_dtype, tm, tn):
    m, k = x.shape
    n = w.shape[1]
    tm = min(tm, m)
    tn = min(tn, n)
    assert m % tm == 0 and n % tn == 0
    return pl.pallas_call(
        _mm_kernel,
        grid=(m // tm, n // tn),
        in_specs=[pl.BlockSpec((tm, k), lambda i, j: (i, 0)),
                  pl.BlockSpec((k, tn), lambda i, j: (0, j))],
        out_specs=pl.BlockSpec((tm, tn), lambda i, j: (i, j)),
        out_shape=jax.ShapeDtypeStruct((m, n), out_dtype),
        compiler_params=_params("arbitrary", "arbitrary"),
        name="down",
    )(x, w)


def _dot_parts(x_refs, w16, w_is_nk):
    terms, off = [], 0
    for x_ref in x_refs:
        k = x_ref.shape[1]
        if w_is_nk:
            terms.append(lax.dot_general(x_ref[...], w16[:, off:off + k], NT_DIMS, preferred_element_type=F32))
        else:
            terms.append(jnp.dot(x_ref[...], w16[off:off + k, :], preferred_element_type=F32))
        off += k
    return _add_all(terms)


def _proj_kernel(*refs, n_parts, w_is_nk, n_valid):
    xps, xss = refs[:n_parts], refs[n_parts:2 * n_parts]
    w_ref, op_ref, os_ref, w16 = refs[2 * n_parts:]

    @pl.when(pl.program_id(1) == 0)
    def _():
        w = w_ref[...]
        if n_valid is not None:
            w = jnp.where(_iota(w.shape, 0 if w_is_nk else 1) < n_valid, w, 0.0)
        w16[...] = w.astype(BF16)
        os_ref[...] = _dot_parts(xss, w16, w_is_nk).astype(os_ref.dtype)

    op_ref[...] = _dot_parts(xps, w16, w_is_nk).astype(op_ref.dtype)


def _col_tile(col0, ncols, tn_max):
    if ncols < LANE:
        assert col0 % LANE == 0
        return LANE
    tn = tn_max
    while ncols % tn or col0 % tn:
        tn //= 2
    assert tn >= LANE
    return tn


def _proj(xps, xss, w, w_is_nk, col0, ncols, out_dtype, tm, tn_max, name):
    m = xps[0].shape[0]
    s = xss[0].shape[0]
    k = w.shape[1] if w_is_nk else w.shape[0]
    tm = min(tm, m)
    tn = _col_tile(col0, ncols, tn_max)
    nb = pl.cdiv(ncols, tn)
    cb0 = col0 // tn
    assert m % tm == 0 and sum(x.shape[1] for x in xps) == k
    if w_is_nk:
        w_spec = pl.BlockSpec((tn, k), lambda n, i: (cb0 + n, 0))
        w16 = pltpu.VMEM((tn, k), BF16)
    else:
        w_spec = pl.BlockSpec((k, tn), lambda n, i: (0, cb0 + n))
        w16 = pltpu.VMEM((k, tn), BF16)
    return pl.pallas_call(
        functools.partial(_proj_kernel, n_parts=len(xps), w_is_nk=w_is_nk,
                          n_valid=ncols if ncols < tn else None),
        grid=(nb, m // tm),
        in_specs=([pl.BlockSpec((tm, x.shape[1]), lambda n, i: (i, 0)) for x in xps]
                  + [pl.BlockSpec((s, x.shape[1]), lambda n, i: (0, 0)) for x in xss]
                  + [w_spec]),
        out_specs=[pl.BlockSpec((tm, tn), lambda n, i: (i, n)),
                   pl.BlockSpec((s, tn), lambda n, i: (0, n))],
        out_shape=[jax.ShapeDtypeStruct((m, nb * tn), out_dtype), jax.ShapeDtypeStruct((s, nb * tn), out_dtype)],
        scratch_shapes=[w16],
        compiler_params=_params("arbitrary", "arbitrary"),
        name=name,
    )(*xps, *xss, w)


def _gu_kernel(xp_ref, xs_ref, wg_ref, wu_ref, op_ref, os_ref, wg16, wu16):
    def act(x_ref):
        x = x_ref[...]
        g = jnp.dot(x, wg16[...], preferred_element_type=F32)
        u = jnp.dot(x, wu16[...], preferred_element_type=F32)
        return (_silu(g) * u).astype(op_ref.dtype)

    @pl.when(pl.program_id(1) == 0)
    def _():
        wg16[...] = wg_ref[...].astype(BF16)
        wu16[...] = wu_ref[...].astype(BF16)
        os_ref[...] = act(xs_ref)

    op_ref[...] = act(xp_ref)


def _gate_up(xp, xs, w_gu, layer, tm, tn):
    m, k = xp.shape
    s = xs.shape[0]
    f = w_gu.shape[2] // 2
    tm = min(tm, m)
    tn = min(tn, f)
    assert m % tm == 0 and f % tn == 0
    nb = f // tn
    return pl.pallas_call(
        _gu_kernel,
        grid=(nb, m // tm),
        in_specs=[pl.BlockSpec((tm, k), lambda n, i: (i, 0)),
                  pl.BlockSpec((s, k), lambda n, i: (0, 0)),
                  pl.BlockSpec((None, k, tn), lambda n, i: (layer, 0, n)),
                  pl.BlockSpec((None, k, tn), lambda n, i: (layer, 0, n + nb))],
        out_specs=[pl.BlockSpec((tm, tn), lambda n, i: (i, n)),
                   pl.BlockSpec((s, tn), lambda n, i: (0, n))],
        out_shape=[jax.ShapeDtypeStruct((m, f), BF16), jax.ShapeDtypeStruct((s, f), BF16)],
        scratch_shapes=[pltpu.VMEM((k, tn), BF16), pltpu.VMEM((k, tn), BF16)],
        compiler_params=_params("arbitrary", "arbitrary"),
        name="gate_up",
    )(xp, xs, w_gu, w_gu)


def _ln_kernel(x_ref, m_ref, g_ref, b_ref, o_ref, ob_ref, *, alpha):
    y = alpha * x_ref[...] + m_ref[...]
    mu = jnp.mean(y, axis=-1, keepdims=True)
    yc = y - mu
    var = jnp.mean(yc * yc, axis=-1, keepdims=True)
    out = yc * lax.rsqrt(var + LN_EPS) * g_ref[...] + b_ref[...]
    o_ref[...] = out
    ob_ref[...] = out.astype(BF16)


def _res_ln(x, m, g, b, alpha, tm):
    rows, d = x.shape
    tm = min(tm, rows)
    assert rows % tm == 0
    row = pl.BlockSpec((tm, d), lambda i: (i, 0))
    vec = pl.BlockSpec((1, d), lambda i: (0, 0))
    return pl.pallas_call(
        functools.partial(_ln_kernel, alpha=alpha),
        grid=(rows // tm,),
        in_specs=[row, row, vec, vec],
        out_specs=[row, row],
        out_shape=[jax.ShapeDtypeStruct((rows, d), F32), jax.ShapeDtypeStruct((rows, d), BF16)],
        compiler_params=_params("arbitrary"),
        name="res_ln",
    )(x, m, g.reshape(1, d), b.reshape(1, d))


def _sb_tile(z, mask, u01, carry):
    neg_log_keep = jnp.maximum(z, 0.0) + jnp.log(1.0 + jnp.exp(-jnp.abs(z)))
    if mask is not None:
        neg_log_keep = jnp.where(mask, neg_log_keep, 0.0)
    right = _split_dot_right(neg_log_keep, u01, 2) + carry
    w = jnp.exp(z - neg_log_keep - right)
    if mask is not None:
        w = jnp.where(mask, w, 0.0)
    return w, carry + jnp.sum(neg_log_keep, axis=1, keepdims=True)


def _sb_prompt_kernel(bias_ref, q_ref, k_ref, v_ref, o_ref, *, t, hp, scale):
    g = pl.program_id(1)
    i = pl.program_id(2)
    d = q_ref.shape[1] // hp
    rows = _iota((t, t), 0)
    cols = _iota((t, t), 1)
    u01 = (rows > cols).astype(BF16)
    lanes = [slice(s * d, (s + 1) * d) for s in range(hp)]
    qs = [q_ref[:, ln].astype(BF16) for ln in lanes]
    biases = [bias_ref[g * hp + s] for s in range(hp)]

    def tiles(j, mask, state):
        keys = pl.ds(pl.multiple_of(j * t, t), t)
        zs = [lax.dot_general(qs[s], k_ref[keys, lanes[s]].astype(BF16), NT_DIMS, preferred_element_type=F32)
              * scale + biases[s] for s in range(hp)]
        ws = [_sb_tile(zs[s], mask, u01, state[s][0]) for s in range(hp)]
        return tuple(
            (ws[s][1], state[s][1] + jnp.dot(ws[s][0].astype(BF16), v_ref[keys, lanes[s]].astype(BF16),
                                             preferred_element_type=F32))
            for s in range(hp))

    state = tuple((jnp.zeros((t, 1), F32), jnp.zeros((t, d), F32)) for _ in range(hp))
    state = tiles(i, cols < rows, state)
    state = lax.fori_loop(0, i, lambda it, st: tiles(i - 1 - it, None, st), state)
    for s in range(hp):
        o_ref[:, lanes[s]] = state[s][1].astype(o_ref.dtype)


def _sb_prompt(q, k, v, sb_bias, nseq, t, hp):
    rows, width = q.shape
    heads = sb_bias.shape[0]
    d = width // heads
    seq = rows // nseq
    t = min(t, seq)
    nq = seq // t
    assert seq % t == 0 and d == LANE and heads % hp == 0
    qspec = pl.BlockSpec((t, hp * d), lambda b, g, i, bias: (b * nq + i, g))
    kvspec = pl.BlockSpec((seq, hp * d), lambda b, g, i, bias: (b, g))
    return pl.pallas_call(
        functools.partial(_sb_prompt_kernel, t=t, hp=hp, scale=d ** -0.5),
        grid_spec=pltpu.PrefetchScalarGridSpec(
            num_scalar_prefetch=1,
            grid=(nseq, heads // hp, nq),
            in_specs=[qspec, kvspec, kvspec],
            out_specs=qspec),
        out_shape=jax.ShapeDtypeStruct((rows, width), BF16),
        compiler_params=_params("arbitrary", "arbitrary", "arbitrary"),
        name="sb_prompt",
    )(sb_bias, q, k, v)


def _sb_decode_kernel(pt_ref, qbd_ref, bias_ref, knew_ref, vnew_ref, kc_ref, vc_ref, o_ref,
                      carry_scr, acc_scr, kbuf, vbuf, sem, *, heads, steps, scale):
    b = pl.program_id(0)
    p = pl.program_id(1)
    nq = qbd_ref.shape[1]
    nq_per_head = nq // heads
    width = qbd_ref.shape[2]
    d = width // heads
    pps = kbuf.shape[1]
    n_pages = (steps - 1) * pps

    def page_copies(step, slot):
        copies = []
        for j in range(pps):
            page = pt_ref[b, n_pages - step * pps + j]
            for h in range(heads):
                copies.append(pltpu.make_async_copy(kc_ref.at[page, :, h, :], kbuf.at[slot, j, h], sem.at[0, slot]))
                copies.append(pltpu.make_async_copy(vc_ref.at[page, :, h, :], vbuf.at[slot, j, h], sem.at[1, slot]))
        return copies

    @pl.when(p == 0)
    def _():
        carry_scr[...] = jnp.zeros_like(carry_scr)
        acc_scr[...] = jnp.zeros_like(acc_scr)

    @pl.when(p + 1 < steps)
    def _():
        for c in page_copies(p + 1, (p + 1) % 2):
            c.start()

    def step(kcat, vcat, mask):
        nk = kcat.shape[0]
        u01 = (_iota((nk, nk), 0) > _iota((nk, nk), 1)).astype(BF16)
        z = lax.dot_general(qbd_ref[0], kcat, NT_DIMS, preferred_element_type=F32) * scale + bias_ref[...]
        w, carry = _sb_tile(z, mask, u01, carry_scr[...])
        carry_scr[...] = carry
        acc_scr[...] += jnp.dot(w.astype(BF16), vcat, preferred_element_type=F32)

    @pl.when(p == 0)
    def _():
        qi = _div(_iota((nq, PAGE), 0), heads)
        kj = _iota((nq, PAGE), 1)
        step(knew_ref[0].astype(BF16), vnew_ref[0].astype(BF16), kj < qi)

    @pl.when(p > 0)
    def _():
        slot = p % 2
        for c in page_copies(p, slot):
            c.wait()

        def cat(buf):
            return jnp.concatenate(
                [jnp.concatenate([buf[slot, j, h].astype(BF16) for h in range(heads)], axis=1)
                 for j in range(pps)], axis=0)

        step(cat(kbuf), cat(vbuf), None)

    @pl.when(p == steps - 1)
    def _():
        acc = acc_scr[...]
        keep = _mod(_iota((nq, width), 0), heads) == _div(_iota((nq, width), 1), d)
        sel = jnp.where(keep, acc, 0.0)
        out_row = _iota((o_ref.shape[1], width), 0)
        out = jnp.zeros((o_ref.shape[1], width), F32)
        for t in range(nq_per_head):
            head_sum = jnp.sum(sel[t * heads:(t + 1) * heads], axis=0, keepdims=True)
            out = jnp.where(out_row == t, head_sum, out)
        o_ref[0] = out


def _sb_decode(q, k, v, cache_k, cache_v, page_table, sb_bias):
    nseq, n_pages = page_table.shape
    heads = sb_bias.shape[0]
    width = q.shape[1]
    d = width // heads
    n_new = q.shape[0] // nseq
    nq = n_new * heads
    assert cache_k.shape[1] == PAGE and d == LANE and n_new <= SUBLANE
    q4 = q.reshape(nseq, n_new, heads, 1, d)
    eye = jnp.eye(heads, dtype=F32).reshape(1, 1, heads, heads, 1)
    qbd = (q4 * eye).reshape(nseq, nq, width).astype(BF16)
    bias_col = jnp.tile(sb_bias.astype(F32), n_new).reshape(nq, 1)
    pad = ((0, 0), (0, PAGE - n_new), (0, 0))
    knew = jnp.pad(k.reshape(nseq, n_new, width), pad)
    vnew = jnp.pad(v.reshape(nseq, n_new, width), pad)
    pps = DECODE_PAGES_PER_STEP
    assert n_pages % pps == 0
    steps = n_pages // pps + 1

    seq3 = lambda b, p, pt: (b, 0, 0)
    out = pl.pallas_call(
        functools.partial(_sb_decode_kernel, heads=heads, steps=steps, scale=d ** -0.5),
        grid_spec=pltpu.PrefetchScalarGridSpec(
            num_scalar_prefetch=1,
            grid=(nseq, steps),
            in_specs=[pl.BlockSpec((1, nq, width), seq3),
                      pl.BlockSpec((nq, 1), lambda b, p, pt: (0, 0)),
                      pl.BlockSpec((1, PAGE, width), seq3),
                      pl.BlockSpec((1, PAGE, width), seq3),
                      pl.BlockSpec(memory_space=pl.ANY),
                      pl.BlockSpec(memory_space=pl.ANY)],
            out_specs=pl.BlockSpec((1, SUBLANE, width), seq3),
            scratch_shapes=[pltpu.VMEM((nq, 1), F32), pltpu.VMEM((nq, width), F32),
                            pltpu.VMEM((2, pps, heads, PAGE, d), F32), pltpu.VMEM((2, pps, heads, PAGE, d), F32),
                            pltpu.SemaphoreType.DMA((2, 2))]),
        out_shape=jax.ShapeDtypeStruct((nseq, SUBLANE, width), F32),
        compiler_params=_params("arbitrary", "arbitrary"),
        name="sb_decode",
    )(page_table, qbd, bias_col, knew, vnew, cache_k, cache_v)
    return out[:, :n_new].reshape(nseq * n_new, width).astype(BF16)


def _ssd_kernel(x_ref, b_ref, c_ref, px_ref, pb_ref, pc_ref, wx_ref, wb_ref, wc_ref,
                bx_ref, bb_ref, bc_ref, dt_ref, dtb_ref, alog_ref, dskip_ref, z_ref, nw_ref, h0_ref,
                y_ref, hout_ref, h_scr, ex_scr, eb_scr, ec_scr, *, valid_len, n_chunks, rep, hd, gp):
    c = pl.program_id(2)
    cl = x_ref.shape[0]
    xw = rep * hd
    n = h_scr.shape[1]

    @pl.when(c == 0)
    def _():
        h_scr[...] = h0_ref[0].reshape(gp * xw, n)
        ex_scr[0:SUBLANE, :] = px_ref[0]
        eb_scr[0:SUBLANE, :] = pb_ref[0]
        ec_scr[0:SUBLANE, :] = pc_ref[0]

    def conv_silu(e_scr, raw_ref, w_ref, bias_ref):
        e_scr[SUBLANE:SUBLANE + cl, :] = raw_ref[...]
        acc = bias_ref[...]
        for i in range(CONV_TAPS):
            lo = SUBLANE - (CONV_TAPS - 1) + i
            acc = acc + e_scr[lo:lo + cl, :] * w_ref[i:i + 1, :]
        e_scr[0:SUBLANE, :] = e_scr[cl:cl + SUBLANE, :]
        return _silu(acc)

    xc_all = conv_silu(ex_scr, x_ref, wx_ref, bx_ref)
    bm_all = conv_silu(eb_scr, b_ref, wb_ref, bb_ref)
    cm_all = conv_silu(ec_scr, c_ref, wc_ref, bc_ref)

    row = c * cl + _iota((cl, 1), 0)
    t_idx = _iota((cl, cl), 0)
    s_idx = _iota((cl, cl), 1)
    tril = t_idx >= s_idx
    tril16 = tril.astype(BF16)
    lane = _iota((1, xw), 1)
    in_head = [(lane >= r * hd) & (lane < (r + 1) * hd) for r in range(rep)]
    groups = range(gp)

    xc = [xc_all[:, s * xw:(s + 1) * xw] for s in groups]
    bm16 = [bm_all[:, s * n:(s + 1) * n].astype(BF16) for s in groups]
    cm16 = [cm_all[:, s * n:(s + 1) * n].astype(BF16) for s in groups]
    h = [h_scr[s * xw:(s + 1) * xw, :] for s in groups]

    def step_sizes(s):
        dtr = dt_ref[0, s] + dtb_ref[s]
        return jnp.where(row < valid_len, jnp.maximum(dtr, 0.0) + _softplus_neg_abs(dtr), 0.0)

    dt = [step_sizes(s) for s in groups]
    cum = [_split_dot_left(tril16, dt[s] * -jnp.exp(alog_ref[s]), 3) for s in groups]
    g = [lax.dot_general(cm16[s], bm16[s], NT_DIMS, preferred_element_type=F32) for s in groups]
    inter = [lax.dot_general(cm16[s], h[s].astype(BF16), NT_DIMS, preferred_element_type=F32) for s in groups]
    cum_t = [cum[s].T for s in groups]
    dt_t = [dt[s].T for s in groups]

    def head_terms(s):
        cum_last = cum[s][cl - 1:cl, :]
        decay_last = jnp.exp(cum_last)
        ws, xrs, hdecay = [], [], []
        e_all = jnp.zeros((cl, xw), F32)
        tail_all = jnp.zeros((cl, xw), F32)
        for r in range(rep):
            ccol = cum[s][:, r:r + 1]
            decay = jnp.exp(jnp.where(tril, ccol - cum_t[s][r:r + 1, :], -jnp.inf))
            ws.append((g[s] * decay * dt_t[s][r:r + 1, :]).astype(BF16))
            xrs.append(jnp.where(in_head[r], xc[s], 0.0).astype(BF16))
            e_all = jnp.where(in_head[r], jnp.exp(ccol), e_all)
            tail_all = jnp.where(in_head[r], jnp.exp(cum_last[:, r:r + 1] - ccol) * dt[s][:, r:r + 1], tail_all)
            hdecay.append(jnp.broadcast_to(decay_last[:, r:r + 1], (hd, n)))
        return (jnp.concatenate(ws, axis=1), jnp.concatenate(xrs, axis=0), e_all, tail_all,
                jnp.concatenate(hdecay, axis=0))

    terms = [head_terms(s) for s in groups]
    intra = [jnp.dot(terms[s][0], terms[s][1], preferred_element_type=F32) for s in groups]
    y = [intra[s] + inter[s] * terms[s][2] + xc[s] * dskip_ref[s] for s in groups]

    def transposed(xt):
        return jnp.concatenate([xt[:, j * LANE:(j + 1) * LANE].T for j in range(xw // LANE)], axis=0)

    xt_t = [transposed(xc[s] * terms[s][3]).astype(BF16) for s in groups]
    h_new = [h[s] * terms[s][4] + jnp.dot(xt_t[s], bm16[s], preferred_element_type=F32) for s in groups]
    for s in groups:
        h_scr[s * xw:(s + 1) * xw, :] = h_new[s]

    for s in groups:
        lanes = slice(s * xw, (s + 1) * xw)
        gated = y[s] * _silu(z_ref[:, lanes])
        ms = jnp.mean(gated * gated, axis=-1, keepdims=True)
        y_ref[:, lanes] = (gated * lax.rsqrt(ms + RMS_EPS) * nw_ref[:, lanes]).astype(y_ref.dtype)

    @pl.when(c == n_chunks - 1)
    def _():
        hout_ref[0] = h_scr[...].reshape(hout_ref.shape[1:])


def _ssd(xbc, dt_raw, z, conv_prev, h0, valid_len, conv_w, conv_b, dt_bias, a_log, d_skip, norm_w, groups):
    nseq, heads, hd, n = h0.shape
    rows, ch = xbc.shape
    seq = rows // nseq
    rep = heads // groups
    xw = rep * hd
    ssm_w = heads * hd
    cl = CHUNK
    nc = seq // cl
    gp = SSD_GROUPS_PER_STEP
    assert seq % cl == 0 and n == LANE and xw % LANE == 0 and ch == ssm_w + 2 * groups * n
    assert groups % gp == 0 and ssm_w % (gp * n) == 0
    b_off = ssm_w // (gp * n)
    c_off = b_off + groups // gp

    prev = jnp.pad(conv_prev, ((0, 0), (SUBLANE - (CONV_TAPS - 1), 0), (0, 0)))
    dt_g = dt_raw.reshape(nseq, seq, groups, rep).transpose(0, 2, 1, 3)
    dt_g = jnp.pad(dt_g, ((0, 0), (0, 0), (0, 0), (0, LANE - rep)))

    def lane_pad(p):
        return jnp.pad(p.astype(F32).reshape(groups, 1, rep), ((0, 0), (0, 0), (0, LANE - rep)))

    dskip_rep = jnp.repeat(d_skip.astype(F32), hd).reshape(groups, 1, xw)
    cb = conv_b.reshape(1, ch)

    def rows_at(width, off):
        return pl.BlockSpec((cl, width), lambda b, g, c: (b * nc + c, off + g))

    def prev_at(width, off):
        return pl.BlockSpec((1, SUBLANE, width), lambda b, g, c: (b, 0, off + g))

    def taps_at(nrows, width, off):
        return pl.BlockSpec((nrows, width), lambda b, g, c: (0, off + g))

    par = pl.BlockSpec((gp, 1, LANE), lambda b, g, c: (g, 0, 0))
    state = pl.BlockSpec((1, gp * rep, hd, n), lambda b, g, c: (b, g, 0, 0))
    xs_w, bc_w = gp * xw, gp * n
    y, h = pl.pallas_call(
        functools.partial(_ssd_kernel, valid_len=valid_len, n_chunks=nc, rep=rep, hd=hd, gp=gp),
        grid=(nseq, groups // gp, nc),
        in_specs=[rows_at(xs_w, 0), rows_at(bc_w, b_off), rows_at(bc_w, c_off),
                  prev_at(xs_w, 0), prev_at(bc_w, b_off), prev_at(bc_w, c_off),
                  taps_at(CONV_TAPS, xs_w, 0), taps_at(CONV_TAPS, bc_w, b_off), taps_at(CONV_TAPS, bc_w, c_off),
                  taps_at(1, xs_w, 0), taps_at(1, bc_w, b_off), taps_at(1, bc_w, c_off),
                  pl.BlockSpec((1, gp, cl, LANE), lambda b, g, c: (b, g, c, 0)),
                  par, par,
                  pl.BlockSpec((gp, 1, xw), lambda b, g, c: (g, 0, 0)),
                  rows_at(xs_w, 0),
                  taps_at(1, xs_w, 0),
                  state],
        out_specs=[rows_at(xs_w, 0), state],
        out_shape=[jax.ShapeDtypeStruct((rows, ssm_w), BF16), jax.ShapeDtypeStruct(h0.shape, F32)],
        scratch_shapes=[pltpu.VMEM((gp * xw, n), F32),
                        pltpu.VMEM((cl + SUBLANE, xs_w), F32),
                        pltpu.VMEM((cl + SUBLANE, bc_w), F32),
                        pltpu.VMEM((cl + SUBLANE, bc_w), F32)],
        compiler_params=_params("arbitrary", "arbitrary", "arbitrary"),
        name="ssd",
    )(xbc, xbc, xbc, prev, prev, prev, conv_w, conv_w, conv_w, cb, cb, cb,
      dt_g, lane_pad(dt_bias), lane_pad(a_log), dskip_rep, z, norm_w.reshape(1, ssm_w), h0)
    return y, h


def _pool_kernel(u_ref, prev_ref, w_ref, s_ref, o_ref, ext_scr, *, pos0, gd):
    i = pl.program_id(1)
    tr = u_ref.shape[0]

    @pl.when(i == 0)
    def _():
        ext_scr[0:POOL_HIST, :] = prev_ref[0]

    ext_scr[POOL_HIST:POOL_HIST + tr, :] = u_ref[...]
    pos = (pos0 + i * tr + _iota((tr, 1), 0)).astype(F32)
    for g, win in enumerate(POOL_WINDOWS):
        lanes = slice(g * gd, (g + 1) * gd)
        total = ext_scr[POOL_HIST:POOL_HIST + tr, lanes]
        for j in range(1, win):
            total = total + ext_scr[POOL_HIST - j:POOL_HIST - j + tr, lanes]
        count = jnp.minimum(pos + 1.0, float(win))
        dlt = total / count - u_ref[:, lanes]
        y = jnp.dot(dlt.astype(BF16), w_ref[g], preferred_element_type=F32)
        o_ref[:, lanes] = (y * s_ref[:, lanes]).astype(o_ref.dtype)
    ext_scr[0:POOL_HIST, :] = ext_scr[tr:tr + POOL_HIST, :]


def _pool(u, prev, pos0, w_pool16, pool_scale, tr):
    nseq = prev.shape[0]
    rows, width = u.shape
    seq = rows // nseq
    tr = min(tr, seq)
    nt = seq // tr
    ng, gd, _ = w_pool16.shape
    assert seq % tr == 0 and ng == len(POOL_WINDOWS) and ng * gd == width
    prev16 = jnp.pad(prev, ((0, 0), (POOL_HIST - prev.shape[1], 0), (0, 0)))
    return pl.pallas_call(
        functools.partial(_pool_kernel, pos0=pos0, gd=gd),
        grid=(nseq, nt),
        in_specs=[pl.BlockSpec((tr, width), lambda b, i: (b * nt + i, 0)),
                  pl.BlockSpec((1, POOL_HIST, width), lambda b, i: (b, 0, 0)),
                  pl.BlockSpec((ng, gd, gd), lambda b, i: (0, 0, 0)),
                  pl.BlockSpec((1, width), lambda b, i: (0, 0))],
        out_specs=pl.BlockSpec((tr, width), lambda b, i: (b * nt + i, 0)),
        out_shape=jax.ShapeDtypeStruct((rows, width), BF16),
        scratch_shapes=[pltpu.VMEM((tr + POOL_HIST, width), F32)],
        compiler_params=_params("arbitrary", "arbitrary"),
        name="pool",
    )(u, prev16, w_pool16, pool_scale.reshape(1, width))


def _gla_kernel(q_ref, k_ref, v_ref, r_ref, glr_ref, w2_ref, b2_ref, nw_ref, s0_ref,
                o_ref, sout_ref, s_scr, *, valid_len, n_chunks, scale, hp):
    c = pl.program_id(2)
    cl = q_ref.shape[0]
    dk = s_scr.shape[1]
    dv = s_scr.shape[2]
    heads = range(hp)

    @pl.when(c == 0)
    def _():
        s_scr[...] = s0_ref[0]

    row = c * cl + _iota((cl, 1), 0)
    valid = row < valid_len
    pre = jnp.dot(glr_ref[...], w2_ref[...], preferred_element_type=F32) + b2_ref[...]
    lg_all = (jnp.minimum(pre, 0.0) - _softplus_neg_abs(pre)) / GLA_TAU
    lg_all = jnp.where(valid, lg_all, 0.0)
    qs_all = q_ref[...] * scale
    k_all = jnp.where(valid, k_ref[...], 0.0)
    kl = [slice(h * dk, (h + 1) * dk) for h in heads]
    vl = [slice(h * dv, (h + 1) * dv) for h in heads]
    lg = [lg_all[:, kl[h]] for h in heads]
    qs = [qs_all[:, kl[h]] for h in heads]
    k = [k_all[:, kl[h]] for h in heads]
    v16 = [v_ref[:, vl[h]].astype(BF16) for h in heads]

    t_idx = _iota((cl, cl), 0)
    s_idx = _iota((cl, cl), 1)
    r_idx = _iota((cl, 1), 0)
    def masked_qk(qd, kd, mask):
        return jnp.where(mask, lax.dot_general(qd.astype(BF16), kd.astype(BF16), NT_DIMS,
                                               preferred_element_type=F32), 0.0)

    att = [masked_qk(qs[h], k[h], t_idx == s_idx) for h in heads]
    pre_b = lg
    suf_b = [jnp.zeros_like(lg[h]) for h in heads]
    tot_b = lg
    bsz = 1
    while bsz < cl:
        siblings = ((_div(t_idx, 2 * bsz) == _div(s_idx, 2 * bsz))
                    & (_mod(_div(t_idx, bsz), 2) == 1) & (_mod(_div(s_idx, bsz), 2) == 0))
        att = [att[h] + masked_qk(qs[h] * jnp.exp(pre_b[h]), k[h] * jnp.exp(suf_b[h]), siblings) for h in heads]
        is_right = _mod(_div(r_idx, bsz), 2) == 1
        left_tot = [pltpu.roll(tot_b[h], bsz, axis=0) for h in heads]
        right_tot = [pltpu.roll(tot_b[h], cl - bsz, axis=0) for h in heads]
        pre_b = [pre_b[h] + jnp.where(is_right, left_tot[h], 0.0) for h in heads]
        suf_b = [suf_b[h] + jnp.where(is_right, 0.0, right_tot[h]) for h in heads]
        tot_b = [tot_b[h] + jnp.where(is_right, left_tot[h], right_tot[h]) for h in heads]
        bsz *= 2

    s = [s_scr[h] for h in heads]
    qd = [(qs[h] * jnp.exp(pre_b[h])).astype(BF16) for h in heads]
    kd_t = [(k[h] * jnp.exp(suf_b[h])).T.astype(BF16) for h in heads]
    o = [jnp.dot(att[h].astype(BF16), v16[h], preferred_element_type=F32)
         + jnp.dot(qd[h], s[h].astype(BF16), preferred_element_type=F32) for h in heads]
    chunk_decay = [jnp.exp(tot_b[h].T) for h in heads]
    s_new = [s[h] * jnp.concatenate([chunk_decay[h]] * (dv // cl), axis=1)
             + jnp.dot(kd_t[h], v16[h], preferred_element_type=F32) for h in heads]
    for h in heads:
        s_scr[h] = s_new[h]

    for h in heads:
        ms = jnp.mean(o[h] * o[h], axis=-1, keepdims=True)
        o_ref[:, vl[h]] = (o[h] * lax.rsqrt(ms + RMS_EPS) * nw_ref[:, vl[h]]
                           * _silu(r_ref[:, vl[h]])).astype(o_ref.dtype)

    @pl.when(c == n_chunks - 1)
    def _():
        sout_ref[0] = s_scr[...]


def _gla(q, k, v, r, glr, w2, b2, norm_w, s0, valid_len):
    nseq, heads, dk, dv = s0.shape
    rows = q.shape[0]
    seq = rows // nseq
    cl = CHUNK
    nc = seq // cl
    hp = GLA_HEADS_PER_STEP
    assert seq % cl == 0 and dk == cl and dv % cl == 0 and glr.shape[1] == LANE and heads % hp == 0

    def rows_at(width, col):
        return pl.BlockSpec((cl, width), (lambda b, h, c: (b * nc + c, h)) if col else (lambda b, h, c: (b * nc + c, 0)))

    def head_at(nrows, width):
        return pl.BlockSpec((nrows, width), lambda b, h, c: (0, h))

    state = pl.BlockSpec((1, hp, dk, dv), lambda b, h, c: (b, h, 0, 0))
    kw, vw = hp * dk, hp * dv
    return pl.pallas_call(
        functools.partial(_gla_kernel, valid_len=valid_len, n_chunks=nc, scale=dk ** -0.5, hp=hp),
        grid=(nseq, heads // hp, nc),
        in_specs=[rows_at(kw, True), rows_at(kw, True), rows_at(vw, True), rows_at(vw, True),
                  rows_at(LANE, False), head_at(LANE, kw), head_at(1, kw), head_at(1, vw), state],
        out_specs=[rows_at(vw, True), state],
        out_shape=[jax.ShapeDtypeStruct((rows, heads * dv), BF16), jax.ShapeDtypeStruct(s0.shape, F32)],
        scratch_shapes=[pltpu.VMEM((hp, dk, dv), F32)],
        compiler_params=_params("arbitrary", "arbitrary", "arbitrary"),
        name="gla",
    )(q, k, v, r, glr, w2, b2.reshape(1, heads * dk), norm_w.reshape(1, heads * dv), s0)


def _pad_seq(a, nseq, seq_pad):
    n = a.shape[0] // nseq
    a3 = jnp.pad(a.reshape(nseq, n, a.shape[1]), ((0, 0), (0, seq_pad - n), (0, 0)))
    return a3.reshape(nseq * seq_pad, a.shape[1])


def _unpad_seq(a, nseq, n):
    return a.reshape(nseq, -1, a.shape[1])[:, :n].reshape(nseq * n, a.shape[1])


def _project(xp16, xs16, w, sizes, name):
    outs_p, outs_s = [], []
    off = 0
    w_nk = w.T
    for idx, size in enumerate(sizes):
        op, os_ = _proj([xp16], [xs16], w_nk, True, off, size, F32, PROJ_TM, PROJ_TN, f"{name}_{idx}")
        off += size
        outs_p.append(op)
        outs_s.append(os_)
    return outs_p, outs_s


def kernel(x_prompt, x_sample, cache_k, cache_v, page_table, state_conv, state_ssm, state_pool, state_gla, w_in0, sb_bias, conv_w, conv_b, dt_bias, a_log, d_skip, ssm_norm_w, w_out0, w_in1, w_pool, pool_scale, gla_w2, gla_b2, gla_norm_w, w_out1, ln_g, ln_b, w_gu, w_down):
    bp, seq, dm = x_prompt.shape
    bs, n_new, _ = x_sample.shape
    depth = ln_g.shape[0]
    alpha = (2 * depth) ** 0.25
    sb_heads, sb_d = cache_k.shape[2], cache_k.shape[3]
    sb_w = sb_heads * sb_d
    ssm_heads, ssm_hd, ssm_n = state_ssm.shape[1:]
    ssm_w = ssm_heads * ssm_hd
    conv_ch = state_conv.shape[2]
    groups = (conv_ch - ssm_w) // (2 * ssm_n)
    pool_w = state_pool.shape[2]
    gla_heads, gla_dk, gla_dv = state_gla.shape[1:]
    gla_kw, gla_vw = gla_heads * gla_dk, gla_heads * gla_dv
    gla_rank = gla_w2.shape[0]
    past_len = page_table.shape[1] * cache_k.shape[1]
    seq_pad = CHUNK

    xp = x_prompt.reshape(bp * seq, dm)
    xs = x_sample.reshape(bs * n_new, dm)
    xp16, xs16 = xp.astype(BF16), xs.astype(BF16)

    def mix_ffn_ln(xp, xs, mixed_p, mixed_s, w_out, layer):
        mp, ms = _proj(mixed_p, mixed_s, w_out, False, 0, dm, F32, PROJ_TM, PROJ_TN, f"out{layer}")
        xp, xp16 = _res_ln(xp, mp, ln_g[layer, 0], ln_b[layer, 0], alpha, LN_TM)
        xs, xs16 = _res_ln(xs, ms, ln_g[layer, 0], ln_b[layer, 0], alpha, LN_TM)
        hp, hs = _gate_up(xp16, xs16, w_gu, layer, PROJ_TM, GU_TN)
        w_down16 = w_down[layer].astype(BF16)
        fp = _mm(hp, w_down16, F32, DOWN_TM, DOWN_TN)
        fs = _mm(hs, w_down16, F32, DOWN_TM, DOWN_TN)
        xp, xp16 = _res_ln(xp, fp, ln_g[layer, 1], ln_b[layer, 1], alpha, LN_TM)
        xs, xs16 = _res_ln(xs, fs, ln_g[layer, 1], ln_b[layer, 1], alpha, LN_TM)
        return xp, xp16, xs, xs16

    split0 = (sb_w, sb_w, sb_w, ssm_w, conv_ch, ssm_heads)
    (qp, kp, vp, zp, xbcp, dtp), (qs, ks, vs, zs, xbcs, dts) = _project(xp16, xs16, w_in0, split0, "in0")
    dtp, dts = dtp[:, :ssm_heads], dts[:, :ssm_heads]

    oap = _sb_prompt(qp, kp, vp, sb_bias, bp, SB_T, SB_HEADS_PER_STEP)
    oas = _sb_decode(qs, ks, vs, cache_k, cache_v, page_table, sb_bias)

    ssd_w = (conv_w, conv_b, dt_bias, a_log, d_skip, ssm_norm_w, groups)
    yp, ssm_p = _ssd(xbcp, dtp, zp, jnp.zeros((bp, CONV_TAPS - 1, conv_ch), F32),
                     jnp.zeros((bp,) + state_ssm.shape[1:], F32), seq, *ssd_w)
    ys, ssm_s = _ssd(_pad_seq(xbcs, bs, seq_pad), _pad_seq(dts, bs, seq_pad), _pad_seq(zs, bs, seq_pad),
                     state_conv, state_ssm, n_new, *ssd_w)
    ys = _unpad_seq(ys, bs, n_new)

    xp, xp16, xs, xs16 = mix_ffn_ln(xp, xs, [oap, yp], [oas, ys], w_out0, 0)

    split1 = (pool_w, gla_kw, gla_kw, gla_vw, gla_vw, gla_rank)
    (up, gqp, gkp, gvp, grp, glrp), (us, gqs, gks, gvs, grs, glrs) = _project(xp16, xs16, w_in1, split1, "in1")

    w_pool16 = w_pool.astype(BF16)
    ocp = _pool(up, jnp.zeros((bp, POOL_HIST - 1, pool_w), F32), 0, w_pool16, pool_scale, POOL_TM)
    us8 = _pad_seq(us, bs, SUBLANE)
    ocs = _unpad_seq(_pool(us8, state_pool, past_len, w_pool16, pool_scale, SUBLANE), bs, n_new)

    w2_16 = jnp.pad(gla_w2, ((0, LANE - gla_rank), (0, 0))).astype(BF16)
    gla_w = (w2_16, gla_b2, gla_norm_w)
    odp, gla_p = _gla(gqp, gkp, gvp, grp, glrp.astype(BF16), *gla_w,
                      jnp.zeros((bp,) + state_gla.shape[1:], F32), seq)
    pad = lambda a: _pad_seq(a, bs, seq_pad)
    ods, gla_s = _gla(pad(gqs), pad(gks), pad(gvs), pad(grs), pad(glrs).astype(BF16), *gla_w, state_gla, n_new)
    ods = _unpad_seq(ods, bs, n_new)

    xp, xp16, xs, xs16 = mix_ffn_ln(xp, xs, [ocp, odp], [ocs, ods], w_out1, 1)

    hd4 = lambda a, nseq: a.reshape(nseq, -1, sb_heads, sb_d)
    conv_p = xbcp.reshape(bp, seq, conv_ch)[:, seq - (CONV_TAPS - 1):]
    conv_s = jnp.concatenate([state_conv, xbcs.reshape(bs, n_new, conv_ch)], axis=1)[:, n_new:]
    pool_p = up.reshape(bp, seq, pool_w)[:, seq - (POOL_HIST - 1):]
    pool_s = jnp.concatenate([state_pool, us.reshape(bs, n_new, pool_w)], axis=1)[:, n_new:]
    return (xp.reshape(bp, seq, dm), xs.reshape(bs, n_new, dm),
            hd4(kp, bp), hd4(vp, bp), hd4(ks, bs), hd4(vs, bs),
            conv_p, conv_s, ssm_p, ssm_s, pool_p, pool_s, gla_p, gla_s)
```

```python
import functools

import jax
import jax.numpy as jnp
from jax import lax
from jax.experimental import pallas as pl
from jax.experimental.pallas import tpu as pltpu

F32 = jnp.float32
BF16 = jnp.bfloat16

LANE = 128
SUBLANE = 8
VMEM_LIMIT = 56 * 1024 * 1024

LN_EPS = 1e-5
RMS_EPS = 1e-5
GLA_TAU = 16.0
PAGE = 128
POOL_WINDOWS = (2, 4, 8, 16)
POOL_HIST = 16
CONV_TAPS = 4
CHUNK = 128

PROJ_TM, PROJ_TN = 1024, 1024
PROJ_TC = 256
GU_TN = 256
DOWN_TM, DOWN_TN = 512, 512
LN_TM = 128
SB_T, SB_HEADS_PER_STEP = 256, 4
POOL_TM = 256
DECODE_PAGES_PER_STEP = 4
SSD_GROUPS_PER_STEP = 2
GLA_HEADS_PER_STEP = 2

NT_DIMS = (((1,), (1,)), ((), ()))


def _params(*sem):
    return pltpu.CompilerParams(dimension_semantics=sem, vmem_limit_bytes=VMEM_LIMIT)


def _sigmoid(x):
    return 1.0 / (1.0 + jnp.exp(-x))


def _silu(x):
    return x * _sigmoid(x)


def _softplus_neg_abs(x):
    return jnp.log1p(jnp.exp(-jnp.abs(x)))


def _iota(shape, dim):
    return lax.broadcasted_iota(jnp.int32, shape, dim)


def _div(x, c):
    return lax.shift_right_logical(x, (c.bit_length() - 1)) if c & (c - 1) == 0 else x // c


def _mod(x, c):
    return (x & (c - 1)) if c & (c - 1) == 0 else x % c


def _bf16_pieces(x, terms):
    pieces = [x.astype(BF16)]
    for _ in range(terms - 1):
        x = x - pieces[-1].astype(F32)
        pieces.append(x.astype(BF16))
    return pieces


def _add_all(terms):
    return functools.reduce(lambda a, b: a + b, terms)


def _split_dot_left(m01, x, terms):
    return _add_all([jnp.dot(m01, p, preferred_element_type=F32) for p in _bf16_pieces(x, terms)])


def _mm_kernel(x_ref, w_ref, o_ref):
    o_ref[...] = jnp.dot(x_ref[...], w_ref[...], preferred_element_type=F32).astype(o_ref.dtype)


def _mm(x, w, layer, out_dtype, tm, tn):
    m, k = x.shape
    n = w.shape[2]
    tm = min(tm, m)
    tn = min(tn, n)
    assert m % tm == 0 and n % tn == 0
    return pl.pallas_call(
        _mm_kernel,
        grid=(m // tm, n // tn),
        in_specs=[pl.BlockSpec((tm, k), lambda i, j: (i, 0)),
                  pl.BlockSpec((None, k, tn), lambda i, j: (layer, 0, j))],
        out_specs=pl.BlockSpec((tm, tn), lambda i, j: (i, j)),
        out_shape=jax.ShapeDtypeStruct((m, n), out_dtype),
        compiler_params=_params("arbitrary", "arbitrary"),
        name="down",
    )(x, w)


def _dot_parts(x_refs, w16, w_is_nk):
    terms, off = [], 0
    for x_ref in x_refs:
        k = x_ref.shape[1]
        if w_is_nk:
            terms.append(lax.dot_general(x_ref[...], w16[:, off:off + k], NT_DIMS, preferred_element_type=F32))
        else:
            terms.append(jnp.dot(x_ref[...], w16[off:off + k, :], preferred_element_type=F32))
        off += k
    return _add_all(terms)


def _proj_kernel(*refs, n_parts, w_is_nk, n_valid, n_cast):
    xps, xss = refs[:n_parts], refs[n_parts:2 * n_parts]
    w_ref, op_ref, os_ref, w16 = refs[2 * n_parts:]
    j = pl.program_id(1)
    tc = w_ref.shape[0 if w_is_nk else 1]

    for c in range(n_cast):
        @pl.when(j == c)
        def _(c=c):
            w = w_ref[...]
            if n_valid is not None:
                w = jnp.where(_iota(w.shape, 0 if w_is_nk else 1) < n_valid, w, 0.0)
            if w_is_nk:
                w16[c * tc:(c + 1) * tc, :] = w.astype(BF16)
            else:
                w16[:, c * tc:(c + 1) * tc] = w.astype(BF16)

    @pl.when(j == n_cast - 1)
    def _():
        os_ref[...] = _dot_parts(xss, w16, w_is_nk).astype(os_ref.dtype)

    @pl.when(j >= n_cast)
    def _():
        op_ref[...] = _dot_parts(xps, w16, w_is_nk).astype(op_ref.dtype)


def _col_tile(col0, ncols, tn_max):
    if ncols < LANE:
        assert col0 % LANE == 0
        return LANE
    tn = tn_max
    while ncols % tn or col0 % tn:
        tn //= 2
    assert tn >= LANE
    return tn


def _proj(xps, xss, w, w_is_nk, col0, ncols, out_dtype, tm, tn_max, name):
    m = xps[0].shape[0]
    s = xss[0].shape[0]
    k = w.shape[1] if w_is_nk else w.shape[0]
    tm = min(tm, m)
    tn = _col_tile(col0, ncols, tn_max)
    tc = min(PROJ_TC, tn)
    n_cast = tn // tc
    nb = pl.cdiv(ncols, tn)
    cb0 = (col0 // tn) * n_cast
    assert m % tm == 0 and sum(x.shape[1] for x in xps) == k

    def w_block(n, j):
        return cb0 + n * n_cast + jnp.minimum(j, n_cast - 1)

    def row_tile(j):
        return jnp.maximum(j - n_cast, 0)

    if w_is_nk:
        w_spec = pl.BlockSpec((tc, k), lambda n, j: (w_block(n, j), 0))
        w16 = pltpu.VMEM((tn, k), BF16)
    else:
        w_spec = pl.BlockSpec((k, tc), lambda n, j: (0, w_block(n, j)))
        w16 = pltpu.VMEM((k, tn), BF16)
    return pl.pallas_call(
        functools.partial(_proj_kernel, n_parts=len(xps), w_is_nk=w_is_nk,
                          n_valid=ncols if ncols < tn else None, n_cast=n_cast),
        grid=(nb, n_cast + m // tm),
        in_specs=([pl.BlockSpec((tm, x.shape[1]), lambda n, j: (row_tile(j), 0)) for x in xps]
                  + [pl.BlockSpec((s, x.shape[1]), lambda n, j: (0, 0)) for x in xss]
                  + [w_spec]),
        out_specs=[pl.BlockSpec((tm, tn), lambda n, j: (row_tile(j), n)),
                   pl.BlockSpec((s, tn), lambda n, j: (0, n))],
        out_shape=[jax.ShapeDtypeStruct((m, nb * tn), out_dtype), jax.ShapeDtypeStruct((s, nb * tn), out_dtype)],
        scratch_shapes=[w16],
        compiler_params=_params("arbitrary", "arbitrary"),
        name=name,
    )(*xps, *xss, w)


def _gu_kernel(xp_ref, xs_ref, wg_ref, wu_ref, op_ref, os_ref, wg16, wu16):
    def act(x_ref):
        x = x_ref[...]
        g = jnp.dot(x, wg16[...], preferred_element_type=F32)
        u = jnp.dot(x, wu16[...], preferred_element_type=F32)
        return (_silu(g) * u).astype(op_ref.dtype)

    @pl.when(pl.program_id(1) == 0)
    def _():
        wg16[...] = wg_ref[...].astype(BF16)
        wu16[...] = wu_ref[...].astype(BF16)
        os_ref[...] = act(xs_ref)

    op_ref[...] = act(xp_ref)


def _gate_up(xp, xs, w_gu, layer, tm, tn):
    m, k = xp.shape
    s = xs.shape[0]
    f = w_gu.shape[2] // 2
    tm = min(tm, m)
    tn = min(tn, f)
    assert m % tm == 0 and f % tn == 0
    nb = f // tn
    return pl.pallas_call(
        _gu_kernel,
        grid=(nb, m // tm),
        in_specs=[pl.BlockSpec((tm, k), lambda n, i: (i, 0)),
                  pl.BlockSpec((s, k), lambda n, i: (0, 0)),
                  pl.BlockSpec((None, k, tn), lambda n, i: (layer, 0, n)),
                  pl.BlockSpec((None, k, tn), lambda n, i: (layer, 0, n + nb))],
        out_specs=[pl.BlockSpec((tm, tn), lambda n, i: (i, n)),
                   pl.BlockSpec((s, tn), lambda n, i: (0, n))],
        out_shape=[jax.ShapeDtypeStruct((m, f), BF16), jax.ShapeDtypeStruct((s, f), BF16)],
        scratch_shapes=[pltpu.VMEM((k, tn), BF16), pltpu.VMEM((k, tn), BF16)],
        compiler_params=_params("arbitrary", "arbitrary"),
        name="gate_up",
    )(xp, xs, w_gu, w_gu)


def _ln_kernel(x_ref, m_ref, g_ref, b_ref, o_ref, ob_ref, *, alpha):
    y = alpha * x_ref[...] + m_ref[...]
    mu = jnp.mean(y, axis=-1, keepdims=True)
    yc = y - mu
    var = jnp.mean(yc * yc, axis=-1, keepdims=True)
    out = yc * lax.rsqrt(var + LN_EPS) * g_ref[...] + b_ref[...]
    o_ref[...] = out
    ob_ref[...] = out.astype(BF16)


def _res_ln(x, m, g, b, alpha, tm):
    rows, d = x.shape
    tm = min(tm, rows)
    assert rows % tm == 0
    row = pl.BlockSpec((tm, d), lambda i: (i, 0))
    vec = pl.BlockSpec((1, d), lambda i: (0, 0))
    return pl.pallas_call(
        functools.partial(_ln_kernel, alpha=alpha),
        grid=(rows // tm,),
        in_specs=[row, row, vec, vec],
        out_specs=[row, row],
        out_shape=[jax.ShapeDtypeStruct((rows, d), F32), jax.ShapeDtypeStruct((rows, d), BF16)],
        compiler_params=_params("arbitrary"),
        name="res_ln",
    )(x, m, g.reshape(1, d), b.reshape(1, d))


def _sb_tile(z, mask, u01, carry):
    neg_log_keep = jnp.maximum(z, 0.0) + jnp.log(1.0 + jnp.exp(-jnp.abs(z)))
    if mask is not None:
        neg_log_keep = jnp.where(mask, neg_log_keep, 0.0)
    right = jnp.dot(neg_log_keep.astype(BF16), u01, preferred_element_type=F32) + carry
    w = jnp.exp(z - neg_log_keep - right)
    if mask is not None:
        w = jnp.where(mask, w, 0.0)
    return w, carry + jnp.sum(neg_log_keep, axis=1, keepdims=True)


def _sb_prompt_kernel(bias_ref, q_ref, k_ref, v_ref, o_ref, *, t, hp, scale):
    g = pl.program_id(1)
    i = pl.program_id(2)
    d = q_ref.shape[1] // hp
    rows = _iota((t, t), 0)
    cols = _iota((t, t), 1)
    u01 = (rows > cols).astype(BF16)
    lanes = [slice(s * d, (s + 1) * d) for s in range(hp)]
    qs = [q_ref[:, ln].astype(BF16) for ln in lanes]
    biases = [bias_ref[g * hp + s] for s in range(hp)]

    def tiles(j, mask, state):
        keys = pl.ds(pl.multiple_of(j * t, t), t)
        zs = [lax.dot_general(qs[s], k_ref[keys, lanes[s]].astype(BF16), NT_DIMS, preferred_element_type=F32)
              * scale + biases[s] for s in range(hp)]
        ws = [_sb_tile(zs[s], mask, u01, state[s][0]) for s in range(hp)]
        return tuple(
            (ws[s][1], state[s][1] + jnp.dot(ws[s][0].astype(BF16), v_ref[keys, lanes[s]].astype(BF16),
                                             preferred_element_type=F32))
            for s in range(hp))

    state = tuple((jnp.zeros((t, 1), F32), jnp.zeros((t, d), F32)) for _ in range(hp))
    state = tiles(i, cols < rows, state)
    state = lax.fori_loop(0, i, lambda it, st: tiles(i - 1 - it, None, st), state)
    for s in range(hp):
        o_ref[:, lanes[s]] = state[s][1].astype(o_ref.dtype)


def _sb_prompt(q, k, v, sb_bias, nseq, t, hp):
    rows, width = q.shape
    heads = sb_bias.shape[0]
    d = width // heads
    seq = rows // nseq
    t = min(t, seq)
    nq = seq // t
    assert seq % t == 0 and d == LANE and heads % hp == 0
    qspec = pl.BlockSpec((t, hp * d), lambda b, g, i, bias: (b * nq + i, g))
    kvspec = pl.BlockSpec((seq, hp * d), lambda b, g, i, bias: (b, g))
    return pl.pallas_call(
        functools.partial(_sb_prompt_kernel, t=t, hp=hp, scale=d ** -0.5),
        grid_spec=pltpu.PrefetchScalarGridSpec(
            num_scalar_prefetch=1,
            grid=(nseq, heads // hp, nq),
            in_specs=[qspec, kvspec, kvspec],
            out_specs=qspec),
        out_shape=jax.ShapeDtypeStruct((rows, width), BF16),
        compiler_params=_params("arbitrary", "arbitrary", "arbitrary"),
        name="sb_prompt",
    )(sb_bias, q, k, v)


def _sb_decode_kernel(pt_ref, qbd_ref, bias_ref, knew_ref, vnew_ref, kc_ref, vc_ref, o_ref,
                      carry_scr, acc_scr, kbuf, vbuf, sem, *, heads, steps, scale):
    b = pl.program_id(0)
    p = pl.program_id(1)
    nq = qbd_ref.shape[1]
    nq_per_head = nq // heads
    width = qbd_ref.shape[2]
    d = width // heads
    pps = kbuf.shape[1]
    n_pages = (steps - 1) * pps

    def page_copies(step, slot):
        copies = []
        for j in range(pps):
            page = pt_ref[b, n_pages - step * pps + j]
            for h in range(heads):
                copies.append(pltpu.make_async_copy(kc_ref.at[page, :, h, :], kbuf.at[slot, j, h], sem.at[0, slot]))
                copies.append(pltpu.make_async_copy(vc_ref.at[page, :, h, :], vbuf.at[slot, j, h], sem.at[1, slot]))
        return copies

    @pl.when(p == 0)
    def _():
        carry_scr[...] = jnp.zeros_like(carry_scr)
        acc_scr[...] = jnp.zeros_like(acc_scr)

    @pl.when(p + 1 < steps)
    def _():
        for c in page_copies(p + 1, (p + 1) % 2):
            c.start()

    def step(kcat, vcat, mask):
        nk = kcat.shape[0]
        u01 = (_iota((nk, nk), 0) > _iota((nk, nk), 1)).astype(BF16)
        z = lax.dot_general(qbd_ref[0], kcat, NT_DIMS, preferred_element_type=F32) * scale + bias_ref[...]
        w, carry = _sb_tile(z, mask, u01, carry_scr[...])
        carry_scr[...] = carry
        acc_scr[...] += jnp.dot(w.astype(BF16), vcat, preferred_element_type=F32)

    @pl.when(p == 0)
    def _():
        qi = _div(_iota((nq, PAGE), 0), heads)
        kj = _iota((nq, PAGE), 1)
        step(knew_ref[0].astype(BF16), vnew_ref[0].astype(BF16), kj < qi)

    @pl.when(p > 0)
    def _():
        slot = p % 2
        for c in page_copies(p, slot):
            c.wait()

        def cat(buf):
            return jnp.concatenate(
                [jnp.concatenate([buf[slot, j, h].astype(BF16) for h in range(heads)], axis=1)
                 for j in range(pps)], axis=0)

        step(cat(kbuf), cat(vbuf), None)

    @pl.when(p == steps - 1)
    def _():
        acc = acc_scr[...]
        keep = _mod(_iota((nq, width), 0), heads) == _div(_iota((nq, width), 1), d)
        sel = jnp.where(keep, acc, 0.0)
        out_row = _iota((o_ref.shape[1], width), 0)
        out = jnp.zeros((o_ref.shape[1], width), F32)
        for t in range(nq_per_head):
            head_sum = jnp.sum(sel[t * heads:(t + 1) * heads], axis=0, keepdims=True)
            out = jnp.where(out_row == t, head_sum, out)
        o_ref[0] = out


def _sb_decode(q, k, v, cache_k, cache_v, page_table, sb_bias):
    nseq, n_pages = page_table.shape
    heads = sb_bias.shape[0]
    width = q.shape[1]
    d = width // heads
    n_new = q.shape[0] // nseq
    nq = n_new * heads
    assert cache_k.shape[1] == PAGE and d == LANE and n_new <= SUBLANE
    q4 = q.reshape(nseq, n_new, heads, 1, d)
    eye = jnp.eye(heads, dtype=F32).reshape(1, 1, heads, heads, 1)
    qbd = (q4 * eye).reshape(nseq, nq, width).astype(BF16)
    bias_col = jnp.tile(sb_bias.astype(F32), n_new).reshape(nq, 1)
    pad = ((0, 0), (0, PAGE - n_new), (0, 0))
    knew = jnp.pad(k.reshape(nseq, n_new, width), pad)
    vnew = jnp.pad(v.reshape(nseq, n_new, width), pad)
    pps = DECODE_PAGES_PER_STEP
    assert n_pages % pps == 0
    steps = n_pages // pps + 1

    seq3 = lambda b, p, pt: (b, 0, 0)
    out = pl.pallas_call(
        functools.partial(_sb_decode_kernel, heads=heads, steps=steps, scale=d ** -0.5),
        grid_spec=pltpu.PrefetchScalarGridSpec(
            num_scalar_prefetch=1,
            grid=(nseq, steps),
            in_specs=[pl.BlockSpec((1, nq, width), seq3),
                      pl.BlockSpec((nq, 1), lambda b, p, pt: (0, 0)),
                      pl.BlockSpec((1, PAGE, width), seq3),
                      pl.BlockSpec((1, PAGE, width), seq3),
                      pl.BlockSpec(memory_space=pl.ANY),
                      pl.BlockSpec(memory_space=pl.ANY)],
            out_specs=pl.BlockSpec((1, SUBLANE, width), seq3),
            scratch_shapes=[pltpu.VMEM((nq, 1), F32), pltpu.VMEM((nq, width), F32),
                            pltpu.VMEM((2, pps, heads, PAGE, d), F32), pltpu.VMEM((2, pps, heads, PAGE, d), F32),
                            pltpu.SemaphoreType.DMA((2, 2))]),
        out_shape=jax.ShapeDtypeStruct((nseq, SUBLANE, width), F32),
        compiler_params=_params("arbitrary", "arbitrary"),
        name="sb_decode",
    )(page_table, qbd, bias_col, knew, vnew, cache_k, cache_v)
    return out[:, :n_new].reshape(nseq * n_new, width).astype(BF16)


def _ssd_kernel(x_ref, b_ref, c_ref, px_ref, pb_ref, pc_ref, wx_ref, wb_ref, wc_ref,
                bx_ref, bb_ref, bc_ref, dt_ref, dtb_ref, alog_ref, dskip_ref, z_ref, nw_ref, h0_ref,
                y_ref, hout_ref, h_scr, ex_scr, eb_scr, ec_scr, *, valid_len, n_chunks, rep, hd, gp):
    c = pl.program_id(2)
    cl = x_ref.shape[0]
    xw = rep * hd
    n = h_scr.shape[1]

    @pl.when(c == 0)
    def _():
        h_scr[...] = h0_ref[0].reshape(gp * xw, n)
        ex_scr[0:SUBLANE, :] = px_ref[0]
        eb_scr[0:SUBLANE, :] = pb_ref[0]
        ec_scr[0:SUBLANE, :] = pc_ref[0]

    def conv_silu(e_scr, raw_ref, w_ref, bias_ref):
        e_scr[SUBLANE:SUBLANE + cl, :] = raw_ref[...]
        acc = bias_ref[...]
        for i in range(CONV_TAPS):
            lo = SUBLANE - (CONV_TAPS - 1) + i
            acc = acc + e_scr[lo:lo + cl, :] * w_ref[i:i + 1, :]
        e_scr[0:SUBLANE, :] = e_scr[cl:cl + SUBLANE, :]
        return _silu(acc)

    xc_all = conv_silu(ex_scr, x_ref, wx_ref, bx_ref)
    bm_all = conv_silu(eb_scr, b_ref, wb_ref, bb_ref)
    cm_all = conv_silu(ec_scr, c_ref, wc_ref, bc_ref)

    row = c * cl + _iota((cl, 1), 0)
    t_idx = _iota((cl, cl), 0)
    s_idx = _iota((cl, cl), 1)
    tril = t_idx >= s_idx
    tril16 = tril.astype(BF16)
    lane = _iota((1, xw), 1)
    in_head = [(lane >= r * hd) & (lane < (r + 1) * hd) for r in range(rep)]
    groups = range(gp)

    xc = [xc_all[:, s * xw:(s + 1) * xw] for s in groups]
    bm16 = [bm_all[:, s * n:(s + 1) * n].astype(BF16) for s in groups]
    cm16 = [cm_all[:, s * n:(s + 1) * n].astype(BF16) for s in groups]
    h = [h_scr[s * xw:(s + 1) * xw, :] for s in groups]

    def step_sizes(s):
        dtr = dt_ref[0, s] + dtb_ref[s]
        return jnp.where(row < valid_len, jnp.maximum(dtr, 0.0) + _softplus_neg_abs(dtr), 0.0)

    dt = [step_sizes(s) for s in groups]
    cum = [_split_dot_left(tril16, dt[s] * -jnp.exp(alog_ref[s]), 3) for s in groups]
    g = [lax.dot_general(cm16[s], bm16[s], NT_DIMS, preferred_element_type=F32) for s in groups]
    inter = [lax.dot_general(cm16[s], h[s].astype(BF16), NT_DIMS, preferred_element_type=F32) for s in groups]
    cum_t = [cum[s].T for s in groups]
    dt_t = [dt[s].T for s in groups]

    def head_terms(s):
        cum_last = cum[s][cl - 1:cl, :]
        decay_last = jnp.exp(cum_last)
        ws, xrs, hdecay = [], [], []
        e_all = jnp.zeros((cl, xw), F32)
        tail_all = jnp.zeros((cl, xw), F32)
        for r in range(rep):
            ccol = cum[s][:, r:r + 1]
            decay = jnp.exp(jnp.where(tril, ccol - cum_t[s][r:r + 1, :], -jnp.inf))
            ws.append((g[s] * decay * dt_t[s][r:r + 1, :]).astype(BF16))
            xrs.append(jnp.where(in_head[r], xc[s], 0.0).astype(BF16))
            e_all = jnp.where(in_head[r], jnp.exp(ccol), e_all)
            tail_all = jnp.where(in_head[r], jnp.exp(cum_last[:, r:r + 1] - ccol) * dt[s][:, r:r + 1], tail_all)
            hdecay.append(jnp.broadcast_to(decay_last[:, r:r + 1], (hd, n)))
        return (jnp.concatenate(ws, axis=1), jnp.concatenate(xrs, axis=0), e_all, tail_all,
                jnp.concatenate(hdecay, axis=0))

    terms = [head_terms(s) for s in groups]
    intra = [jnp.dot(terms[s][0], terms[s][1], preferred_element_type=F32) for s in groups]
    y = [intra[s] + inter[s] * terms[s][2] + xc[s] * dskip_ref[s] for s in groups]

    def transposed(xt):
        return jnp.concatenate([xt[:, j * LANE:(j + 1) * LANE].T for j in range(xw // LANE)], axis=0)

    xt_t = [transposed(xc[s] * terms[s][3]).astype(BF16) for s in groups]
    h_new = [h[s] * terms[s][4] + jnp.dot(xt_t[s], bm16[s], preferred_element_type=F32) for s in groups]
    for s in groups:
        h_scr[s * xw:(s + 1) * xw, :] = h_new[s]

    for s in groups:
        lanes = slice(s * xw, (s + 1) * xw)
        gated = y[s] * _silu(z_ref[:, lanes])
        ms = jnp.mean(gated * gated, axis=-1, keepdims=True)
        y_ref[:, lanes] = (gated * lax.rsqrt(ms + RMS_EPS) * nw_ref[:, lanes]).astype(y_ref.dtype)

    @pl.when(c == n_chunks - 1)
    def _():
        hout_ref[0] = h_scr[...].reshape(hout_ref.shape[1:])


def _ssd(xbc, dt_raw, z, conv_prev, h0, valid_len, conv_w, conv_b, dt_bias, a_log, d_skip, norm_w, groups):
    nseq, heads, hd, n = h0.shape
    rows, ch = xbc.shape
    seq = rows // nseq
    rep = heads // groups
    xw = rep * hd
    ssm_w = heads * hd
    cl = CHUNK
    nc = seq // cl
    gp = SSD_GROUPS_PER_STEP
    assert seq % cl == 0 and n == LANE and xw % LANE == 0 and ch == ssm_w + 2 * groups * n
    assert groups % gp == 0 and ssm_w % (gp * n) == 0
    b_off = ssm_w // (gp * n)
    c_off = b_off + groups // gp

    prev = jnp.pad(conv_prev, ((0, 0), (SUBLANE - (CONV_TAPS - 1), 0), (0, 0)))
    dt_g = dt_raw.reshape(nseq, seq, groups, rep).transpose(0, 2, 1, 3)
    dt_g = jnp.pad(dt_g, ((0, 0), (0, 0), (0, 0), (0, LANE - rep)))

    def lane_pad(p):
        return jnp.pad(p.astype(F32).reshape(groups, 1, rep), ((0, 0), (0, 0), (0, LANE - rep)))

    dskip_rep = jnp.repeat(d_skip.astype(F32), hd).reshape(groups, 1, xw)
    cb = conv_b.reshape(1, ch)

    def rows_at(width, off):
        return pl.BlockSpec((cl, width), lambda b, g, c: (b * nc + c, off + g))

    def prev_at(width, off):
        return pl.BlockSpec((1, SUBLANE, width), lambda b, g, c: (b, 0, off + g))

    def taps_at(nrows, width, off):
        return pl.BlockSpec((nrows, width), lambda b, g, c: (0, off + g))

    par = pl.BlockSpec((gp, 1, LANE), lambda b, g, c: (g, 0, 0))
    state = pl.BlockSpec((1, gp * rep, hd, n), lambda b, g, c: (b, g, 0, 0))
    xs_w, bc_w = gp * xw, gp * n
    y, h = pl.pallas_call(
        functools.partial(_ssd_kernel, valid_len=valid_len, n_chunks=nc, rep=rep, hd=hd, gp=gp),
        grid=(nseq, groups // gp, nc),
        in_specs=[rows_at(xs_w, 0), rows_at(bc_w, b_off), rows_at(bc_w, c_off),
                  prev_at(xs_w, 0), prev_at(bc_w, b_off), prev_at(bc_w, c_off),
                  taps_at(CONV_TAPS, xs_w, 0), taps_at(CONV_TAPS, bc_w, b_off), taps_at(CONV_TAPS, bc_w, c_off),
                  taps_at(1, xs_w, 0), taps_at(1, bc_w, b_off), taps_at(1, bc_w, c_off),
                  pl.BlockSpec((1, gp, cl, LANE), lambda b, g, c: (b, g, c, 0)),
                  par, par,
                  pl.BlockSpec((gp, 1, xw), lambda b, g, c: (g, 0, 0)),
                  rows_at(xs_w, 0),
                  taps_at(1, xs_w, 0),
                  state],
        out_specs=[rows_at(xs_w, 0), state],
        out_shape=[jax.ShapeDtypeStruct((rows, ssm_w), BF16), jax.ShapeDtypeStruct(h0.shape, F32)],
        scratch_shapes=[pltpu.VMEM((gp * xw, n), F32),
                        pltpu.VMEM((cl + SUBLANE, xs_w), F32),
                        pltpu.VMEM((cl + SUBLANE, bc_w), F32),
                        pltpu.VMEM((cl + SUBLANE, bc_w), F32)],
        compiler_params=_params("arbitrary", "arbitrary", "arbitrary"),
        name="ssd",
    )(xbc, xbc, xbc, prev, prev, prev, conv_w, conv_w, conv_w, cb, cb, cb,
      dt_g, lane_pad(dt_bias), lane_pad(a_log), dskip_rep, z, norm_w.reshape(1, ssm_w), h0)
    return y, h


def _pool_kernel(u_ref, prev_ref, w_ref, s_ref, o_ref, ext_scr, *, pos0, gd):
    i = pl.program_id(1)
    tr = u_ref.shape[0]

    @pl.when(i == 0)
    def _():
        ext_scr[0:POOL_HIST, :] = prev_ref[0]

    ext_scr[POOL_HIST:POOL_HIST + tr, :] = u_ref[...]
    pos = (pos0 + i * tr + _iota((tr, 1), 0)).astype(F32)
    for g, win in enumerate(POOL_WINDOWS):
        lanes = slice(g * gd, (g + 1) * gd)
        total = ext_scr[POOL_HIST:POOL_HIST + tr, lanes]
        for j in range(1, win):
            total = total + ext_scr[POOL_HIST - j:POOL_HIST - j + tr, lanes]
        count = jnp.minimum(pos + 1.0, float(win))
        dlt = total / count - u_ref[:, lanes]
        y = jnp.dot(dlt.astype(BF16), w_ref[g], preferred_element_type=F32)
        o_ref[:, lanes] = (y * s_ref[:, lanes]).astype(o_ref.dtype)
    ext_scr[0:POOL_HIST, :] = ext_scr[tr:tr + POOL_HIST, :]


def _pool(u, prev, pos0, w_pool16, pool_scale, tr):
    nseq = prev.shape[0]
    rows, width = u.shape
    seq = rows // nseq
    tr = min(tr, seq)
    nt = seq // tr
    ng, gd, _ = w_pool16.shape
    assert seq % tr == 0 and ng == len(POOL_WINDOWS) and ng * gd == width
    prev16 = jnp.pad(prev, ((0, 0), (POOL_HIST - prev.shape[1], 0), (0, 0)))
    return pl.pallas_call(
        functools.partial(_pool_kernel, pos0=pos0, gd=gd),
        grid=(nseq, nt),
        in_specs=[pl.BlockSpec((tr, width), lambda b, i: (b * nt + i, 0)),
                  pl.BlockSpec((1, POOL_HIST, width), lambda b, i: (b, 0, 0)),
                  pl.BlockSpec((ng, gd, gd), lambda b, i: (0, 0, 0)),
                  pl.BlockSpec((1, width), lambda b, i: (0, 0))],
        out_specs=pl.BlockSpec((tr, width), lambda b, i: (b * nt + i, 0)),
        out_shape=jax.ShapeDtypeStruct((rows, width), BF16),
        scratch_shapes=[pltpu.VMEM((tr + POOL_HIST, width), F32)],
        compiler_params=_params("arbitrary", "arbitrary"),
        name="pool",
    )(u, prev16, w_pool16, pool_scale.reshape(1, width))


def _gla_kernel(q_ref, k_ref, v_ref, r_ref, glr_ref, w2_ref, b2_ref, nw_ref, s0_ref,
                o_ref, sout_ref, s_scr, *, valid_len, n_chunks, scale, hp):
    c = pl.program_id(2)
    cl = q_ref.shape[0]
    dk = s_scr.shape[1]
    dv = s_scr.shape[2]
    heads = range(hp)

    @pl.when(c == 0)
    def _():
        s_scr[...] = s0_ref[0]

    row = c * cl + _iota((cl, 1), 0)
    valid = row < valid_len
    pre = jnp.dot(glr_ref[...], w2_ref[...], preferred_element_type=F32) + b2_ref[...]
    lg_all = (jnp.minimum(pre, 0.0) - _softplus_neg_abs(pre)) / GLA_TAU
    lg_all = jnp.where(valid, lg_all, 0.0)
    qs_all = q_ref[...] * scale
    k_all = jnp.where(valid, k_ref[...], 0.0)
    kl = [slice(h * dk, (h + 1) * dk) for h in heads]
    vl = [slice(h * dv, (h + 1) * dv) for h in heads]
    lg = [lg_all[:, kl[h]] for h in heads]
    qs = [qs_all[:, kl[h]] for h in heads]
    k = [k_all[:, kl[h]] for h in heads]
    v16 = [v_ref[:, vl[h]].astype(BF16) for h in heads]

    t_idx = _iota((cl, cl), 0)
    s_idx = _iota((cl, cl), 1)
    r_idx = _iota((cl, 1), 0)
    def masked_qk(qd, kd, mask):
        return jnp.where(mask, lax.dot_general(qd.astype(BF16), kd.astype(BF16), NT_DIMS,
                                               preferred_element_type=F32), 0.0)

    att = [masked_qk(qs[h], k[h], t_idx == s_idx) for h in heads]
    pre_b = lg
    suf_b = [jnp.zeros_like(lg[h]) for h in heads]
    tot_b = lg
    bsz = 1
    while bsz < cl:
        siblings = ((_div(t_idx, 2 * bsz) == _div(s_idx, 2 * bsz))
                    & (_mod(_div(t_idx, bsz), 2) == 1) & (_mod(_div(s_idx, bsz), 2) == 0))
        att = [att[h] + masked_qk(qs[h] * jnp.exp(pre_b[h]), k[h] * jnp.exp(suf_b[h]), siblings) for h in heads]
        is_right = _mod(_div(r_idx, bsz), 2) == 1
        left_tot = [pltpu.roll(tot_b[h], bsz, axis=0) for h in heads]
        right_tot = [pltpu.roll(tot_b[h], cl - bsz, axis=0) for h in heads]
        pre_b = [pre_b[h] + jnp.where(is_right, left_tot[h], 0.0) for h in heads]
        suf_b = [suf_b[h] + jnp.where(is_right, 0.0, right_tot[h]) for h in heads]
        tot_b = [tot_b[h] + jnp.where(is_right, left_tot[h], right_tot[h]) for h in heads]
        bsz *= 2

    s = [s_scr[h] for h in heads]
    qd = [(qs[h] * jnp.exp(pre_b[h])).astype(BF16) for h in heads]
    kd_t = [(k[h] * jnp.exp(suf_b[h])).T.astype(BF16) for h in heads]
    o = [jnp.dot(att[h].astype(BF16), v16[h], preferred_element_type=F32)
         + jnp.dot(qd[h], s[h].astype(BF16), preferred_element_type=F32) for h in heads]
    chunk_decay = [jnp.exp(tot_b[h].T) for h in heads]
    s_new = [s[h] * jnp.concatenate([chunk_decay[h]] * (dv // cl), axis=1)
             + jnp.dot(kd_t[h], v16[h], preferred_element_type=F32) for h in heads]
    for h in heads:
        s_scr[h] = s_new[h]

    for h in heads:
        ms = jnp.mean(o[h] * o[h], axis=-1, keepdims=True)
        o_ref[:, vl[h]] = (o[h] * lax.rsqrt(ms + RMS_EPS) * nw_ref[:, vl[h]]
                           * _silu(r_ref[:, vl[h]])).astype(o_ref.dtype)

    @pl.when(c == n_chunks - 1)
    def _():
        sout_ref[0] = s_scr[...]


def _gla(q, k, v, r, glr, w2, b2, norm_w, s0, valid_len):
    nseq, heads, dk, dv = s0.shape
    rows = q.shape[0]
    seq = rows // nseq
    cl = CHUNK
    nc = seq // cl
    hp = GLA_HEADS_PER_STEP
    assert seq % cl == 0 and dk == cl and dv % cl == 0 and glr.shape[1] == LANE and heads % hp == 0

    def rows_at(width, col):
        return pl.BlockSpec((cl, width), (lambda b, h, c: (b * nc + c, h)) if col else (lambda b, h, c: (b * nc + c, 0)))

    def head_at(nrows, width):
        return pl.BlockSpec((nrows, width), lambda b, h, c: (0, h))

    state = pl.BlockSpec((1, hp, dk, dv), lambda b, h, c: (b, h, 0, 0))
    kw, vw = hp * dk, hp * dv
    return pl.pallas_call(
        functools.partial(_gla_kernel, valid_len=valid_len, n_chunks=nc, scale=dk ** -0.5, hp=hp),
        grid=(nseq, heads // hp, nc),
        in_specs=[rows_at(kw, True), rows_at(kw, True), rows_at(vw, True), rows_at(vw, True),
                  rows_at(LANE, False), head_at(LANE, kw), head_at(1, kw), head_at(1, vw), state],
        out_specs=[rows_at(vw, True), state],
        out_shape=[jax.ShapeDtypeStruct((rows, heads * dv), BF16), jax.ShapeDtypeStruct(s0.shape, F32)],
        scratch_shapes=[pltpu.VMEM((hp, dk, dv), F32)],
        compiler_params=_params("arbitrary", "arbitrary", "arbitrary"),
        name="gla",
    )(q, k, v, r, glr, w2, b2.reshape(1, heads * dk), norm_w.reshape(1, heads * dv), s0)


def _pad_seq(a, nseq, seq_pad):
    n = a.shape[0] // nseq
    a3 = jnp.pad(a.reshape(nseq, n, a.shape[1]), ((0, 0), (0, seq_pad - n), (0, 0)))
    return a3.reshape(nseq * seq_pad, a.shape[1])


def _unpad_seq(a, nseq, n):
    return a.reshape(nseq, -1, a.shape[1])[:, :n].reshape(nseq * n, a.shape[1])


def _project(xp16, xs16, w, sizes, name):
    outs_p, outs_s = [], []
    off = 0
    w_nk = w.T
    for idx, size in enumerate(sizes):
        op, os_ = _proj([xp16], [xs16], w_nk, True, off, size, F32, PROJ_TM, PROJ_TN, f"{name}_{idx}")
        off += size
        outs_p.append(op)
        outs_s.append(os_)
    return outs_p, outs_s


def kernel(x_prompt, x_sample, cache_k, cache_v, page_table, state_conv, state_ssm, state_pool, state_gla, w_in0, sb_bias, conv_w, conv_b, dt_bias, a_log, d_skip, ssm_norm_w, w_out0, w_in1, w_pool, pool_scale, gla_w2, gla_b2, gla_norm_w, w_out1, ln_g, ln_b, w_gu, w_down):
    bp, seq, dm = x_prompt.shape
    bs, n_new, _ = x_sample.shape
    depth = ln_g.shape[0]
    alpha = (2 * depth) ** 0.25
    sb_heads, sb_d = cache_k.shape[2], cache_k.shape[3]
    sb_w = sb_heads * sb_d
    ssm_heads, ssm_hd, ssm_n = state_ssm.shape[1:]
    ssm_w = ssm_heads * ssm_hd
    conv_ch = state_conv.shape[2]
    groups = (conv_ch - ssm_w) // (2 * ssm_n)
    pool_w = state_pool.shape[2]
    gla_heads, gla_dk, gla_dv = state_gla.shape[1:]
    gla_kw, gla_vw = gla_heads * gla_dk, gla_heads * gla_dv
    gla_rank = gla_w2.shape[0]
    past_len = page_table.shape[1] * cache_k.shape[1]
    seq_pad = CHUNK

    xp = x_prompt.reshape(bp * seq, dm)
    xs = x_sample.reshape(bs * n_new, dm)
    xp16, xs16 = xp.astype(BF16), xs.astype(BF16)

    w_down16 = w_down.astype(BF16)

    def mix_ffn_ln(xp, xs, mixed_p, mixed_s, w_out, layer):
        mp, ms = _proj(mixed_p, mixed_s, w_out, False, 0, dm, F32, PROJ_TM, PROJ_TN, f"out{layer}")
        xp, xp16 = _res_ln(xp, mp, ln_g[layer, 0], ln_b[layer, 0], alpha, LN_TM)
        xs, xs16 = _res_ln(xs, ms, ln_g[layer, 0], ln_b[layer, 0], alpha, LN_TM)
        hp, hs = _gate_up(xp16, xs16, w_gu, layer, PROJ_TM, GU_TN)
        fp = _mm(hp, w_down16, layer, F32, DOWN_TM, DOWN_TN)
        fs = _mm(hs, w_down16, layer, F32, DOWN_TM, DOWN_TN)
        xp, xp16 = _res_ln(xp, fp, ln_g[layer, 1], ln_b[layer, 1], alpha, LN_TM)
        xs, xs16 = _res_ln(xs, fs, ln_g[layer, 1], ln_b[layer, 1], alpha, LN_TM)
        return xp, xp16, xs, xs16

    split0 = (sb_w, sb_w, sb_w, ssm_w, conv_ch, ssm_heads)
    (qp, kp, vp, zp, xbcp, dtp), (qs, ks, vs, zs, xbcs, dts) = _project(xp16, xs16, w_in0, split0, "in0")
    dtp, dts = dtp[:, :ssm_heads], dts[:, :ssm_heads]

    oap = _sb_prompt(qp, kp, vp, sb_bias, bp, SB_T, SB_HEADS_PER_STEP)
    oas = _sb_decode(qs, ks, vs, cache_k, cache_v, page_table, sb_bias)

    ssd_w = (conv_w, conv_b, dt_bias, a_log, d_skip, ssm_norm_w, groups)
    yp, ssm_p = _ssd(xbcp, dtp, zp, jnp.zeros((bp, CONV_TAPS - 1, conv_ch), F32),
                     jnp.zeros((bp,) + state_ssm.shape[1:], F32), seq, *ssd_w)
    ys, ssm_s = _ssd(_pad_seq(xbcs, bs, seq_pad), _pad_seq(dts, bs, seq_pad), _pad_seq(zs, bs, seq_pad),
                     state_conv, state_ssm, n_new, *ssd_w)
    ys = _unpad_seq(ys, bs, n_new)

    xp, xp16, xs, xs16 = mix_ffn_ln(xp, xs, [oap, yp], [oas, ys], w_out0, 0)

    split1 = (pool_w, gla_kw, gla_kw, gla_vw, gla_vw, gla_rank)
    (up, gqp, gkp, gvp, grp, glrp), (us, gqs, gks, gvs, grs, glrs) = _project(xp16, xs16, w_in1, split1, "in1")

    w_pool16 = w_pool.astype(BF16)
    ocp = _pool(up, jnp.zeros((bp, POOL_HIST - 1, pool_w), F32), 0, w_pool16, pool_scale, POOL_TM)
    us8 = _pad_seq(us, bs, SUBLANE)
    ocs = _unpad_seq(_pool(us8, state_pool, past_len, w_pool16, pool_scale, SUBLANE), bs, n_new)

    w2_16 = jnp.pad(gla_w2, ((0, LANE - gla_rank), (0, 0))).astype(BF16)
    gla_w = (w2_16, gla_b2, gla_norm_w)
    odp, gla_p = _gla(gqp, gkp, gvp, grp, glrp.astype(BF16), *gla_w,
                      jnp.zeros((bp,) + state_gla.shape[1:], F32), seq)
    pad = lambda a: _pad_seq(a, bs, seq_pad)
    ods, gla_s = _gla(pad(gqs), pad(gks), pad(gvs), pad(grs), pad(glrs).astype(BF16), *gla_w, state_gla, n_new)
    ods = _unpad_seq(ods, bs, n_new)

    xp, xp16, xs, xs16 = mix_ffn_ln(xp, xs, [ocp, odp], [ocs, ods], w_out1, 1)

    hd4 = lambda a, nseq: a.reshape(nseq, -1, sb_heads, sb_d)
    conv_p = xbcp.reshape(bp, seq, conv_ch)[:, seq - (CONV_TAPS - 1):]
    conv_s = jnp.concatenate([state_conv, xbcs.reshape(bs, n_new, conv_ch)], axis=1)[:, n_new:]
    pool_p = up.reshape(bp, seq, pool_w)[:, seq - (POOL_HIST - 1):]
    pool_s = jnp.concatenate([state_pool, us.reshape(bs, n_new, pool_w)], axis=1)[:, n_new:]
    return (xp.reshape(bp, seq, dm), xs.reshape(bs, n_new, dm),
            hd4(kp, bp), hd4(vp, bp), hd4(ks, bs), hd4(vs, bs),
            conv_p, conv_s, ssm_p, ssm_s, pool_p, pool_s, gla_p, gla_s)
```

```python
import functools

import jax
import jax.numpy as jnp
from jax import lax
from jax.experimental import pallas as pl
from jax.experimental.pallas import tpu as pltpu

F32 = jnp.float32
BF16 = jnp.bfloat16

LANE = 128
SUBLANE = 8
VMEM_LIMIT = 56 * 1024 * 1024

LN_EPS = 1e-5
RMS_EPS = 1e-5
GLA_TAU = 16.0
PAGE = 128
POOL_WINDOWS = (2, 4, 8, 16)
POOL_HIST = 16
CONV_TAPS = 4
CHUNK = 128

PROJ_TM, PROJ_TN = 1024, 1024
PROJ_TC = 256
GU_TN = 256
DOWN_TM, DOWN_TN = 512, 512
LN_TM = 256
SB_T, SB_HEADS_PER_STEP = 256, 4
POOL_TM = 256
DECODE_PAGES_PER_STEP = 4
SSD_GROUPS_PER_STEP = 2
GLA_HEADS_PER_STEP = 2

NT_DIMS = (((1,), (1,)), ((), ()))


def _params(*sem):
    return pltpu.CompilerParams(dimension_semantics=sem, vmem_limit_bytes=VMEM_LIMIT)


def _sigmoid(x):
    return 1.0 / (1.0 + jnp.exp(-x))


def _silu(x):
    return x * _sigmoid(x)


def _softplus_neg_abs(x):
    return jnp.log1p(jnp.exp(-jnp.abs(x)))


def _iota(shape, dim):
    return lax.broadcasted_iota(jnp.int32, shape, dim)


def _div(x, c):
    return lax.shift_right_logical(x, (c.bit_length() - 1)) if c & (c - 1) == 0 else x // c


def _mod(x, c):
    return (x & (c - 1)) if c & (c - 1) == 0 else x % c


def _bf16_pieces(x, terms):
    pieces = [x.astype(BF16)]
    for _ in range(terms - 1):
        x = x - pieces[-1].astype(F32)
        pieces.append(x.astype(BF16))
    return pieces


def _add_all(terms):
    return functools.reduce(lambda a, b: a + b, terms)


def _split_dot_left(m01, x, terms):
    return _add_all([jnp.dot(m01, p, preferred_element_type=F32) for p in _bf16_pieces(x, terms)])


def _mm_res_kernel(x_ref, w_ref, r_ref, o_ref, *, alpha):
    o_ref[...] = alpha * r_ref[...] + jnp.dot(x_ref[...], w_ref[...], preferred_element_type=F32)


def _mm_res(x, w, res, alpha, tm, tn):
    m, k = x.shape
    n = w.shape[1]
    tm = min(tm, m)
    tn = min(tn, n)
    assert m % tm == 0 and n % tn == 0
    tile = pl.BlockSpec((tm, tn), lambda i, j: (i, j))
    return pl.pallas_call(
        functools.partial(_mm_res_kernel, alpha=alpha),
        grid=(m // tm, n // tn),
        in_specs=[pl.BlockSpec((tm, k), lambda i, j: (i, 0)),
                  pl.BlockSpec((k, tn), lambda i, j: (0, j)),
                  tile],
        out_specs=tile,
        out_shape=jax.ShapeDtypeStruct((m, n), F32),
        compiler_params=_params("arbitrary", "arbitrary"),
        name="down",
    )(x, w, res)


def _dot_parts(x_refs, w16, w_is_nk):
    terms, off = [], 0
    for x_ref in x_refs:
        k = x_ref.shape[1]
        if w_is_nk:
            terms.append(lax.dot_general(x_ref[...], w16[:, off:off + k], NT_DIMS, preferred_element_type=F32))
        else:
            terms.append(jnp.dot(x_ref[...], w16[off:off + k, :], preferred_element_type=F32))
        off += k
    return _add_all(terms)


def _proj_kernel(*refs, n_parts, w_is_nk, n_valid, n_cast):
    xps, xss = refs[:n_parts], refs[n_parts:2 * n_parts]
    w_ref, op_ref, os_ref, w16 = refs[2 * n_parts:]
    j = pl.program_id(1)
    tc = w_ref.shape[0 if w_is_nk else 1]

    for c in range(n_cast):
        @pl.when(j == c)
        def _(c=c):
            w = w_ref[...]
            if n_valid is not None:
                w = jnp.where(_iota(w.shape, 0 if w_is_nk else 1) < n_valid, w, 0.0)
            if w_is_nk:
                w16[c * tc:(c + 1) * tc, :] = w.astype(BF16)
            else:
                w16[:, c * tc:(c + 1) * tc] = w.astype(BF16)

    @pl.when(j == n_cast - 1)
    def _():
        os_ref[...] = _dot_parts(xss, w16, w_is_nk).astype(os_ref.dtype)

    @pl.when(j >= n_cast)
    def _():
        op_ref[...] = _dot_parts(xps, w16, w_is_nk).astype(op_ref.dtype)


def _col_tile(col0, ncols, tn_max):
    if ncols < LANE:
        assert col0 % LANE == 0
        return LANE
    tn = tn_max
    while ncols % tn or col0 % tn:
        tn //= 2
    assert tn >= LANE
    return tn


def _proj(xps, xss, w, w_is_nk, col0, ncols, out_dtype, tm, tn_max, name):
    m = xps[0].shape[0]
    s = xss[0].shape[0]
    k = w.shape[1] if w_is_nk else w.shape[0]
    tm = min(tm, m)
    tn = _col_tile(col0, ncols, tn_max)
    tc = min(PROJ_TC, tn)
    n_cast = tn // tc
    nb = pl.cdiv(ncols, tn)
    cb0 = (col0 // tn) * n_cast
    assert m % tm == 0 and sum(x.shape[1] for x in xps) == k

    def w_block(n, j):
        return cb0 + n * n_cast + jnp.minimum(j, n_cast - 1)

    def row_tile(j):
        return jnp.maximum(j - n_cast, 0)

    if w_is_nk:
        w_spec = pl.BlockSpec((tc, k), lambda n, j: (w_block(n, j), 0))
        w16 = pltpu.VMEM((tn, k), BF16)
    else:
        w_spec = pl.BlockSpec((k, tc), lambda n, j: (0, w_block(n, j)))
        w16 = pltpu.VMEM((k, tn), BF16)
    return pl.pallas_call(
        functools.partial(_proj_kernel, n_parts=len(xps), w_is_nk=w_is_nk,
                          n_valid=ncols if ncols < tn else None, n_cast=n_cast),
        grid=(nb, n_cast + m // tm),
        in_specs=([pl.BlockSpec((tm, x.shape[1]), lambda n, j: (row_tile(j), 0)) for x in xps]
                  + [pl.BlockSpec((s, x.shape[1]), lambda n, j: (0, 0)) for x in xss]
                  + [w_spec]),
        out_specs=[pl.BlockSpec((tm, tn), lambda n, j: (row_tile(j), n)),
                   pl.BlockSpec((s, tn), lambda n, j: (0, n))],
        out_shape=[jax.ShapeDtypeStruct((m, nb * tn), out_dtype), jax.ShapeDtypeStruct((s, nb * tn), out_dtype)],
        scratch_shapes=[w16],
        compiler_params=_params("arbitrary", "arbitrary"),
        name=name,
    )(*xps, *xss, w)


def _gu_kernel(xp_ref, xs_ref, wg_ref, wu_ref, wd_ref, op_ref, os_ref, wd16_ref, wg16, wu16):
    def act(x_ref):
        x = x_ref[...]
        g = jnp.dot(x, wg16[...], preferred_element_type=F32)
        u = jnp.dot(x, wu16[...], preferred_element_type=F32)
        return (_silu(g) * u).astype(op_ref.dtype)

    @pl.when(pl.program_id(1) == 0)
    def _():
        wg16[...] = wg_ref[...].astype(BF16)
        wu16[...] = wu_ref[...].astype(BF16)
        wd16_ref[...] = wd_ref[...].astype(BF16)
        os_ref[...] = act(xs_ref)

    op_ref[...] = act(xp_ref)


def _gate_up(xp, xs, w_gu, w_down, layer, tm, tn):
    m, k = xp.shape
    s = xs.shape[0]
    f = w_gu.shape[2] // 2
    dm = w_down.shape[2]
    tm = min(tm, m)
    tn = min(tn, f)
    assert m % tm == 0 and f % tn == 0 and w_down.shape[1] == f
    nb = f // tn
    return pl.pallas_call(
        _gu_kernel,
        grid=(nb, m // tm),
        in_specs=[pl.BlockSpec((tm, k), lambda n, i: (i, 0)),
                  pl.BlockSpec((s, k), lambda n, i: (0, 0)),
                  pl.BlockSpec((None, k, tn), lambda n, i: (layer, 0, n)),
                  pl.BlockSpec((None, k, tn), lambda n, i: (layer, 0, n + nb)),
                  pl.BlockSpec((None, tn, dm), lambda n, i: (layer, n, 0))],
        out_specs=[pl.BlockSpec((tm, tn), lambda n, i: (i, n)),
                   pl.BlockSpec((s, tn), lambda n, i: (0, n)),
                   pl.BlockSpec((tn, dm), lambda n, i: (n, 0))],
        out_shape=[jax.ShapeDtypeStruct((m, f), BF16), jax.ShapeDtypeStruct((s, f), BF16),
                   jax.ShapeDtypeStruct((f, dm), BF16)],
        scratch_shapes=[pltpu.VMEM((k, tn), BF16), pltpu.VMEM((k, tn), BF16)],
        compiler_params=_params("arbitrary", "arbitrary"),
        name="gate_up",
    )(xp, xs, w_gu, w_gu, w_down)


def _layer_norm_store(y, g_ref, b_ref, o_ref, ob_ref):
    mu = jnp.mean(y, axis=-1, keepdims=True)
    yc = y - mu
    var = jnp.mean(yc * yc, axis=-1, keepdims=True)
    out = yc * lax.rsqrt(var + LN_EPS) * g_ref[...] + b_ref[...]
    o_ref[...] = out
    ob_ref[...] = out.astype(BF16)


def _res_ln_kernel(x_ref, m_ref, g_ref, b_ref, o_ref, ob_ref, *, alpha):
    _layer_norm_store(alpha * x_ref[...] + m_ref[...], g_ref, b_ref, o_ref, ob_ref)


def _ln_kernel(y_ref, g_ref, b_ref, o_ref, ob_ref):
    _layer_norm_store(y_ref[...], g_ref, b_ref, o_ref, ob_ref)


def _ln(y, g, b, tm, res=None, alpha=None):
    rows, d = y.shape
    tm = min(tm, rows)
    assert rows % tm == 0
    row = pl.BlockSpec((tm, d), lambda i: (i, 0))
    vec = pl.BlockSpec((1, d), lambda i: (0, 0))
    if res is None:
        body, acts = _ln_kernel, (y,)
    else:
        body, acts = functools.partial(_res_ln_kernel, alpha=alpha), (res, y)
    return pl.pallas_call(
        body,
        grid=(rows // tm,),
        in_specs=[row] * len(acts) + [vec, vec],
        out_specs=[row, row],
        out_shape=[jax.ShapeDtypeStruct((rows, d), F32), jax.ShapeDtypeStruct((rows, d), BF16)],
        compiler_params=_params("arbitrary"),
        name="layer_norm",
    )(*acts, g.reshape(1, d), b.reshape(1, d))


LOG2_E = 1.4426950408889634


def _sb_tile(z2, mask, u01, carry):
    neg_log_keep = jnp.maximum(z2, 0.0) + jnp.log2(1.0 + jnp.exp2(-jnp.abs(z2)))
    if mask is not None:
        neg_log_keep = jnp.where(mask, neg_log_keep, 0.0)
    right = jnp.dot(neg_log_keep.astype(BF16), u01, preferred_element_type=F32) + carry
    w = jnp.exp2(z2 - neg_log_keep - right)
    if mask is not None:
        w = jnp.where(mask, w, 0.0)
    return w, carry + jnp.sum(neg_log_keep, axis=1, keepdims=True)


def _sb_prompt_kernel(bias_ref, q_ref, k_ref, v_ref, o_ref, *, t, hp, scale):
    g = pl.program_id(1)
    i = pl.program_id(2)
    d = q_ref.shape[1] // hp
    rows = _iota((t, t), 0)
    cols = _iota((t, t), 1)
    u01 = (rows > cols).astype(BF16)
    lanes = [slice(s * d, (s + 1) * d) for s in range(hp)]
    qs = [q_ref[:, ln].astype(BF16) for ln in lanes]
    scale2 = scale * LOG2_E
    biases2 = [bias_ref[g * hp + s] * LOG2_E for s in range(hp)]

    def tiles(j, mask, state):
        keys = pl.ds(pl.multiple_of(j * t, t), t)
        zs = [lax.dot_general(qs[s], k_ref[keys, lanes[s]].astype(BF16), NT_DIMS, preferred_element_type=F32)
              * scale2 + biases2[s] for s in range(hp)]
        ws = [_sb_tile(zs[s], mask, u01, state[s][0]) for s in range(hp)]
        return tuple(
            (ws[s][1], state[s][1] + jnp.dot(ws[s][0].astype(BF16), v_ref[keys, lanes[s]].astype(BF16),
                                             preferred_element_type=F32))
            for s in range(hp))

    state = tuple((jnp.zeros((t, 1), F32), jnp.zeros((t, d), F32)) for _ in range(hp))
    state = tiles(i, cols < rows, state)
    state = lax.fori_loop(0, i, lambda it, st: tiles(i - 1 - it, None, st), state)
    for s in range(hp):
        o_ref[:, lanes[s]] = state[s][1].astype(o_ref.dtype)


def _sb_prompt(q, k, v, sb_bias, nseq, t, hp):
    rows, width = q.shape
    heads = sb_bias.shape[0]
    d = width // heads
    seq = rows // nseq
    t = min(t, seq)
    nq = seq // t
    assert seq % t == 0 and d == LANE and heads % hp == 0
    qspec = pl.BlockSpec((t, hp * d), lambda b, g, i, bias: (b * nq + i, g))
    kvspec = pl.BlockSpec((seq, hp * d), lambda b, g, i, bias: (b, g))
    return pl.pallas_call(
        functools.partial(_sb_prompt_kernel, t=t, hp=hp, scale=d ** -0.5),
        grid_spec=pltpu.PrefetchScalarGridSpec(
            num_scalar_prefetch=1,
            grid=(nseq, heads // hp, nq),
            in_specs=[qspec, kvspec, kvspec],
            out_specs=qspec),
        out_shape=jax.ShapeDtypeStruct((rows, width), BF16),
        compiler_params=_params("arbitrary", "arbitrary", "arbitrary"),
        name="sb_prompt",
    )(sb_bias, q, k, v)


def _sb_decode_kernel(pt_ref, qbd_ref, bias_ref, knew_ref, vnew_ref, kc_ref, vc_ref, o_ref,
                      carry_scr, acc_scr, kbuf, vbuf, sem, *, heads, steps, scale):
    b = pl.program_id(0)
    p = pl.program_id(1)
    nq = qbd_ref.shape[1]
    nq_per_head = nq // heads
    width = qbd_ref.shape[2]
    d = width // heads
    pps = kbuf.shape[1]
    n_pages = (steps - 1) * pps

    def page_copies(step, slot):
        copies = []
        for j in range(pps):
            page = pt_ref[b, n_pages - step * pps + j]
            for h in range(heads):
                copies.append(pltpu.make_async_copy(kc_ref.at[page, :, h, :], kbuf.at[slot, j, h], sem.at[0, slot]))
                copies.append(pltpu.make_async_copy(vc_ref.at[page, :, h, :], vbuf.at[slot, j, h], sem.at[1, slot]))
        return copies

    @pl.when(p == 0)
    def _():
        carry_scr[...] = jnp.zeros_like(carry_scr)
        acc_scr[...] = jnp.zeros_like(acc_scr)

    @pl.when(p + 1 < steps)
    def _():
        for c in page_copies(p + 1, (p + 1) % 2):
            c.start()

    def step(kcat, vcat, mask):
        nk = kcat.shape[0]
        u01 = (_iota((nk, nk), 0) > _iota((nk, nk), 1)).astype(BF16)
        z2 = (lax.dot_general(qbd_ref[0], kcat, NT_DIMS, preferred_element_type=F32) * (scale * LOG2_E)
              + bias_ref[...] * LOG2_E)
        w, carry = _sb_tile(z2, mask, u01, carry_scr[...])
        carry_scr[...] = carry
        acc_scr[...] += jnp.dot(w.astype(BF16), vcat, preferred_element_type=F32)

    @pl.when(p == 0)
    def _():
        qi = _div(_iota((nq, PAGE), 0), heads)
        kj = _iota((nq, PAGE), 1)
        step(knew_ref[0].astype(BF16), vnew_ref[0].astype(BF16), kj < qi)

    @pl.when(p > 0)
    def _():
        slot = p % 2
        for c in page_copies(p, slot):
            c.wait()

        def cat(buf):
            return jnp.concatenate(
                [jnp.concatenate([buf[slot, j, h].astype(BF16) for h in range(heads)], axis=1)
                 for j in range(pps)], axis=0)

        step(cat(kbuf), cat(vbuf), None)

    @pl.when(p == steps - 1)
    def _():
        acc = acc_scr[...]
        keep = _mod(_iota((nq, width), 0), heads) == _div(_iota((nq, width), 1), d)
        sel = jnp.where(keep, acc, 0.0)
        out_row = _iota((o_ref.shape[1], width), 0)
        out = jnp.zeros((o_ref.shape[1], width), F32)
        for t in range(nq_per_head):
            head_sum = jnp.sum(sel[t * heads:(t + 1) * heads], axis=0, keepdims=True)
            out = jnp.where(out_row == t, head_sum, out)
        o_ref[0] = out


def _sb_decode(q, k, v, cache_k, cache_v, page_table, sb_bias):
    nseq, n_pages = page_table.shape
    heads = sb_bias.shape[0]
    width = q.shape[1]
    d = width // heads
    n_new = q.shape[0] // nseq
    nq = n_new * heads
    assert cache_k.shape[1] == PAGE and d == LANE and n_new <= SUBLANE
    q4 = q.reshape(nseq, n_new, heads, 1, d)
    eye = jnp.eye(heads, dtype=F32).reshape(1, 1, heads, heads, 1)
    qbd = (q4 * eye).reshape(nseq, nq, width).astype(BF16)
    bias_col = jnp.tile(sb_bias.astype(F32), n_new).reshape(nq, 1)
    pad = ((0, 0), (0, PAGE - n_new), (0, 0))
    knew = jnp.pad(k.reshape(nseq, n_new, width), pad)
    vnew = jnp.pad(v.reshape(nseq, n_new, width), pad)
    pps = DECODE_PAGES_PER_STEP
    assert n_pages % pps == 0
    steps = n_pages // pps + 1

    seq3 = lambda b, p, pt: (b, 0, 0)
    out = pl.pallas_call(
        functools.partial(_sb_decode_kernel, heads=heads, steps=steps, scale=d ** -0.5),
        grid_spec=pltpu.PrefetchScalarGridSpec(
            num_scalar_prefetch=1,
            grid=(nseq, steps),
            in_specs=[pl.BlockSpec((1, nq, width), seq3),
                      pl.BlockSpec((nq, 1), lambda b, p, pt: (0, 0)),
                      pl.BlockSpec((1, PAGE, width), seq3),
                      pl.BlockSpec((1, PAGE, width), seq3),
                      pl.BlockSpec(memory_space=pl.ANY),
                      pl.BlockSpec(memory_space=pl.ANY)],
            out_specs=pl.BlockSpec((1, SUBLANE, width), seq3),
            scratch_shapes=[pltpu.VMEM((nq, 1), F32), pltpu.VMEM((nq, width), F32),
                            pltpu.VMEM((2, pps, heads, PAGE, d), F32), pltpu.VMEM((2, pps, heads, PAGE, d), F32),
                            pltpu.SemaphoreType.DMA((2, 2))]),
        out_shape=jax.ShapeDtypeStruct((nseq, SUBLANE, width), F32),
        compiler_params=_params("arbitrary", "arbitrary"),
        name="sb_decode",
    )(page_table, qbd, bias_col, knew, vnew, cache_k, cache_v)
    return out[:, :n_new].reshape(nseq * n_new, width).astype(BF16)


def _ssd_kernel(x_ref, b_ref, c_ref, px_ref, pb_ref, pc_ref, wx_ref, wb_ref, wc_ref,
                bx_ref, bb_ref, bc_ref, dt_ref, dtb_ref, alog_ref, dskip_ref, z_ref, nw_ref, h0_ref,
                y_ref, hout_ref, h_scr, ex_scr, eb_scr, ec_scr, *, valid_len, n_chunks, rep, hd, gp):
    c = pl.program_id(2)
    cl = x_ref.shape[0]
    xw = rep * hd
    n = h_scr.shape[1]

    @pl.when(c == 0)
    def _():
        h_scr[...] = h0_ref[0].reshape(gp * xw, n)
        ex_scr[0:SUBLANE, :] = px_ref[0]
        eb_scr[0:SUBLANE, :] = pb_ref[0]
        ec_scr[0:SUBLANE, :] = pc_ref[0]

    def conv_silu(e_scr, raw_ref, w_ref, bias_ref):
        e_scr[SUBLANE:SUBLANE + cl, :] = raw_ref[...]
        acc = bias_ref[...]
        for i in range(CONV_TAPS):
            lo = SUBLANE - (CONV_TAPS - 1) + i
            acc = acc + e_scr[lo:lo + cl, :] * w_ref[i:i + 1, :]
        e_scr[0:SUBLANE, :] = e_scr[cl:cl + SUBLANE, :]
        return _silu(acc)

    xc_all = conv_silu(ex_scr, x_ref, wx_ref, bx_ref)
    bm_all = conv_silu(eb_scr, b_ref, wb_ref, bb_ref)
    cm_all = conv_silu(ec_scr, c_ref, wc_ref, bc_ref)

    row = c * cl + _iota((cl, 1), 0)
    t_idx = _iota((cl, cl), 0)
    s_idx = _iota((cl, cl), 1)
    tril = t_idx >= s_idx
    tril16 = tril.astype(BF16)
    lane = _iota((1, xw), 1)
    in_head = [(lane >= r * hd) & (lane < (r + 1) * hd) for r in range(rep)]
    groups = range(gp)

    xc = [xc_all[:, s * xw:(s + 1) * xw] for s in groups]
    bm16 = [bm_all[:, s * n:(s + 1) * n].astype(BF16) for s in groups]
    cm16 = [cm_all[:, s * n:(s + 1) * n].astype(BF16) for s in groups]
    h = [h_scr[s * xw:(s + 1) * xw, :] for s in groups]

    def step_sizes(s):
        dtr = dt_ref[0, s] + dtb_ref[s]
        return jnp.where(row < valid_len, jnp.maximum(dtr, 0.0) + _softplus_neg_abs(dtr), 0.0)

    dt = [step_sizes(s) for s in groups]
    cum = [_split_dot_left(tril16, dt[s] * -jnp.exp(alog_ref[s]), 3) for s in groups]
    g = [lax.dot_general(cm16[s], bm16[s], NT_DIMS, preferred_element_type=F32) for s in groups]
    inter = [lax.dot_general(cm16[s], h[s].astype(BF16), NT_DIMS, preferred_element_type=F32) for s in groups]
    cum_t = [cum[s].T for s in groups]
    dt_t = [dt[s].T for s in groups]

    def head_terms(s):
        cum_last = cum[s][cl - 1:cl, :]
        decay_last = jnp.exp(cum_last)
        ws, xrs, hdecay = [], [], []
        e_all = jnp.zeros((cl, xw), F32)
        tail_all = jnp.zeros((cl, xw), F32)
        for r in range(rep):
            ccol = cum[s][:, r:r + 1]
            decay = jnp.exp(jnp.where(tril, ccol - cum_t[s][r:r + 1, :], -jnp.inf))
            ws.append((g[s] * decay * dt_t[s][r:r + 1, :]).astype(BF16))
            xrs.append(jnp.where(in_head[r], xc[s], 0.0).astype(BF16))
            e_all = jnp.where(in_head[r], jnp.exp(ccol), e_all)
            tail_all = jnp.where(in_head[r], jnp.exp(cum_last[:, r:r + 1] - ccol) * dt[s][:, r:r + 1], tail_all)
            hdecay.append(jnp.broadcast_to(decay_last[:, r:r + 1], (hd, n)))
        return (jnp.concatenate(ws, axis=1), jnp.concatenate(xrs, axis=0), e_all, tail_all,
                jnp.concatenate(hdecay, axis=0))

    terms = [head_terms(s) for s in groups]
    intra = [jnp.dot(terms[s][0], terms[s][1], preferred_element_type=F32) for s in groups]
    y = [intra[s] + inter[s] * terms[s][2] + xc[s] * dskip_ref[s] for s in groups]

    def transposed(xt):
        return jnp.concatenate([xt[:, j * LANE:(j + 1) * LANE].T for j in range(xw // LANE)], axis=0)

    xt_t = [transposed(xc[s] * terms[s][3]).astype(BF16) for s in groups]
    h_new = [h[s] * terms[s][4] + jnp.dot(xt_t[s], bm16[s], preferred_element_type=F32) for s in groups]
    for s in groups:
        h_scr[s * xw:(s + 1) * xw, :] = h_new[s]

    for s in groups:
        lanes = slice(s * xw, (s + 1) * xw)
        gated = y[s] * _silu(z_ref[:, lanes])
        ms = jnp.mean(gated * gated, axis=-1, keepdims=True)
        y_ref[:, lanes] = (gated * lax.rsqrt(ms + RMS_EPS) * nw_ref[:, lanes]).astype(y_ref.dtype)

    @pl.when(c == n_chunks - 1)
    def _():
        hout_ref[0] = h_scr[...].reshape(hout_ref.shape[1:])


def _ssd(xbc, dt_raw, z, conv_prev, h0, valid_len, conv_w, conv_b, dt_bias, a_log, d_skip, norm_w, groups):
    nseq, heads, hd, n = h0.shape
    rows, ch = xbc.shape
    seq = rows // nseq
    rep = heads // groups
    xw = rep * hd
    ssm_w = heads * hd
    cl = CHUNK
    nc = seq // cl
    gp = SSD_GROUPS_PER_STEP
    assert seq % cl == 0 and n == LANE and xw % LANE == 0 and ch == ssm_w + 2 * groups * n
    assert groups % gp == 0 and ssm_w % (gp * n) == 0
    b_off = ssm_w // (gp * n)
    c_off = b_off + groups // gp

    prev = jnp.pad(conv_prev, ((0, 0), (SUBLANE - (CONV_TAPS - 1), 0), (0, 0)))
    dt_g = dt_raw.reshape(nseq, seq, groups, rep).transpose(0, 2, 1, 3)
    dt_g = jnp.pad(dt_g, ((0, 0), (0, 0), (0, 0), (0, LANE - rep)))

    def lane_pad(p):
        return jnp.pad(p.astype(F32).reshape(groups, 1, rep), ((0, 0), (0, 0), (0, LANE - rep)))

    dskip_rep = jnp.repeat(d_skip.astype(F32), hd).reshape(groups, 1, xw)
    cb = conv_b.reshape(1, ch)

    def rows_at(width, off):
        return pl.BlockSpec((cl, width), lambda b, g, c: (b * nc + c, off + g))

    def prev_at(width, off):
        return pl.BlockSpec((1, SUBLANE, width), lambda b, g, c: (b, 0, off + g))

    def taps_at(nrows, width, off):
        return pl.BlockSpec((nrows, width), lambda b, g, c: (0, off + g))

    par = pl.BlockSpec((gp, 1, LANE), lambda b, g, c: (g, 0, 0))
    state = pl.BlockSpec((1, gp * rep, hd, n), lambda b, g, c: (b, g, 0, 0))
    xs_w, bc_w = gp * xw, gp * n
    y, h = pl.pallas_call(
        functools.partial(_ssd_kernel, valid_len=valid_len, n_chunks=nc, rep=rep, hd=hd, gp=gp),
        grid=(nseq, groups // gp, nc),
        in_specs=[rows_at(xs_w, 0), rows_at(bc_w, b_off), rows_at(bc_w, c_off),
                  prev_at(xs_w, 0), prev_at(bc_w, b_off), prev_at(bc_w, c_off),
                  taps_at(CONV_TAPS, xs_w, 0), taps_at(CONV_TAPS, bc_w, b_off), taps_at(CONV_TAPS, bc_w, c_off),
                  taps_at(1, xs_w, 0), taps_at(1, bc_w, b_off), taps_at(1, bc_w, c_off),
                  pl.BlockSpec((1, gp, cl, LANE), lambda b, g, c: (b, g, c, 0)),
                  par, par,
                  pl.BlockSpec((gp, 1, xw), lambda b, g, c: (g, 0, 0)),
                  rows_at(xs_w, 0),
                  taps_at(1, xs_w, 0),
                  state],
        out_specs=[rows_at(xs_w, 0), state],
        out_shape=[jax.ShapeDtypeStruct((rows, ssm_w), BF16), jax.ShapeDtypeStruct(h0.shape, F32)],
        scratch_shapes=[pltpu.VMEM((gp * xw, n), F32),
                        pltpu.VMEM((cl + SUBLANE, xs_w), F32),
                        pltpu.VMEM((cl + SUBLANE, bc_w), F32),
                        pltpu.VMEM((cl + SUBLANE, bc_w), F32)],
        compiler_params=_params("arbitrary", "arbitrary", "arbitrary"),
        name="ssd",
    )(xbc, xbc, xbc, prev, prev, prev, conv_w, conv_w, conv_w, cb, cb, cb,
      dt_g, lane_pad(dt_bias), lane_pad(a_log), dskip_rep, z, norm_w.reshape(1, ssm_w), h0)
    return y, h


def _pool_kernel(u_ref, prev_ref, w_ref, s_ref, o_ref, ext_scr, *, pos0, gd):
    i = pl.program_id(1)
    tr = u_ref.shape[0]

    @pl.when(i == 0)
    def _():
        ext_scr[0:POOL_HIST, :] = prev_ref[0]

    ext_scr[POOL_HIST:POOL_HIST + tr, :] = u_ref[...]
    pos = (pos0 + i * tr + _iota((tr, 1), 0)).astype(F32)
    for g, win in enumerate(POOL_WINDOWS):
        lanes = slice(g * gd, (g + 1) * gd)
        total = ext_scr[POOL_HIST:POOL_HIST + tr, lanes]
        for j in range(1, win):
            total = total + ext_scr[POOL_HIST - j:POOL_HIST - j + tr, lanes]
        count = jnp.minimum(pos + 1.0, float(win))
        dlt = total / count - u_ref[:, lanes]
        y = jnp.dot(dlt.astype(BF16), w_ref[g], preferred_element_type=F32)
        o_ref[:, lanes] = (y * s_ref[:, lanes]).astype(o_ref.dtype)
    ext_scr[0:POOL_HIST, :] = ext_scr[tr:tr + POOL_HIST, :]


def _pool(u, prev, pos0, w_pool16, pool_scale, tr):
    nseq = prev.shape[0]
    rows, width = u.shape
    seq = rows // nseq
    tr = min(tr, seq)
    nt = seq // tr
    ng, gd, _ = w_pool16.shape
    assert seq % tr == 0 and ng == len(POOL_WINDOWS) and ng * gd == width
    prev16 = jnp.pad(prev, ((0, 0), (POOL_HIST - prev.shape[1], 0), (0, 0)))
    return pl.pallas_call(
        functools.partial(_pool_kernel, pos0=pos0, gd=gd),
        grid=(nseq, nt),
        in_specs=[pl.BlockSpec((tr, width), lambda b, i: (b * nt + i, 0)),
                  pl.BlockSpec((1, POOL_HIST, width), lambda b, i: (b, 0, 0)),
                  pl.BlockSpec((ng, gd, gd), lambda b, i: (0, 0, 0)),
                  pl.BlockSpec((1, width), lambda b, i: (0, 0))],
        out_specs=pl.BlockSpec((tr, width), lambda b, i: (b * nt + i, 0)),
        out_shape=jax.ShapeDtypeStruct((rows, width), BF16),
        scratch_shapes=[pltpu.VMEM((tr + POOL_HIST, width), F32)],
        compiler_params=_params("arbitrary", "arbitrary"),
        name="pool",
    )(u, prev16, w_pool16, pool_scale.reshape(1, width))


def _gla_kernel(q_ref, k_ref, v_ref, r_ref, glr_ref, w2_ref, b2_ref, nw_ref, s0_ref,
                o_ref, sout_ref, s_scr, *, valid_len, n_chunks, scale, hp):
    c = pl.program_id(2)
    cl = q_ref.shape[0]
    dk = s_scr.shape[1]
    dv = s_scr.shape[2]
    heads = range(hp)

    @pl.when(c == 0)
    def _():
        s_scr[...] = s0_ref[0]

    row = c * cl + _iota((cl, 1), 0)
    valid = row < valid_len
    pre = jnp.dot(glr_ref[...], w2_ref[...], preferred_element_type=F32) + b2_ref[...]
    lg_all = (jnp.minimum(pre, 0.0) - _softplus_neg_abs(pre)) / GLA_TAU
    lg_all = jnp.where(valid, lg_all, 0.0)
    qs_all = q_ref[...] * scale
    k_all = jnp.where(valid, k_ref[...], 0.0)
    kl = [slice(h * dk, (h + 1) * dk) for h in heads]
    vl = [slice(h * dv, (h + 1) * dv) for h in heads]
    lg = [lg_all[:, kl[h]] for h in heads]
    qs = [qs_all[:, kl[h]] for h in heads]
    k = [k_all[:, kl[h]] for h in heads]
    v16 = [v_ref[:, vl[h]].astype(BF16) for h in heads]

    t_idx = _iota((cl, cl), 0)
    s_idx = _iota((cl, cl), 1)
    r_idx = _iota((cl, 1), 0)
    def masked_qk(qd, kd, mask):
        return jnp.where(mask, lax.dot_general(qd.astype(BF16), kd.astype(BF16), NT_DIMS,
                                               preferred_element_type=F32), 0.0)

    att = [masked_qk(qs[h], k[h], t_idx == s_idx) for h in heads]
    pre_b = lg
    suf_b = [jnp.zeros_like(lg[h]) for h in heads]
    tot_b = lg
    bsz = 1
    while bsz < cl:
        siblings = ((_div(t_idx, 2 * bsz) == _div(s_idx, 2 * bsz))
                    & (_mod(_div(t_idx, bsz), 2) == 1) & (_mod(_div(s_idx, bsz), 2) == 0))
        att = [att[h] + masked_qk(qs[h] * jnp.exp(pre_b[h]), k[h] * jnp.exp(suf_b[h]), siblings) for h in heads]
        is_right = _mod(_div(r_idx, bsz), 2) == 1
        left_tot = [pltpu.roll(tot_b[h], bsz, axis=0) for h in heads]
        right_tot = [pltpu.roll(tot_b[h], cl - bsz, axis=0) for h in heads]
        pre_b = [pre_b[h] + jnp.where(is_right, left_tot[h], 0.0) for h in heads]
        suf_b = [suf_b[h] + jnp.where(is_right, 0.0, right_tot[h]) for h in heads]
        tot_b = [tot_b[h] + jnp.where(is_right, left_tot[h], right_tot[h]) for h in heads]
        bsz *= 2

    s = [s_scr[h] for h in heads]
    qd = [(qs[h] * jnp.exp(pre_b[h])).astype(BF16) for h in heads]
    kd_t = [(k[h] * jnp.exp(suf_b[h])).T.astype(BF16) for h in heads]
    o = [jnp.dot(att[h].astype(BF16), v16[h], preferred_element_type=F32)
         + jnp.dot(qd[h], s[h].astype(BF16), preferred_element_type=F32) for h in heads]
    chunk_decay = [jnp.exp(tot_b[h].T) for h in heads]
    s_new = [s[h] * jnp.concatenate([chunk_decay[h]] * (dv // cl), axis=1)
             + jnp.dot(kd_t[h], v16[h], preferred_element_type=F32) for h in heads]
    for h in heads:
        s_scr[h] = s_new[h]

    for h in heads:
        ms = jnp.mean(o[h] * o[h], axis=-1, keepdims=True)
        o_ref[:, vl[h]] = (o[h] * lax.rsqrt(ms + RMS_EPS) * nw_ref[:, vl[h]]
                           * _silu(r_ref[:, vl[h]])).astype(o_ref.dtype)

    @pl.when(c == n_chunks - 1)
    def _():
        sout_ref[0] = s_scr[...]


def _gla(q, k, v, r, glr, w2, b2, norm_w, s0, valid_len):
    nseq, heads, dk, dv = s0.shape
    rows = q.shape[0]
    seq = rows // nseq
    cl = CHUNK
    nc = seq // cl
    hp = GLA_HEADS_PER_STEP
    assert seq % cl == 0 and dk == cl and dv % cl == 0 and glr.shape[1] == LANE and heads % hp == 0

    def rows_at(width, col):
        return pl.BlockSpec((cl, width), (lambda b, h, c: (b * nc + c, h)) if col else (lambda b, h, c: (b * nc + c, 0)))

    def head_at(nrows, width):
        return pl.BlockSpec((nrows, width), lambda b, h, c: (0, h))

    state = pl.BlockSpec((1, hp, dk, dv), lambda b, h, c: (b, h, 0, 0))
    kw, vw = hp * dk, hp * dv
    return pl.pallas_call(
        functools.partial(_gla_kernel, valid_len=valid_len, n_chunks=nc, scale=dk ** -0.5, hp=hp),
        grid=(nseq, heads // hp, nc),
        in_specs=[rows_at(kw, True), rows_at(kw, True), rows_at(vw, True), rows_at(vw, True),
                  rows_at(LANE, False), head_at(LANE, kw), head_at(1, kw), head_at(1, vw), state],
        out_specs=[rows_at(vw, True), state],
        out_shape=[jax.ShapeDtypeStruct((rows, heads * dv), BF16), jax.ShapeDtypeStruct(s0.shape, F32)],
        scratch_shapes=[pltpu.VMEM((hp, dk, dv), F32)],
        compiler_params=_params("arbitrary", "arbitrary", "arbitrary"),
        name="gla",
    )(q, k, v, r, glr, w2, b2.reshape(1, heads * dk), norm_w.reshape(1, heads * dv), s0)


def _pad_seq(a, nseq, seq_pad):
    n = a.shape[0] // nseq
    a3 = jnp.pad(a.reshape(nseq, n, a.shape[1]), ((0, 0), (0, seq_pad - n), (0, 0)))
    return a3.reshape(nseq * seq_pad, a.shape[1])


def _unpad_seq(a, nseq, n):
    return a.reshape(nseq, -1, a.shape[1])[:, :n].reshape(nseq * n, a.shape[1])


def _project(xp16, xs16, w, sizes, name):
    outs_p, outs_s = [], []
    off = 0
    w_nk = w.T
    for idx, size in enumerate(sizes):
        op, os_ = _proj([xp16], [xs16], w_nk, True, off, size, F32, PROJ_TM, PROJ_TN, f"{name}_{idx}")
        off += size
        outs_p.append(op)
        outs_s.append(os_)
    return outs_p, outs_s


def kernel(x_prompt, x_sample, cache_k, cache_v, page_table, state_conv, state_ssm, state_pool, state_gla, w_in0, sb_bias, conv_w, conv_b, dt_bias, a_log, d_skip, ssm_norm_w, w_out0, w_in1, w_pool, pool_scale, gla_w2, gla_b2, gla_norm_w, w_out1, ln_g, ln_b, w_gu, w_down):
    bp, seq, dm = x_prompt.shape
    bs, n_new, _ = x_sample.shape
    depth = ln_g.shape[0]
    alpha = (2 * depth) ** 0.25
    sb_heads, sb_d = cache_k.shape[2], cache_k.shape[3]
    sb_w = sb_heads * sb_d
    ssm_heads, ssm_hd, ssm_n = state_ssm.shape[1:]
    ssm_w = ssm_heads * ssm_hd
    conv_ch = state_conv.shape[2]
    groups = (conv_ch - ssm_w) // (2 * ssm_n)
    pool_w = state_pool.shape[2]
    gla_heads, gla_dk, gla_dv = state_gla.shape[1:]
    gla_kw, gla_vw = gla_heads * gla_dk, gla_heads * gla_dv
    gla_rank = gla_w2.shape[0]
    past_len = page_table.shape[1] * cache_k.shape[1]
    seq_pad = CHUNK

    xp = x_prompt.reshape(bp * seq, dm)
    xs = x_sample.reshape(bs * n_new, dm)
    xp16, xs16 = xp.astype(BF16), xs.astype(BF16)

    def mix_ffn_ln(xp, xs, mixed_p, mixed_s, w_out, layer):
        mp, ms = _proj(mixed_p, mixed_s, w_out, False, 0, dm, F32, PROJ_TM, PROJ_TN, f"out{layer}")
        xp, xp16 = _ln(mp, ln_g[layer, 0], ln_b[layer, 0], LN_TM, res=xp, alpha=alpha)
        xs, xs16 = _ln(ms, ln_g[layer, 0], ln_b[layer, 0], LN_TM, res=xs, alpha=alpha)
        hp, hs, w_down16 = _gate_up(xp16, xs16, w_gu, w_down, layer, PROJ_TM, GU_TN)
        yp = _mm_res(hp, w_down16, xp, alpha, DOWN_TM, DOWN_TN)
        ys = _mm_res(hs, w_down16, xs, alpha, DOWN_TM, DOWN_TN)
        xp, xp16 = _ln(yp, ln_g[layer, 1], ln_b[layer, 1], LN_TM)
        xs, xs16 = _ln(ys, ln_g[layer, 1], ln_b[layer, 1], LN_TM)
        return xp, xp16, xs, xs16

    split0 = (sb_w, sb_w, sb_w, ssm_w, conv_ch, ssm_heads)
    (qp, kp, vp, zp, xbcp, dtp), (qs, ks, vs, zs, xbcs, dts) = _project(xp16, xs16, w_in0, split0, "in0")
    dtp, dts = dtp[:, :ssm_heads], dts[:, :ssm_heads]

    oap = _sb_prompt(qp, kp, vp, sb_bias, bp, SB_T, SB_HEADS_PER_STEP)
    oas = _sb_decode(qs, ks, vs, cache_k, cache_v, page_table, sb_bias)

    ssd_w = (conv_w, conv_b, dt_bias, a_log, d_skip, ssm_norm_w, groups)
    yp, ssm_p = _ssd(xbcp, dtp, zp, jnp.zeros((bp, CONV_TAPS - 1, conv_ch), F32),
                     jnp.zeros((bp,) + state_ssm.shape[1:], F32), seq, *ssd_w)
    ys, ssm_s = _ssd(_pad_seq(xbcs, bs, seq_pad), _pad_seq(dts, bs, seq_pad), _pad_seq(zs, bs, seq_pad),
                     state_conv, state_ssm, n_new, *ssd_w)
    ys = _unpad_seq(ys, bs, n_new)

    xp, xp16, xs, xs16 = mix_ffn_ln(xp, xs, [oap, yp], [oas, ys], w_out0, 0)

    split1 = (pool_w, gla_kw, gla_kw, gla_vw, gla_vw, gla_rank)
    (up, gqp, gkp, gvp, grp, glrp), (us, gqs, gks, gvs, grs, glrs) = _project(xp16, xs16, w_in1, split1, "in1")

    w_pool16 = w_pool.astype(BF16)
    ocp = _pool(up, jnp.zeros((bp, POOL_HIST - 1, pool_w), F32), 0, w_pool16, pool_scale, POOL_TM)
    us8 = _pad_seq(us, bs, SUBLANE)
    ocs = _unpad_seq(_pool(us8, state_pool, past_len, w_pool16, pool_scale, SUBLANE), bs, n_new)

    w2_16 = jnp.pad(gla_w2, ((0, LANE - gla_rank), (0, 0))).astype(BF16)
    gla_w = (w2_16, gla_b2, gla_norm_w)
    odp, gla_p = _gla(gqp, gkp, gvp, grp, glrp.astype(BF16), *gla_w,
                      jnp.zeros((bp,) + state_gla.shape[1:], F32), seq)
    pad = lambda a: _pad_seq(a, bs, seq_pad)
    ods, gla_s = _gla(pad(gqs), pad(gks), pad(gvs), pad(grs), pad(glrs).astype(BF16), *gla_w, state_gla, n_new)
    ods = _unpad_seq(ods, bs, n_new)

    xp, xp16, xs, xs16 = mix_ffn_ln(xp, xs, [ocp, odp], [ocs, ods], w_out1, 1)

    hd4 = lambda a, nseq: a.reshape(nseq, -1, sb_heads, sb_d)
    conv_p = xbcp.reshape(bp, seq, conv_ch)[:, seq - (CONV_TAPS - 1):]
    conv_s = jnp.concatenate([state_conv, xbcs.reshape(bs, n_new, conv_ch)], axis=1)[:, n_new:]
    pool_p = up.reshape(bp, seq, pool_w)[:, seq - (POOL_HIST - 1):]
    pool_s = jnp.concatenate([state_pool, us.reshape(bs, n_new, pool_w)], axis=1)[:, n_new:]
    return (xp.reshape(bp, seq, dm), xs.reshape(bs, n_new, dm),
            hd4(kp, bp), hd4(vp, bp), hd4(ks, bs), hd4(vs, bs),
            conv_p, conv_s, ssm_p, ssm_s, pool_p, pool_s, gla_p, gla_s)
```

```python
import functools

import jax
import jax.numpy as jnp
from jax import lax
from jax.experimental import pallas as pl
from jax.experimental.pallas import tpu as pltpu

F32 = jnp.float32
BF16 = jnp.bfloat16

LANE = 128
SUBLANE = 8
VMEM_LIMIT = 56 * 1024 * 1024

LN_EPS = 1e-5
RMS_EPS = 1e-5
GLA_TAU = 16.0
PAGE = 128
POOL_WINDOWS = (2, 4, 8, 16)
POOL_HIST = 16
CONV_TAPS = 4
CHUNK = 128

PROJ_TM, PROJ_TN = 1024, 512
GU_TN = 256
DOWN_TM, DOWN_TN = 512, 512
LN_TM = 256
SB_T, SB_HEADS_PER_STEP = 256, 4
POOL_TM = 256
DECODE_PAGES_PER_STEP = 4
SSD_GROUPS_PER_STEP = 2
GLA_HEADS_PER_STEP = 2

NT_DIMS = (((1,), (1,)), ((), ()))


def _params(*sem):
    return pltpu.CompilerParams(dimension_semantics=sem, vmem_limit_bytes=VMEM_LIMIT)


def _sigmoid(x):
    return 1.0 / (1.0 + jnp.exp(-x))


def _silu(x):
    return x * _sigmoid(x)


def _softplus_neg_abs(x):
    return jnp.log1p(jnp.exp(-jnp.abs(x)))


def _iota(shape, dim):
    return lax.broadcasted_iota(jnp.int32, shape, dim)


def _div(x, c):
    return lax.shift_right_logical(x, (c.bit_length() - 1)) if c & (c - 1) == 0 else x // c


def _mod(x, c):
    return (x & (c - 1)) if c & (c - 1) == 0 else x % c


def _bf16_pieces(x, terms):
    pieces = [x.astype(BF16)]
    for _ in range(terms - 1):
        x = x - pieces[-1].astype(F32)
        pieces.append(x.astype(BF16))
    return pieces


def _add_all(terms):
    return functools.reduce(lambda a, b: a + b, terms)


def _split_dot_left(m01, x, terms):
    return _add_all([jnp.dot(m01, p, preferred_element_type=F32) for p in _bf16_pieces(x, terms)])


def _mm_res_kernel(x_ref, w_ref, r_ref, o_ref, *, alpha):
    o_ref[...] = alpha * r_ref[...] + jnp.dot(x_ref[...], w_ref[...], preferred_element_type=F32)


def _mm_res(x, w, res, alpha, tm, tn):
    m, k = x.shape
    n = w.shape[1]
    tm = min(tm, m)
    tn = min(tn, n)
    assert m % tm == 0 and n % tn == 0
    tile = pl.BlockSpec((tm, tn), lambda i, j: (i, j))
    return pl.pallas_call(
        functools.partial(_mm_res_kernel, alpha=alpha),
        grid=(m // tm, n // tn),
        in_specs=[pl.BlockSpec((tm, k), lambda i, j: (i, 0)),
                  pl.BlockSpec((k, tn), lambda i, j: (0, j)),
                  tile],
        out_specs=tile,
        out_shape=jax.ShapeDtypeStruct((m, n), F32),
        compiler_params=_params("arbitrary", "arbitrary"),
        name="down",
    )(x, w, res)


def _dot_parts(x_refs, w16, w_is_nk):
    terms, off = [], 0
    for x_ref in x_refs:
        k = x_ref.shape[1]
        if w_is_nk:
            terms.append(lax.dot_general(x_ref[...], w16[:, off:off + k], NT_DIMS, preferred_element_type=F32))
        else:
            terms.append(jnp.dot(x_ref[...], w16[off:off + k, :], preferred_element_type=F32))
        off += k
    return _add_all(terms)


def _proj_kernel(*refs, n_parts, w_is_nk, col0, ncols, nb):
    xps, xss = refs[:n_parts], refs[n_parts:2 * n_parts]
    w_ref, op_ref, os_ref, wstage, sem, w16 = refs[2 * n_parts:]
    n = pl.program_id(0)
    m = pl.program_id(1)
    tn = op_ref.shape[1]
    nv = min(tn, ncols)

    def tile_copy(t, slot):
        cols = pl.ds(pl.multiple_of(col0 + t * tn, nv), nv)
        if w_is_nk:
            return pltpu.make_async_copy(w_ref.at[cols, :], wstage.at[slot, pl.ds(0, nv), :], sem.at[slot])
        return pltpu.make_async_copy(w_ref.at[:, cols], wstage.at[slot, :, pl.ds(0, nv)], sem.at[slot])

    @pl.when(m == 0)
    def _():
        slot = n % 2

        @pl.when(n == 0)
        def _():
            tile_copy(0, 0).start()

        tile_copy(n, slot).wait()
        if nv == tn:
            w16[...] = wstage[slot].astype(BF16)
        elif w_is_nk:
            w16[0:nv, :] = wstage[slot, 0:nv, :].astype(BF16)
            w16[nv:tn, :] = jnp.zeros((tn - nv, w16.shape[1]), BF16)
        else:
            w16[:, 0:nv] = wstage[slot, :, 0:nv].astype(BF16)
            w16[:, nv:tn] = jnp.zeros((w16.shape[0], tn - nv), BF16)

        @pl.when(n + 1 < nb)
        def _():
            tile_copy(n + 1, 1 - slot).start()

        os_ref[...] = _dot_parts(xss, w16, w_is_nk).astype(os_ref.dtype)

    op_ref[...] = _dot_parts(xps, w16, w_is_nk).astype(op_ref.dtype)


def _col_tile(col0, ncols, tn_max):
    if ncols < LANE:
        assert col0 % LANE == 0
        return LANE
    tn = tn_max
    while ncols % tn or col0 % tn:
        tn //= 2
    assert tn >= LANE
    return tn


def _proj(xps, xss, w, w_is_nk, col0, ncols, out_dtype, tm, tn_max, name):
    m = xps[0].shape[0]
    s = xss[0].shape[0]
    k = w.shape[1] if w_is_nk else w.shape[0]
    tm = min(tm, m)
    tn = _col_tile(col0, ncols, tn_max)
    nb = pl.cdiv(ncols, tn)
    assert m % tm == 0 and sum(x.shape[1] for x in xps) == k and min(tn, ncols) % SUBLANE == 0
    tile_shape = (tn, k) if w_is_nk else (k, tn)
    return pl.pallas_call(
        functools.partial(_proj_kernel, n_parts=len(xps), w_is_nk=w_is_nk, col0=col0, ncols=ncols, nb=nb),
        grid=(nb, m // tm),
        in_specs=([pl.BlockSpec((tm, x.shape[1]), lambda n, i: (i, 0)) for x in xps]
                  + [pl.BlockSpec((s, x.shape[1]), lambda n, i: (0, 0)) for x in xss]
                  + [pl.BlockSpec(memory_space=pl.ANY)]),
        out_specs=[pl.BlockSpec((tm, tn), lambda n, i: (i, n)),
                   pl.BlockSpec((s, tn), lambda n, i: (0, n))],
        out_shape=[jax.ShapeDtypeStruct((m, nb * tn), out_dtype), jax.ShapeDtypeStruct((s, nb * tn), out_dtype)],
        scratch_shapes=[pltpu.VMEM((2,) + tile_shape, F32), pltpu.SemaphoreType.DMA((2,)),
                        pltpu.VMEM(tile_shape, BF16)],
        compiler_params=_params("arbitrary", "arbitrary"),
        name=name,
    )(*xps, *xss, w)


def _gu_kernel(xp_ref, xs_ref, wgu_ref, wd_ref, op_ref, os_ref, wd16_ref, wstage, sem, wg16, wu16,
               *, layer, nb):
    n = pl.program_id(0)
    m = pl.program_id(1)
    tn = wg16.shape[1]

    def tile_copies(t, slot):
        cols = [pl.ds(pl.multiple_of((t + half * nb) * tn, tn), tn) for half in range(2)]
        return [pltpu.make_async_copy(wgu_ref.at[layer, :, cols[half]], wstage.at[slot, half], sem.at[slot, half])
                for half in range(2)]

    def act(x_ref):
        x = x_ref[...]
        g = jnp.dot(x, wg16[...], preferred_element_type=F32)
        u = jnp.dot(x, wu16[...], preferred_element_type=F32)
        return (_silu(g) * u).astype(op_ref.dtype)

    @pl.when(m == 0)
    def _():
        slot = n % 2

        @pl.when(n == 0)
        def _():
            for c in tile_copies(0, 0):
                c.start()

        for c in tile_copies(n, slot):
            c.wait()
        wg16[...] = wstage[slot, 0].astype(BF16)
        wu16[...] = wstage[slot, 1].astype(BF16)

        @pl.when(n + 1 < nb)
        def _():
            for c in tile_copies(n + 1, 1 - slot):
                c.start()

        os_ref[...] = act(xs_ref)

    wd16_ref[...] = wd_ref[...].astype(BF16)
    op_ref[...] = act(xp_ref)


def _gate_up(xp, xs, w_gu, w_down, layer, tm, tn):
    m, k = xp.shape
    s = xs.shape[0]
    f = w_gu.shape[2] // 2
    dm = w_down.shape[2]
    tm = min(tm, m)
    tn = min(tn, f)
    assert m % tm == 0 and f % tn == 0 and w_down.shape[1] == f
    nb = f // tn
    mt = m // tm
    td = f // (nb * mt)
    assert td * nb * mt == f and td % (2 * SUBLANE) == 0
    return pl.pallas_call(
        functools.partial(_gu_kernel, layer=layer, nb=nb),
        grid=(nb, mt),
        in_specs=[pl.BlockSpec((tm, k), lambda n, i: (i, 0)),
                  pl.BlockSpec((s, k), lambda n, i: (0, 0)),
                  pl.BlockSpec(memory_space=pl.ANY),
                  pl.BlockSpec((None, td, dm), lambda n, i: (layer, n * mt + i, 0))],
        out_specs=[pl.BlockSpec((tm, tn), lambda n, i: (i, n)),
                   pl.BlockSpec((s, tn), lambda n, i: (0, n)),
                   pl.BlockSpec((td, dm), lambda n, i: (n * mt + i, 0))],
        out_shape=[jax.ShapeDtypeStruct((m, f), BF16), jax.ShapeDtypeStruct((s, f), BF16),
                   jax.ShapeDtypeStruct((f, dm), BF16)],
        scratch_shapes=[pltpu.VMEM((2, 2, k, tn), F32), pltpu.SemaphoreType.DMA((2, 2)),
                        pltpu.VMEM((k, tn), BF16), pltpu.VMEM((k, tn), BF16)],
        compiler_params=_params("arbitrary", "arbitrary"),
        name="gate_up",
    )(xp, xs, w_gu, w_down)


def _layer_norm_store(y, g_ref, b_ref, o_ref, ob_ref):
    mu = jnp.mean(y, axis=-1, keepdims=True)
    yc = y - mu
    var = jnp.mean(yc * yc, axis=-1, keepdims=True)
    out = yc * lax.rsqrt(var + LN_EPS) * g_ref[...] + b_ref[...]
    o_ref[...] = out
    ob_ref[...] = out.astype(BF16)


def _res_ln_kernel(x_ref, m_ref, g_ref, b_ref, o_ref, ob_ref, *, alpha):
    _layer_norm_store(alpha * x_ref[...] + m_ref[...], g_ref, b_ref, o_ref, ob_ref)


def _ln_kernel(y_ref, g_ref, b_ref, o_ref, ob_ref):
    _layer_norm_store(y_ref[...], g_ref, b_ref, o_ref, ob_ref)


def _ln(y, g, b, tm, res=None, alpha=None):
    rows, d = y.shape
    tm = min(tm, rows)
    assert rows % tm == 0
    row = pl.BlockSpec((tm, d), lambda i: (i, 0))
    vec = pl.BlockSpec((1, d), lambda i: (0, 0))
    if res is None:
        body, acts = _ln_kernel, (y,)
    else:
        body, acts = functools.partial(_res_ln_kernel, alpha=alpha), (res, y)
    return pl.pallas_call(
        body,
        grid=(rows // tm,),
        in_specs=[row] * len(acts) + [vec, vec],
        out_specs=[row, row],
        out_shape=[jax.ShapeDtypeStruct((rows, d), F32), jax.ShapeDtypeStruct((rows, d), BF16)],
        compiler_params=_params("arbitrary"),
        name="layer_norm",
    )(*acts, g.reshape(1, d), b.reshape(1, d))


LOG2_E = 1.4426950408889634


def _sb_tile(z2, mask, u01, carry):
    neg_log_keep = jnp.maximum(z2, 0.0) + jnp.log2(1.0 + jnp.exp2(-jnp.abs(z2)))
    if mask is not None:
        neg_log_keep = jnp.where(mask, neg_log_keep, 0.0)
    right = jnp.dot(neg_log_keep.astype(BF16), u01, preferred_element_type=F32) + carry
    w = jnp.exp2(z2 - neg_log_keep - right)
    if mask is not None:
        w = jnp.where(mask, w, 0.0)
    return w, carry + jnp.sum(neg_log_keep, axis=1, keepdims=True)


def _sb_prompt_kernel(bias_ref, q_ref, k_ref, v_ref, o_ref, *, t, hp, scale):
    g = pl.program_id(1)
    i = pl.program_id(2)
    d = q_ref.shape[1] // hp
    rows = _iota((t, t), 0)
    cols = _iota((t, t), 1)
    u01 = (rows > cols).astype(BF16)
    lanes = [slice(s * d, (s + 1) * d) for s in range(hp)]
    qs = [q_ref[:, ln].astype(BF16) for ln in lanes]
    scale2 = scale * LOG2_E
    biases2 = [bias_ref[g * hp + s] * LOG2_E for s in range(hp)]

    def tiles(j, mask, state):
        keys = pl.ds(pl.multiple_of(j * t, t), t)
        zs = [lax.dot_general(qs[s], k_ref[keys, lanes[s]].astype(BF16), NT_DIMS, preferred_element_type=F32)
              * scale2 + biases2[s] for s in range(hp)]
        ws = [_sb_tile(zs[s], mask, u01, state[s][0]) for s in range(hp)]
        return tuple(
            (ws[s][1], state[s][1] + jnp.dot(ws[s][0].astype(BF16), v_ref[keys, lanes[s]].astype(BF16),
                                             preferred_element_type=F32))
            for s in range(hp))

    state = tuple((jnp.zeros((t, 1), F32), jnp.zeros((t, d), F32)) for _ in range(hp))
    state = tiles(i, cols < rows, state)
    state = lax.fori_loop(0, i, lambda it, st: tiles(i - 1 - it, None, st), state)
    for s in range(hp):
        o_ref[:, lanes[s]] = state[s][1].astype(o_ref.dtype)


def _sb_prompt(q, k, v, sb_bias, nseq, t, hp):
    rows, width = q.shape
    heads = sb_bias.shape[0]
    d = width // heads
    seq = rows // nseq
    t = min(t, seq)
    nq = seq // t
    assert seq % t == 0 and d == LANE and heads % hp == 0
    qspec = pl.BlockSpec((t, hp * d), lambda b, g, i, bias: (b * nq + i, g))
    kvspec = pl.BlockSpec((seq, hp * d), lambda b, g, i, bias: (b, g))
    return pl.pallas_call(
        functools.partial(_sb_prompt_kernel, t=t, hp=hp, scale=d ** -0.5),
        grid_spec=pltpu.PrefetchScalarGridSpec(
            num_scalar_prefetch=1,
            grid=(nseq, heads // hp, nq),
            in_specs=[qspec, kvspec, kvspec],
            out_specs=qspec),
        out_shape=jax.ShapeDtypeStruct((rows, width), BF16),
        compiler_params=_params("arbitrary", "arbitrary", "arbitrary"),
        name="sb_prompt",
    )(sb_bias, q, k, v)


def _sb_decode_kernel(pt_ref, qbd_ref, bias_ref, knew_ref, vnew_ref, kc_ref, vc_ref, o_ref,
                      carry_scr, acc_scr, kbuf, vbuf, sem, *, heads, steps, scale):
    b = pl.program_id(0)
    p = pl.program_id(1)
    nq = qbd_ref.shape[1]
    nq_per_head = nq // heads
    width = qbd_ref.shape[2]
    d = width // heads
    pps = kbuf.shape[1]
    n_pages = (steps - 1) * pps

    def page_copies(step, slot):
        copies = []
        for j in range(pps):
            page = pt_ref[b, n_pages - step * pps + j]
            for h in range(heads):
                copies.append(pltpu.make_async_copy(kc_ref.at[page, :, h, :], kbuf.at[slot, j, h], sem.at[0, slot]))
                copies.append(pltpu.make_async_copy(vc_ref.at[page, :, h, :], vbuf.at[slot, j, h], sem.at[1, slot]))
        return copies

    @pl.when(p == 0)
    def _():
        carry_scr[...] = jnp.zeros_like(carry_scr)
        acc_scr[...] = jnp.zeros_like(acc_scr)

    @pl.when(p + 1 < steps)
    def _():
        for c in page_copies(p + 1, (p + 1) % 2):
            c.start()

    def step(kcat, vcat, mask):
        nk = kcat.shape[0]
        u01 = (_iota((nk, nk), 0) > _iota((nk, nk), 1)).astype(BF16)
        z2 = (lax.dot_general(qbd_ref[0], kcat, NT_DIMS, preferred_element_type=F32) * (scale * LOG2_E)
              + bias_ref[...] * LOG2_E)
        w, carry = _sb_tile(z2, mask, u01, carry_scr[...])
        carry_scr[...] = carry
        acc_scr[...] += jnp.dot(w.astype(BF16), vcat, preferred_element_type=F32)

    @pl.when(p == 0)
    def _():
        qi = _div(_iota((nq, PAGE), 0), heads)
        kj = _iota((nq, PAGE), 1)
        step(knew_ref[0].astype(BF16), vnew_ref[0].astype(BF16), kj < qi)

    @pl.when(p > 0)
    def _():
        slot = p % 2
        for c in page_copies(p, slot):
            c.wait()

        def cat(buf):
            return jnp.concatenate(
                [jnp.concatenate([buf[slot, j, h].astype(BF16) for h in range(heads)], axis=1)
                 for j in range(pps)], axis=0)

        step(cat(kbuf), cat(vbuf), None)

    @pl.when(p == steps - 1)
    def _():
        acc = acc_scr[...]
        keep = _mod(_iota((nq, width), 0), heads) == _div(_iota((nq, width), 1), d)
        sel = jnp.where(keep, acc, 0.0)
        out_row = _iota((o_ref.shape[1], width), 0)
        out = jnp.zeros((o_ref.shape[1], width), F32)
        for t in range(nq_per_head):
            head_sum = jnp.sum(sel[t * heads:(t + 1) * heads], axis=0, keepdims=True)
            out = jnp.where(out_row == t, head_sum, out)
        o_ref[0] = out


def _sb_decode(q, k, v, cache_k, cache_v, page_table, sb_bias):
    nseq, n_pages = page_table.shape
    heads = sb_bias.shape[0]
    width = q.shape[1]
    d = width // heads
    n_new = q.shape[0] // nseq
    nq = n_new * heads
    assert cache_k.shape[1] == PAGE and d == LANE and n_new <= SUBLANE
    q4 = q.reshape(nseq, n_new, heads, 1, d)
    eye = jnp.eye(heads, dtype=F32).reshape(1, 1, heads, heads, 1)
    qbd = (q4 * eye).reshape(nseq, nq, width).astype(BF16)
    bias_col = jnp.tile(sb_bias.astype(F32), n_new).reshape(nq, 1)
    pad = ((0, 0), (0, PAGE - n_new), (0, 0))
    knew = jnp.pad(k.reshape(nseq, n_new, width), pad)
    vnew = jnp.pad(v.reshape(nseq, n_new, width), pad)
    pps = DECODE_PAGES_PER_STEP
    assert n_pages % pps == 0
    steps = n_pages // pps + 1

    seq3 = lambda b, p, pt: (b, 0, 0)
    out = pl.pallas_call(
        functools.partial(_sb_decode_kernel, heads=heads, steps=steps, scale=d ** -0.5),
        grid_spec=pltpu.PrefetchScalarGridSpec(
            num_scalar_prefetch=1,
            grid=(nseq, steps),
            in_specs=[pl.BlockSpec((1, nq, width), seq3),
                      pl.BlockSpec((nq, 1), lambda b, p, pt: (0, 0)),
                      pl.BlockSpec((1, PAGE, width), seq3),
                      pl.BlockSpec((1, PAGE, width), seq3),
                      pl.BlockSpec(memory_space=pl.ANY),
                      pl.BlockSpec(memory_space=pl.ANY)],
            out_specs=pl.BlockSpec((1, SUBLANE, width), seq3),
            scratch_shapes=[pltpu.VMEM((nq, 1), F32), pltpu.VMEM((nq, width), F32),
                            pltpu.VMEM((2, pps, heads, PAGE, d), F32), pltpu.VMEM((2, pps, heads, PAGE, d), F32),
                            pltpu.SemaphoreType.DMA((2, 2))]),
        out_shape=jax.ShapeDtypeStruct((nseq, SUBLANE, width), F32),
        compiler_params=_params("arbitrary", "arbitrary"),
        name="sb_decode",
    )(page_table, qbd, bias_col, knew, vnew, cache_k, cache_v)
    return out[:, :n_new].reshape(nseq * n_new, width).astype(BF16)


def _ssd_kernel(x_ref, b_ref, c_ref, px_ref, pb_ref, pc_ref, wx_ref, wb_ref, wc_ref,
                bx_ref, bb_ref, bc_ref, dt_ref, dtb_ref, alog_ref, dskip_ref, z_ref, nw_ref, h0_ref,
                y_ref, hout_ref, h_scr, ex_scr, eb_scr, ec_scr, *, valid_len, n_chunks, rep, hd, gp):
    c = pl.program_id(2)
    cl = x_ref.shape[0]
    xw = rep * hd
    n = h_scr.shape[1]

    @pl.when(c == 0)
    def _():
        h_scr[...] = h0_ref[0].reshape(gp * xw, n)
        ex_scr[0:SUBLANE, :] = px_ref[0]
        eb_scr[0:SUBLANE, :] = pb_ref[0]
        ec_scr[0:SUBLANE, :] = pc_ref[0]

    def conv_silu(e_scr, raw_ref, w_ref, bias_ref):
        e_scr[SUBLANE:SUBLANE + cl, :] = raw_ref[...]
        acc = bias_ref[...]
        for i in range(CONV_TAPS):
            lo = SUBLANE - (CONV_TAPS - 1) + i
            acc = acc + e_scr[lo:lo + cl, :] * w_ref[i:i + 1, :]
        e_scr[0:SUBLANE, :] = e_scr[cl:cl + SUBLANE, :]
        return _silu(acc)

    xc_all = conv_silu(ex_scr, x_ref, wx_ref, bx_ref)
    bm_all = conv_silu(eb_scr, b_ref, wb_ref, bb_ref)
    cm_all = conv_silu(ec_scr, c_ref, wc_ref, bc_ref)

    row = c * cl + _iota((cl, 1), 0)
    t_idx = _iota((cl, cl), 0)
    s_idx = _iota((cl, cl), 1)
    tril = t_idx >= s_idx
    tril16 = tril.astype(BF16)
    lane = _iota((1, xw), 1)
    in_head = [(lane >= r * hd) & (lane < (r + 1) * hd) for r in range(rep)]
    groups = range(gp)

    xc = [xc_all[:, s * xw:(s + 1) * xw] for s in groups]
    bm16 = [bm_all[:, s * n:(s + 1) * n].astype(BF16) for s in groups]
    cm16 = [cm_all[:, s * n:(s + 1) * n].astype(BF16) for s in groups]
    h = [h_scr[s * xw:(s + 1) * xw, :] for s in groups]

    def step_sizes(s):
        dtr = dt_ref[0, s] + dtb_ref[s]
        return jnp.where(row < valid_len, jnp.maximum(dtr, 0.0) + _softplus_neg_abs(dtr), 0.0)

    dt = [step_sizes(s) for s in groups]
    cum = [_split_dot_left(tril16, dt[s] * -jnp.exp(alog_ref[s]), 3) for s in groups]
    g = [lax.dot_general(cm16[s], bm16[s], NT_DIMS, preferred_element_type=F32) for s in groups]
    inter = [lax.dot_general(cm16[s], h[s].astype(BF16), NT_DIMS, preferred_element_type=F32) for s in groups]
    cum_t = [cum[s].T for s in groups]
    dt_t = [dt[s].T for s in groups]

    def head_terms(s):
        cum_last = cum[s][cl - 1:cl, :]
        decay_last = jnp.exp(cum_last)
        ws, xrs, hdecay = [], [], []
        e_all = jnp.zeros((cl, xw), F32)
        tail_all = jnp.zeros((cl, xw), F32)
        for r in range(rep):
            ccol = cum[s][:, r:r + 1]
            decay = jnp.exp(jnp.where(tril, ccol - cum_t[s][r:r + 1, :], -jnp.inf))
            ws.append((g[s] * decay * dt_t[s][r:r + 1, :]).astype(BF16))
            xrs.append(jnp.where(in_head[r], xc[s], 0.0).astype(BF16))
            e_all = jnp.where(in_head[r], jnp.exp(ccol), e_all)
            tail_all = jnp.where(in_head[r], jnp.exp(cum_last[:, r:r + 1] - ccol) * dt[s][:, r:r + 1], tail_all)
            hdecay.append(jnp.broadcast_to(decay_last[:, r:r + 1], (hd, n)))
        return (jnp.concatenate(ws, axis=1), jnp.concatenate(xrs, axis=0), e_all, tail_all,
                jnp.concatenate(hdecay, axis=0))

    terms = [head_terms(s) for s in groups]
    intra = [jnp.dot(terms[s][0], terms[s][1], preferred_element_type=F32) for s in groups]
    y = [intra[s] + inter[s] * terms[s][2] + xc[s] * dskip_ref[s] for s in groups]

    def transposed(xt):
        return jnp.concatenate([xt[:, j * LANE:(j + 1) * LANE].T for j in range(xw // LANE)], axis=0)

    xt_t = [transposed(xc[s] * terms[s][3]).astype(BF16) for s in groups]
    h_new = [h[s] * terms[s][4] + jnp.dot(xt_t[s], bm16[s], preferred_element_type=F32) for s in groups]
    for s in groups:
        h_scr[s * xw:(s + 1) * xw, :] = h_new[s]

    for s in groups:
        lanes = slice(s * xw, (s + 1) * xw)
        gated = y[s] * _silu(z_ref[:, lanes])
        ms = jnp.mean(gated * gated, axis=-1, keepdims=True)
        y_ref[:, lanes] = (gated * lax.rsqrt(ms + RMS_EPS) * nw_ref[:, lanes]).astype(y_ref.dtype)

    @pl.when(c == n_chunks - 1)
    def _():
        hout_ref[0] = h_scr[...].reshape(hout_ref.shape[1:])


def _ssd(xbc, dt_raw, z, conv_prev, h0, valid_len, conv_w, conv_b, dt_bias, a_log, d_skip, norm_w, groups):
    nseq, heads, hd, n = h0.shape
    rows, ch = xbc.shape
    seq = rows // nseq
    rep = heads // groups
    xw = rep * hd
    ssm_w = heads * hd
    cl = CHUNK
    nc = seq // cl
    gp = SSD_GROUPS_PER_STEP
    assert seq % cl == 0 and n == LANE and xw % LANE == 0 and ch == ssm_w + 2 * groups * n
    assert groups % gp == 0 and ssm_w % (gp * n) == 0
    b_off = ssm_w // (gp * n)
    c_off = b_off + groups // gp

    prev = jnp.pad(conv_prev, ((0, 0), (SUBLANE - (CONV_TAPS - 1), 0), (0, 0)))
    dt_g = dt_raw.reshape(nseq, seq, groups, rep).transpose(0, 2, 1, 3)
    dt_g = jnp.pad(dt_g, ((0, 0), (0, 0), (0, 0), (0, LANE - rep)))

    def lane_pad(p):
        return jnp.pad(p.astype(F32).reshape(groups, 1, rep), ((0, 0), (0, 0), (0, LANE - rep)))

    dskip_rep = jnp.repeat(d_skip.astype(F32), hd).reshape(groups, 1, xw)
    cb = conv_b.reshape(1, ch)

    def rows_at(width, off):
        return pl.BlockSpec((cl, width), lambda b, g, c: (b * nc + c, off + g))

    def prev_at(width, off):
        return pl.BlockSpec((1, SUBLANE, width), lambda b, g, c: (b, 0, off + g))

    def taps_at(nrows, width, off):
        return pl.BlockSpec((nrows, width), lambda b, g, c: (0, off + g))

    par = pl.BlockSpec((gp, 1, LANE), lambda b, g, c: (g, 0, 0))
    state = pl.BlockSpec((1, gp * rep, hd, n), lambda b, g, c: (b, g, 0, 0))
    xs_w, bc_w = gp * xw, gp * n
    y, h = pl.pallas_call(
        functools.partial(_ssd_kernel, valid_len=valid_len, n_chunks=nc, rep=rep, hd=hd, gp=gp),
        grid=(nseq, groups // gp, nc),
        in_specs=[rows_at(xs_w, 0), rows_at(bc_w, b_off), rows_at(bc_w, c_off),
                  prev_at(xs_w, 0), prev_at(bc_w, b_off), prev_at(bc_w, c_off),
                  taps_at(CONV_TAPS, xs_w, 0), taps_at(CONV_TAPS, bc_w, b_off), taps_at(CONV_TAPS, bc_w, c_off),
                  taps_at(1, xs_w, 0), taps_at(1, bc_w, b_off), taps_at(1, bc_w, c_off),
                  pl.BlockSpec((1, gp, cl, LANE), lambda b, g, c: (b, g, c, 0)),
                  par, par,
                  pl.BlockSpec((gp, 1, xw), lambda b, g, c: (g, 0, 0)),
                  rows_at(xs_w, 0),
                  taps_at(1, xs_w, 0),
                  state],
        out_specs=[rows_at(xs_w, 0), state],
        out_shape=[jax.ShapeDtypeStruct((rows, ssm_w), BF16), jax.ShapeDtypeStruct(h0.shape, F32)],
        scratch_shapes=[pltpu.VMEM((gp * xw, n), F32),
                        pltpu.VMEM((cl + SUBLANE, xs_w), F32),
                        pltpu.VMEM((cl + SUBLANE, bc_w), F32),
                        pltpu.VMEM((cl + SUBLANE, bc_w), F32)],
        compiler_params=_params("arbitrary", "arbitrary", "arbitrary"),
        name="ssd",
    )(xbc, xbc, xbc, prev, prev, prev, conv_w, conv_w, conv_w, cb, cb, cb,
      dt_g, lane_pad(dt_bias), lane_pad(a_log), dskip_rep, z, norm_w.reshape(1, ssm_w), h0)
    return y, h


def _pool_kernel(u_ref, prev_ref, w_ref, s_ref, o_ref, ext_scr, *, pos0, gd):
    i = pl.program_id(1)
    tr = u_ref.shape[0]

    @pl.when(i == 0)
    def _():
        ext_scr[0:POOL_HIST, :] = prev_ref[0]

    ext_scr[POOL_HIST:POOL_HIST + tr, :] = u_ref[...]
    pos = (pos0 + i * tr + _iota((tr, 1), 0)).astype(F32)
    for g, win in enumerate(POOL_WINDOWS):
        lanes = slice(g * gd, (g + 1) * gd)
        total = ext_scr[POOL_HIST:POOL_HIST + tr, lanes]
        for j in range(1, win):
            total = total + ext_scr[POOL_HIST - j:POOL_HIST - j + tr, lanes]
        count = jnp.minimum(pos + 1.0, float(win))
        dlt = total / count - u_ref[:, lanes]
        y = jnp.dot(dlt.astype(BF16), w_ref[g], preferred_element_type=F32)
        o_ref[:, lanes] = (y * s_ref[:, lanes]).astype(o_ref.dtype)
    ext_scr[0:POOL_HIST, :] = ext_scr[tr:tr + POOL_HIST, :]


def _pool(u, prev, pos0, w_pool16, pool_scale, tr):
    nseq = prev.shape[0]
    rows, width = u.shape
    seq = rows // nseq
    tr = min(tr, seq)
    nt = seq // tr
    ng, gd, _ = w_pool16.shape
    assert seq % tr == 0 and ng == len(POOL_WINDOWS) and ng * gd == width
    prev16 = jnp.pad(prev, ((0, 0), (POOL_HIST - prev.shape[1], 0), (0, 0)))
    return pl.pallas_call(
        functools.partial(_pool_kernel, pos0=pos0, gd=gd),
        grid=(nseq, nt),
        in_specs=[pl.BlockSpec((tr, width), lambda b, i: (b * nt + i, 0)),
                  pl.BlockSpec((1, POOL_HIST, width), lambda b, i: (b, 0, 0)),
                  pl.BlockSpec((ng, gd, gd), lambda b, i: (0, 0, 0)),
                  pl.BlockSpec((1, width), lambda b, i: (0, 0))],
        out_specs=pl.BlockSpec((tr, width), lambda b, i: (b * nt + i, 0)),
        out_shape=jax.ShapeDtypeStruct((rows, width), BF16),
        scratch_shapes=[pltpu.VMEM((tr + POOL_HIST, width), F32)],
        compiler_params=_params("arbitrary", "arbitrary"),
        name="pool",
    )(u, prev16, w_pool16, pool_scale.reshape(1, width))


def _gla_kernel(q_ref, k_ref, v_ref, r_ref, glr_ref, w2_ref, b2_ref, nw_ref, s0_ref,
                o_ref, sout_ref, s_scr, *, valid_len, n_chunks, scale, hp):
    c = pl.program_id(2)
    cl = q_ref.shape[0]
    dk = s_scr.shape[1]
    dv = s_scr.shape[2]
    heads = range(hp)

    @pl.when(c == 0)
    def _():
        s_scr[...] = s0_ref[0]

    row = c * cl + _iota((cl, 1), 0)
    valid = row < valid_len
    pre = jnp.dot(glr_ref[...], w2_ref[...], preferred_element_type=F32) + b2_ref[...]
    lg_all = (jnp.minimum(pre, 0.0) - _softplus_neg_abs(pre)) / GLA_TAU
    lg_all = jnp.where(valid, lg_all, 0.0)
    qs_all = q_ref[...] * scale
    k_all = jnp.where(valid, k_ref[...], 0.0)
    kl = [slice(h * dk, (h + 1) * dk) for h in heads]
    vl = [slice(h * dv, (h + 1) * dv) for h in heads]
    lg = [lg_all[:, kl[h]] for h in heads]
    qs = [qs_all[:, kl[h]] for h in heads]
    k = [k_all[:, kl[h]] for h in heads]
    v16 = [v_ref[:, vl[h]].astype(BF16) for h in heads]

    t_idx = _iota((cl, cl), 0)
    s_idx = _iota((cl, cl), 1)
    r_idx = _iota((cl, 1), 0)
    def masked_qk(qd, kd, mask):
        return jnp.where(mask, lax.dot_general(qd.astype(BF16), kd.astype(BF16), NT_DIMS,
                                               preferred_element_type=F32), 0.0)

    att = [masked_qk(qs[h], k[h], t_idx == s_idx) for h in heads]
    pre_b = lg
    suf_b = [jnp.zeros_like(lg[h]) for h in heads]
    tot_b = lg
    bsz = 1
    while bsz < cl:
        siblings = ((_div(t_idx, 2 * bsz) == _div(s_idx, 2 * bsz))
                    & (_mod(_div(t_idx, bsz), 2) == 1) & (_mod(_div(s_idx, bsz), 2) == 0))
        att = [att[h] + masked_qk(qs[h] * jnp.exp(pre_b[h]), k[h] * jnp.exp(suf_b[h]), siblings) for h in heads]
        is_right = _mod(_div(r_idx, bsz), 2) == 1
        left_tot = [pltpu.roll(tot_b[h], bsz, axis=0) for h in heads]
        right_tot = [pltpu.roll(tot_b[h], cl - bsz, axis=0) for h in heads]
        pre_b = [pre_b[h] + jnp.where(is_right, left_tot[h], 0.0) for h in heads]
        suf_b = [suf_b[h] + jnp.where(is_right, 0.0, right_tot[h]) for h in heads]
        tot_b = [tot_b[h] + jnp.where(is_right, left_tot[h], right_tot[h]) for h in heads]
        bsz *= 2

    s = [s_scr[h] for h in heads]
    qd = [(qs[h] * jnp.exp(pre_b[h])).astype(BF16) for h in heads]
    kd_t = [(k[h] * jnp.exp(suf_b[h])).T.astype(BF16) for h in heads]
    o = [jnp.dot(att[h].astype(BF16), v16[h], preferred_element_type=F32)
         + jnp.dot(qd[h], s[h].astype(BF16), preferred_element_type=F32) for h in heads]
    chunk_decay = [jnp.exp(tot_b[h].T) for h in heads]
    s_new = [s[h] * jnp.concatenate([chunk_decay[h]] * (dv // cl), axis=1)
             + jnp.dot(kd_t[h], v16[h], preferred_element_type=F32) for h in heads]
    for h in heads:
        s_scr[h] = s_new[h]

    for h in heads:
        ms = jnp.mean(o[h] * o[h], axis=-1, keepdims=True)
        o_ref[:, vl[h]] = (o[h] * lax.rsqrt(ms + RMS_EPS) * nw_ref[:, vl[h]]
                           * _silu(r_ref[:, vl[h]])).astype(o_ref.dtype)

    @pl.when(c == n_chunks - 1)
    def _():
        sout_ref[0] = s_scr[...]


def _gla(q, k, v, r, glr, w2, b2, norm_w, s0, valid_len):
    nseq, heads, dk, dv = s0.shape
    rows = q.shape[0]
    seq = rows // nseq
    cl = CHUNK
    nc = seq // cl
    hp = GLA_HEADS_PER_STEP
    assert seq % cl == 0 and dk == cl and dv % cl == 0 and glr.shape[1] == LANE and heads % hp == 0

    def rows_at(width, col):
        return pl.BlockSpec((cl, width), (lambda b, h, c: (b * nc + c, h)) if col else (lambda b, h, c: (b * nc + c, 0)))

    def head_at(nrows, width):
        return pl.BlockSpec((nrows, width), lambda b, h, c: (0, h))

    state = pl.BlockSpec((1, hp, dk, dv), lambda b, h, c: (b, h, 0, 0))
    kw, vw = hp * dk, hp * dv
    return pl.pallas_call(
        functools.partial(_gla_kernel, valid_len=valid_len, n_chunks=nc, scale=dk ** -0.5, hp=hp),
        grid=(nseq, heads // hp, nc),
        in_specs=[rows_at(kw, True), rows_at(kw, True), rows_at(vw, True), rows_at(vw, True),
                  rows_at(LANE, False), head_at(LANE, kw), head_at(1, kw), head_at(1, vw), state],
        out_specs=[rows_at(vw, True), state],
        out_shape=[jax.ShapeDtypeStruct((rows, heads * dv), BF16), jax.ShapeDtypeStruct(s0.shape, F32)],
        scratch_shapes=[pltpu.VMEM((hp, dk, dv), F32)],
        compiler_params=_params("arbitrary", "arbitrary", "arbitrary"),
        name="gla",
    )(q, k, v, r, glr, w2, b2.reshape(1, heads * dk), norm_w.reshape(1, heads * dv), s0)


def _pad_seq(a, nseq, seq_pad):
    n = a.shape[0] // nseq
    a3 = jnp.pad(a.reshape(nseq, n, a.shape[1]), ((0, 0), (0, seq_pad - n), (0, 0)))
    return a3.reshape(nseq * seq_pad, a.shape[1])


def _unpad_seq(a, nseq, n):
    return a.reshape(nseq, -1, a.shape[1])[:, :n].reshape(nseq * n, a.shape[1])


def _project(xp16, xs16, w, sizes, name):
    outs_p, outs_s = [], []
    off = 0
    w_nk = w.T
    for idx, size in enumerate(sizes):
        op, os_ = _proj([xp16], [xs16], w_nk, True, off, size, F32, PROJ_TM, PROJ_TN, f"{name}_{idx}")
        off += size
        outs_p.append(op)
        outs_s.append(os_)
    return outs_p, outs_s


def kernel(x_prompt, x_sample, cache_k, cache_v, page_table, state_conv, state_ssm, state_pool, state_gla, w_in0, sb_bias, conv_w, conv_b, dt_bias, a_log, d_skip, ssm_norm_w, w_out0, w_in1, w_pool, pool_scale, gla_w2, gla_b2, gla_norm_w, w_out1, ln_g, ln_b, w_gu, w_down):
    bp, seq, dm = x_prompt.shape
    bs, n_new, _ = x_sample.shape
    depth = ln_g.shape[0]
    alpha = (2 * depth) ** 0.25
    sb_heads, sb_d = cache_k.shape[2], cache_k.shape[3]
    sb_w = sb_heads * sb_d
    ssm_heads, ssm_hd, ssm_n = state_ssm.shape[1:]
    ssm_w = ssm_heads * ssm_hd
    conv_ch = state_conv.shape[2]
    groups = (conv_ch - ssm_w) // (2 * ssm_n)
    pool_w = state_pool.shape[2]
    gla_heads, gla_dk, gla_dv = state_gla.shape[1:]
    gla_kw, gla_vw = gla_heads * gla_dk, gla_heads * gla_dv
    gla_rank = gla_w2.shape[0]
    past_len = page_table.shape[1] * cache_k.shape[1]
    seq_pad = CHUNK

    xp = x_prompt.reshape(bp * seq, dm)
    xs = x_sample.reshape(bs * n_new, dm)
    xp16, xs16 = xp.astype(BF16), xs.astype(BF16)

    def mix_ffn_ln(xp, xs, mixed_p, mixed_s, w_out, layer):
        mp, ms = _proj(mixed_p, mixed_s, w_out, False, 0, dm, F32, PROJ_TM, PROJ_TN, f"out{layer}")
        xp, xp16 = _ln(mp, ln_g[layer, 0], ln_b[layer, 0], LN_TM, res=xp, alpha=alpha)
        xs, xs16 = _ln(ms, ln_g[layer, 0], ln_b[layer, 0], LN_TM, res=xs, alpha=alpha)
        hp, hs, w_down16 = _gate_up(xp16, xs16, w_gu, w_down, layer, PROJ_TM, GU_TN)
        yp = _mm_res(hp, w_down16, xp, alpha, DOWN_TM, DOWN_TN)
        ys = _mm_res(hs, w_down16, xs, alpha, DOWN_TM, DOWN_TN)
        xp, xp16 = _ln(yp, ln_g[layer, 1], ln_b[layer, 1], LN_TM)
        xs, xs16 = _ln(ys, ln_g[layer, 1], ln_b[layer, 1], LN_TM)
        return xp, xp16, xs, xs16

    split0 = (sb_w, sb_w, sb_w, ssm_w, conv_ch, ssm_heads)
    (qp, kp, vp, zp, xbcp, dtp), (qs, ks, vs, zs, xbcs, dts) = _project(xp16, xs16, w_in0, split0, "in0")
    dtp, dts = dtp[:, :ssm_heads], dts[:, :ssm_heads]

    oap = _sb_prompt(qp, kp, vp, sb_bias, bp, SB_T, SB_HEADS_PER_STEP)
    oas = _sb_decode(qs, ks, vs, cache_k, cache_v, page_table, sb_bias)

    ssd_w = (conv_w, conv_b, dt_bias, a_log, d_skip, ssm_norm_w, groups)
    yp, ssm_p = _ssd(xbcp, dtp, zp, jnp.zeros((bp, CONV_TAPS - 1, conv_ch), F32),
                     jnp.zeros((bp,) + state_ssm.shape[1:], F32), seq, *ssd_w)
    ys, ssm_s = _ssd(_pad_seq(xbcs, bs, seq_pad), _pad_seq(dts, bs, seq_pad), _pad_seq(zs, bs, seq_pad),
                     state_conv, state_ssm, n_new, *ssd_w)
    ys = _unpad_seq(ys, bs, n_new)

    xp, xp16, xs, xs16 = mix_ffn_ln(xp, xs, [oap, yp], [oas, ys], w_out0, 0)

    split1 = (pool_w, gla_kw, gla_kw, gla_vw, gla_vw, gla_rank)
    (up, gqp, gkp, gvp, grp, glrp), (us, gqs, gks, gvs, grs, glrs) = _project(xp16, xs16, w_in1, split1, "in1")

    w_pool16 = w_pool.astype(BF16)
    ocp = _pool(up, jnp.zeros((bp, POOL_HIST - 1, pool_w), F32), 0, w_pool16, pool_scale, POOL_TM)
    us8 = _pad_seq(us, bs, SUBLANE)
    ocs = _unpad_seq(_pool(us8, state_pool, past_len, w_pool16, pool_scale, SUBLANE), bs, n_new)

    w2_16 = jnp.pad(gla_w2, ((0, LANE - gla_rank), (0, 0))).astype(BF16)
    gla_w = (w2_16, gla_b2, gla_norm_w)
    odp, gla_p = _gla(gqp, gkp, gvp, grp, glrp.astype(BF16), *gla_w,
                      jnp.zeros((bp,) + state_gla.shape[1:], F32), seq)
    pad = lambda a: _pad_seq(a, bs, seq_pad)
    ods, gla_s = _gla(pad(gqs), pad(gks), pad(gvs), pad(grs), pad(glrs).astype(BF16), *gla_w, state_gla, n_new)
    ods = _unpad_seq(ods, bs, n_new)

    xp, xp16, xs, xs16 = mix_ffn_ln(xp, xs, [ocp, odp], [ocs, ods], w_out1, 1)

    hd4 = lambda a, nseq: a.reshape(nseq, -1, sb_heads, sb_d)
    conv_p = xbcp.reshape(bp, seq, conv_ch)[:, seq - (CONV_TAPS - 1):]
    conv_s = jnp.concatenate([state_conv, xbcs.reshape(bs, n_new, conv_ch)], axis=1)[:, n_new:]
    pool_p = up.reshape(bp, seq, pool_w)[:, seq - (POOL_HIST - 1):]
    pool_s = jnp.concatenate([state_pool, us.reshape(bs, n_new, pool_w)], axis=1)[:, n_new:]
    return (xp.reshape(bp, seq, dm), xs.reshape(bs, n_new, dm),
            hd4(kp, bp), hd4(vp, bp), hd4(ks, bs), hd4(vs, bs),
            conv_p, conv_s, ssm_p, ssm_s, pool_p, pool_s, gla_p, gla_s)
```

```python
import functools

import jax
import jax.numpy as jnp
from jax import lax
from jax.experimental import pallas as pl
from jax.experimental.pallas import tpu as pltpu

F32 = jnp.float32
BF16 = jnp.bfloat16

LANE = 128
SUBLANE = 8
VMEM_LIMIT = 56 * 1024 * 1024

LN_EPS = 1e-5
RMS_EPS = 1e-5
GLA_TAU = 16.0
PAGE = 128
POOL_WINDOWS = (2, 4, 8, 16)
POOL_HIST = 16
CONV_TAPS = 4
CHUNK = 128

PROJ_TM, PROJ_TN = 1024, 512
GU_TN = 256
DOWN_TM, DOWN_TN = 512, 512
LN_TM = 256
SB_T, SB_HEADS_PER_STEP = 256, 8
POOL_TM = 256
DECODE_PAGES_PER_STEP = 8
SSD_GROUPS_PER_STEP = 2
GLA_HEADS_PER_STEP = 4

NT_DIMS = (((1,), (1,)), ((), ()))


def _params(*sem):
    return pltpu.CompilerParams(dimension_semantics=sem, vmem_limit_bytes=VMEM_LIMIT)


def _sigmoid(x):
    return 1.0 / (1.0 + jnp.exp(-x))


def _silu(x):
    return x * _sigmoid(x)


def _softplus_neg_abs(x):
    return jnp.log1p(jnp.exp(-jnp.abs(x)))


def _iota(shape, dim):
    return lax.broadcasted_iota(jnp.int32, shape, dim)


def _div(x, c):
    return lax.shift_right_logical(x, (c.bit_length() - 1)) if c & (c - 1) == 0 else x // c


def _mod(x, c):
    return (x & (c - 1)) if c & (c - 1) == 0 else x % c


def _bf16_pieces(x, terms):
    pieces = [x.astype(BF16)]
    for _ in range(terms - 1):
        x = x - pieces[-1].astype(F32)
        pieces.append(x.astype(BF16))
    return pieces


def _add_all(terms):
    return functools.reduce(lambda a, b: a + b, terms)


def _split_dot_left(m01, x, terms):
    return _add_all([jnp.dot(m01, p, preferred_element_type=F32) for p in _bf16_pieces(x, terms)])


def _mm_res_kernel(x_ref, w_ref, r_ref, o_ref, *, alpha):
    o_ref[...] = alpha * r_ref[...] + jnp.dot(x_ref[...], w_ref[...], preferred_element_type=F32)


def _mm_res(x, w, res, alpha, tm, tn):
    m, k = x.shape
    n = w.shape[1]
    tm = min(tm, m)
    tn = min(tn, n)
    assert m % tm == 0 and n % tn == 0
    tile = pl.BlockSpec((tm, tn), lambda i, j: (i, j))
    return pl.pallas_call(
        functools.partial(_mm_res_kernel, alpha=alpha),
        grid=(m // tm, n // tn),
        in_specs=[pl.BlockSpec((tm, k), lambda i, j: (i, 0)),
                  pl.BlockSpec((k, tn), lambda i, j: (0, j)),
                  tile],
        out_specs=tile,
        out_shape=jax.ShapeDtypeStruct((m, n), F32),
        compiler_params=_params("arbitrary", "arbitrary"),
        name="down",
    )(x, w, res)


def _dot_parts(x_refs, w16, w_is_nk):
    terms, off = [], 0
    for x_ref in x_refs:
        k = x_ref.shape[1]
        if w_is_nk:
            terms.append(lax.dot_general(x_ref[...], w16[:, off:off + k], NT_DIMS, preferred_element_type=F32))
        else:
            terms.append(jnp.dot(x_ref[...], w16[off:off + k, :], preferred_element_type=F32))
        off += k
    return _add_all(terms)


def _proj_kernel(*refs, n_parts, w_is_nk, col0, ncols, nb, alpha):
    xps, xss = refs[:n_parts], refs[n_parts:2 * n_parts]
    w_ref = refs[2 * n_parts]
    res = refs[2 * n_parts + 1:-5]
    op_ref, os_ref, wstage, sem, w16 = refs[-5:]

    def finish(acc, res_ref, o_ref):
        if alpha is not None:
            acc = alpha * res_ref[...] + acc
        o_ref[...] = acc.astype(o_ref.dtype)

    n = pl.program_id(0)
    m = pl.program_id(1)
    tn = op_ref.shape[1]
    nv = min(tn, ncols)

    def tile_copy(t, slot):
        cols = pl.ds(pl.multiple_of(col0 + t * tn, nv), nv)
        if w_is_nk:
            return pltpu.make_async_copy(w_ref.at[cols, :], wstage.at[slot, pl.ds(0, nv), :], sem.at[slot])
        return pltpu.make_async_copy(w_ref.at[:, cols], wstage.at[slot, :, pl.ds(0, nv)], sem.at[slot])

    @pl.when(m == 0)
    def _():
        slot = n % 2

        @pl.when(n == 0)
        def _():
            tile_copy(0, 0).start()

        tile_copy(n, slot).wait()
        if nv == tn:
            w16[...] = wstage[slot].astype(BF16)
        elif w_is_nk:
            w16[0:nv, :] = wstage[slot, 0:nv, :].astype(BF16)
            w16[nv:tn, :] = jnp.zeros((tn - nv, w16.shape[1]), BF16)
        else:
            w16[:, 0:nv] = wstage[slot, :, 0:nv].astype(BF16)
            w16[:, nv:tn] = jnp.zeros((w16.shape[0], tn - nv), BF16)

        @pl.when(n + 1 < nb)
        def _():
            tile_copy(n + 1, 1 - slot).start()

        finish(_dot_parts(xss, w16, w_is_nk), res[1] if res else None, os_ref)

    finish(_dot_parts(xps, w16, w_is_nk), res[0] if res else None, op_ref)


def _col_tile(col0, ncols, tn_max):
    if ncols < LANE:
        assert col0 % LANE == 0
        return LANE
    tn = tn_max
    while ncols % tn or col0 % tn:
        tn //= 2
    assert tn >= LANE
    return tn


def _proj(xps, xss, w, w_is_nk, col0, ncols, out_dtype, tm, tn_max, name, res=None, alpha=None):
    m = xps[0].shape[0]
    s = xss[0].shape[0]
    k = w.shape[1] if w_is_nk else w.shape[0]
    tm = min(tm, m)
    tn = _col_tile(col0, ncols, tn_max)
    nb = pl.cdiv(ncols, tn)
    assert m % tm == 0 and sum(x.shape[1] for x in xps) == k and min(tn, ncols) % SUBLANE == 0
    tile_shape = (tn, k) if w_is_nk else (k, tn)
    out_tiles = [pl.BlockSpec((tm, tn), lambda n, i: (i, n)), pl.BlockSpec((s, tn), lambda n, i: (0, n))]
    assert (res is None) == (alpha is None)
    return pl.pallas_call(
        functools.partial(_proj_kernel, n_parts=len(xps), w_is_nk=w_is_nk, col0=col0, ncols=ncols, nb=nb,
                          alpha=alpha),
        grid=(nb, m // tm),
        in_specs=([pl.BlockSpec((tm, x.shape[1]), lambda n, i: (i, 0)) for x in xps]
                  + [pl.BlockSpec((s, x.shape[1]), lambda n, i: (0, 0)) for x in xss]
                  + [pl.BlockSpec(memory_space=pl.ANY)]
                  + (out_tiles if res is not None else [])),
        out_specs=out_tiles,
        out_shape=[jax.ShapeDtypeStruct((m, nb * tn), out_dtype), jax.ShapeDtypeStruct((s, nb * tn), out_dtype)],
        scratch_shapes=[pltpu.VMEM((2,) + tile_shape, F32), pltpu.SemaphoreType.DMA((2,)),
                        pltpu.VMEM(tile_shape, BF16)],
        compiler_params=_params("arbitrary", "arbitrary"),
        name=name,
    )(*xps, *xss, w, *(res or ()))


def _gu_kernel(xp_ref, xs_ref, wgu_ref, wd_ref, op_ref, os_ref, wd16_ref, wstage, sem, wg16, wu16,
               *, layer, nb):
    n = pl.program_id(0)
    m = pl.program_id(1)
    tn = wg16.shape[1]

    def tile_copies(t, slot):
        cols = [pl.ds(pl.multiple_of((t + half * nb) * tn, tn), tn) for half in range(2)]
        return [pltpu.make_async_copy(wgu_ref.at[layer, :, cols[half]], wstage.at[slot, half], sem.at[slot, half])
                for half in range(2)]

    def act(x_ref):
        x = x_ref[...]
        g = jnp.dot(x, wg16[...], preferred_element_type=F32)
        u = jnp.dot(x, wu16[...], preferred_element_type=F32)
        return (_silu(g) * u).astype(op_ref.dtype)

    @pl.when(m == 0)
    def _():
        slot = n % 2

        @pl.when(n == 0)
        def _():
            for c in tile_copies(0, 0):
                c.start()

        for c in tile_copies(n, slot):
            c.wait()
        wg16[...] = wstage[slot, 0].astype(BF16)
        wu16[...] = wstage[slot, 1].astype(BF16)

        @pl.when(n + 1 < nb)
        def _():
            for c in tile_copies(n + 1, 1 - slot):
                c.start()

        os_ref[...] = act(xs_ref)

    wd16_ref[...] = wd_ref[...].astype(BF16)
    op_ref[...] = act(xp_ref)


def _gate_up(xp, xs, w_gu, w_down, layer, tm, tn):
    m, k = xp.shape
    s = xs.shape[0]
    f = w_gu.shape[2] // 2
    dm = w_down.shape[2]
    tm = min(tm, m)
    tn = min(tn, f)
    assert m % tm == 0 and f % tn == 0 and w_down.shape[1] == f
    nb = f // tn
    mt = m // tm
    td = f // (nb * mt)
    assert td * nb * mt == f and td % (2 * SUBLANE) == 0
    return pl.pallas_call(
        functools.partial(_gu_kernel, layer=layer, nb=nb),
        grid=(nb, mt),
        in_specs=[pl.BlockSpec((tm, k), lambda n, i: (i, 0)),
                  pl.BlockSpec((s, k), lambda n, i: (0, 0)),
                  pl.BlockSpec(memory_space=pl.ANY),
                  pl.BlockSpec((None, td, dm), lambda n, i: (layer, n * mt + i, 0))],
        out_specs=[pl.BlockSpec((tm, tn), lambda n, i: (i, n)),
                   pl.BlockSpec((s, tn), lambda n, i: (0, n)),
                   pl.BlockSpec((td, dm), lambda n, i: (n * mt + i, 0))],
        out_shape=[jax.ShapeDtypeStruct((m, f), BF16), jax.ShapeDtypeStruct((s, f), BF16),
                   jax.ShapeDtypeStruct((f, dm), BF16)],
        scratch_shapes=[pltpu.VMEM((2, 2, k, tn), F32), pltpu.SemaphoreType.DMA((2, 2)),
                        pltpu.VMEM((k, tn), BF16), pltpu.VMEM((k, tn), BF16)],
        compiler_params=_params("arbitrary", "arbitrary"),
        name="gate_up",
    )(xp, xs, w_gu, w_down)


def _ln_kernel(y_ref, g_ref, b_ref, o_ref, ob_ref):
    y = y_ref[...]
    mu = jnp.mean(y, axis=-1, keepdims=True)
    yc = y - mu
    var = jnp.mean(yc * yc, axis=-1, keepdims=True)
    out = yc * lax.rsqrt(var + LN_EPS) * g_ref[...] + b_ref[...]
    o_ref[...] = out
    ob_ref[...] = out.astype(BF16)


def _ln(y, g, b, tm):
    rows, d = y.shape
    tm = min(tm, rows)
    assert rows % tm == 0
    row = pl.BlockSpec((tm, d), lambda i: (i, 0))
    vec = pl.BlockSpec((1, d), lambda i: (0, 0))
    return pl.pallas_call(
        _ln_kernel,
        grid=(rows // tm,),
        in_specs=[row, vec, vec],
        out_specs=[row, row],
        out_shape=[jax.ShapeDtypeStruct((rows, d), F32), jax.ShapeDtypeStruct((rows, d), BF16)],
        compiler_params=_params("arbitrary"),
        name="layer_norm",
    )(y, g.reshape(1, d), b.reshape(1, d))


LOG2_E = 1.4426950408889634


def _sb_tile(z2, mask, u01, carry):
    neg_log_keep = jnp.maximum(z2, 0.0) + jnp.log2(1.0 + jnp.exp2(-jnp.abs(z2)))
    if mask is not None:
        neg_log_keep = jnp.where(mask, neg_log_keep, 0.0)
    right = jnp.dot(neg_log_keep.astype(BF16), u01, preferred_element_type=F32) + carry
    w = jnp.exp2(z2 - neg_log_keep - right)
    if mask is not None:
        w = jnp.where(mask, w, 0.0)
    return w, carry + jnp.sum(neg_log_keep, axis=1, keepdims=True)


def _sb_prompt_kernel(bias_ref, q_ref, k_ref, v_ref, o_ref, *, t, hp, scale):
    g = pl.program_id(1)
    i = pl.program_id(2)
    d = q_ref.shape[1] // hp
    rows = _iota((t, t), 0)
    cols = _iota((t, t), 1)
    u01 = (rows > cols).astype(BF16)
    lanes = [slice(s * d, (s + 1) * d) for s in range(hp)]
    qs = [q_ref[:, ln].astype(BF16) for ln in lanes]
    scale2 = scale * LOG2_E
    biases2 = [bias_ref[g * hp + s] * LOG2_E for s in range(hp)]

    def tiles(j, mask, state):
        keys = pl.ds(pl.multiple_of(j * t, t), t)
        zs = [lax.dot_general(qs[s], k_ref[keys, lanes[s]].astype(BF16), NT_DIMS, preferred_element_type=F32)
              * scale2 + biases2[s] for s in range(hp)]
        ws = [_sb_tile(zs[s], mask, u01, state[s][0]) for s in range(hp)]
        return tuple(
            (ws[s][1], state[s][1] + jnp.dot(ws[s][0].astype(BF16), v_ref[keys, lanes[s]].astype(BF16),
                                             preferred_element_type=F32))
            for s in range(hp))

    state = tuple((jnp.zeros((t, 1), F32), jnp.zeros((t, d), F32)) for _ in range(hp))
    state = tiles(i, cols < rows, state)
    state = lax.fori_loop(0, i, lambda it, st: tiles(i - 1 - it, None, st), state)
    for s in range(hp):
        o_ref[:, lanes[s]] = state[s][1].astype(o_ref.dtype)


def _sb_prompt(q, k, v, sb_bias, nseq, t, hp):
    rows, width = q.shape
    heads = sb_bias.shape[0]
    d = width // heads
    seq = rows // nseq
    t = min(t, seq)
    nq = seq // t
    assert seq % t == 0 and d == LANE and heads % hp == 0
    qspec = pl.BlockSpec((t, hp * d), lambda b, g, i, bias: (b * nq + i, g))
    kvspec = pl.BlockSpec((seq, hp * d), lambda b, g, i, bias: (b, g))
    return pl.pallas_call(
        functools.partial(_sb_prompt_kernel, t=t, hp=hp, scale=d ** -0.5),
        grid_spec=pltpu.PrefetchScalarGridSpec(
            num_scalar_prefetch=1,
            grid=(nseq, heads // hp, nq),
            in_specs=[qspec, kvspec, kvspec],
            out_specs=qspec),
        out_shape=jax.ShapeDtypeStruct((rows, width), BF16),
        compiler_params=_params("arbitrary", "arbitrary", "arbitrary"),
        name="sb_prompt",
    )(sb_bias, q, k, v)


def _sb_decode_kernel(pt_ref, qbd_ref, bias_ref, knew_ref, vnew_ref, kc_ref, vc_ref, o_ref,
                      carry_scr, acc_scr, kbuf, vbuf, sem, *, heads, steps, scale):
    b = pl.program_id(0)
    p = pl.program_id(1)
    nq = qbd_ref.shape[1]
    nq_per_head = nq // heads
    width = qbd_ref.shape[2]
    d = width // heads
    pps = kbuf.shape[1]
    n_pages = (steps - 1) * pps

    def page_copies(step, slot):
        copies = []
        for j in range(pps):
            page = pt_ref[b, n_pages - step * pps + j]
            for h in range(heads):
                copies.append(pltpu.make_async_copy(kc_ref.at[page, :, h, :], kbuf.at[slot, j, h], sem.at[0, slot]))
                copies.append(pltpu.make_async_copy(vc_ref.at[page, :, h, :], vbuf.at[slot, j, h], sem.at[1, slot]))
        return copies

    @pl.when(p == 0)
    def _():
        carry_scr[...] = jnp.zeros_like(carry_scr)
        acc_scr[...] = jnp.zeros_like(acc_scr)

    @pl.when(p + 1 < steps)
    def _():
        for c in page_copies(p + 1, (p + 1) % 2):
            c.start()

    def step(kcat, vcat, mask):
        nk = kcat.shape[0]
        u01 = (_iota((nk, nk), 0) > _iota((nk, nk), 1)).astype(BF16)
        z2 = (lax.dot_general(qbd_ref[0], kcat, NT_DIMS, preferred_element_type=F32) * (scale * LOG2_E)
              + bias_ref[...] * LOG2_E)
        w, carry = _sb_tile(z2, mask, u01, carry_scr[...])
        carry_scr[...] = carry
        acc_scr[...] += jnp.dot(w.astype(BF16), vcat, preferred_element_type=F32)

    @pl.when(p == 0)
    def _():
        qi = _div(_iota((nq, PAGE), 0), heads)
        kj = _iota((nq, PAGE), 1)
        step(knew_ref[0].astype(BF16), vnew_ref[0].astype(BF16), kj < qi)

    @pl.when(p > 0)
    def _():
        slot = p % 2
        for c in page_copies(p, slot):
            c.wait()

        def cat(buf):
            return jnp.concatenate(
                [jnp.concatenate([buf[slot, j, h].astype(BF16) for h in range(heads)], axis=1)
                 for j in range(pps)], axis=0)

        step(cat(kbuf), cat(vbuf), None)

    @pl.when(p == steps - 1)
    def _():
        acc = acc_scr[...]
        keep = _mod(_iota((nq, width), 0), heads) == _div(_iota((nq, width), 1), d)
        sel = jnp.where(keep, acc, 0.0)
        out_row = _iota((o_ref.shape[1], width), 0)
        out = jnp.zeros((o_ref.shape[1], width), F32)
        for t in range(nq_per_head):
            head_sum = jnp.sum(sel[t * heads:(t + 1) * heads], axis=0, keepdims=True)
            out = jnp.where(out_row == t, head_sum, out)
        o_ref[0] = out


def _sb_decode(q, k, v, cache_k, cache_v, page_table, sb_bias):
    nseq, n_pages = page_table.shape
    heads = sb_bias.shape[0]
    width = q.shape[1]
    d = width // heads
    n_new = q.shape[0] // nseq
    nq = n_new * heads
    assert cache_k.shape[1] == PAGE and d == LANE and n_new <= SUBLANE
    q4 = q.reshape(nseq, n_new, heads, 1, d)
    eye = jnp.eye(heads, dtype=F32).reshape(1, 1, heads, heads, 1)
    qbd = (q4 * eye).reshape(nseq, nq, width).astype(BF16)
    bias_col = jnp.tile(sb_bias.astype(F32), n_new).reshape(nq, 1)
    pad = ((0, 0), (0, PAGE - n_new), (0, 0))
    knew = jnp.pad(k.reshape(nseq, n_new, width), pad)
    vnew = jnp.pad(v.reshape(nseq, n_new, width), pad)
    pps = DECODE_PAGES_PER_STEP
    assert n_pages % pps == 0
    steps = n_pages // pps + 1

    seq3 = lambda b, p, pt: (b, 0, 0)
    out = pl.pallas_call(
        functools.partial(_sb_decode_kernel, heads=heads, steps=steps, scale=d ** -0.5),
        grid_spec=pltpu.PrefetchScalarGridSpec(
            num_scalar_prefetch=1,
            grid=(nseq, steps),
            in_specs=[pl.BlockSpec((1, nq, width), seq3),
                      pl.BlockSpec((nq, 1), lambda b, p, pt: (0, 0)),
                      pl.BlockSpec((1, PAGE, width), seq3),
                      pl.BlockSpec((1, PAGE, width), seq3),
                      pl.BlockSpec(memory_space=pl.ANY),
                      pl.BlockSpec(memory_space=pl.ANY)],
            out_specs=pl.BlockSpec((1, SUBLANE, width), seq3),
            scratch_shapes=[pltpu.VMEM((nq, 1), F32), pltpu.VMEM((nq, width), F32),
                            pltpu.VMEM((2, pps, heads, PAGE, d), F32), pltpu.VMEM((2, pps, heads, PAGE, d), F32),
                            pltpu.SemaphoreType.DMA((2, 2))]),
        out_shape=jax.ShapeDtypeStruct((nseq, SUBLANE, width), F32),
        compiler_params=_params("arbitrary", "arbitrary"),
        name="sb_decode",
    )(page_table, qbd, bias_col, knew, vnew, cache_k, cache_v)
    return out[:, :n_new].reshape(nseq * n_new, width).astype(BF16)


def _ssd_kernel(x_ref, b_ref, c_ref, px_ref, pb_ref, pc_ref, wx_ref, wb_ref, wc_ref,
                bx_ref, bb_ref, bc_ref, dt_ref, dtb_ref, alog_ref, dskip_ref, z_ref, nw_ref, h0_ref,
                y_ref, hout_ref, h_scr, ex_scr, eb_scr, ec_scr, *, valid_len, n_chunks, rep, hd, gp):
    c = pl.program_id(2)
    cl = x_ref.shape[0]
    xw = rep * hd
    n = h_scr.shape[1]

    @pl.when(c == 0)
    def _():
        h_scr[...] = h0_ref[0].reshape(gp * xw, n)
        ex_scr[0:SUBLANE, :] = px_ref[0]
        eb_scr[0:SUBLANE, :] = pb_ref[0]
        ec_scr[0:SUBLANE, :] = pc_ref[0]

    def conv_silu(e_scr, raw_ref, w_ref, bias_ref):
        e_scr[SUBLANE:SUBLANE + cl, :] = raw_ref[...]
        acc = bias_ref[...]
        for i in range(CONV_TAPS):
            lo = SUBLANE - (CONV_TAPS - 1) + i
            acc = acc + e_scr[lo:lo + cl, :] * w_ref[i:i + 1, :]
        e_scr[0:SUBLANE, :] = e_scr[cl:cl + SUBLANE, :]
        return _silu(acc)

    xc_all = conv_silu(ex_scr, x_ref, wx_ref, bx_ref)
    bm_all = conv_silu(eb_scr, b_ref, wb_ref, bb_ref)
    cm_all = conv_silu(ec_scr, c_ref, wc_ref, bc_ref)

    row = c * cl + _iota((cl, 1), 0)
    t_idx = _iota((cl, cl), 0)
    s_idx = _iota((cl, cl), 1)
    tril = t_idx >= s_idx
    tril16 = tril.astype(BF16)
    lane = _iota((1, xw), 1)
    in_head = [(lane >= r * hd) & (lane < (r + 1) * hd) for r in range(rep)]
    groups = range(gp)

    xc = [xc_all[:, s * xw:(s + 1) * xw] for s in groups]
    bm16 = [bm_all[:, s * n:(s + 1) * n].astype(BF16) for s in groups]
    cm16 = [cm_all[:, s * n:(s + 1) * n].astype(BF16) for s in groups]
    h = [h_scr[s * xw:(s + 1) * xw, :] for s in groups]

    def step_sizes(s):
        dtr = dt_ref[0, s] + dtb_ref[s]
        return jnp.where(row < valid_len, jnp.maximum(dtr, 0.0) + _softplus_neg_abs(dtr), 0.0)

    dt = [step_sizes(s) for s in groups]
    cum = [_split_dot_left(tril16, dt[s] * -jnp.exp(alog_ref[s]), 3) for s in groups]
    g = [lax.dot_general(cm16[s], bm16[s], NT_DIMS, preferred_element_type=F32) for s in groups]
    inter = [lax.dot_general(cm16[s], h[s].astype(BF16), NT_DIMS, preferred_element_type=F32) for s in groups]
    cum_t = [cum[s].T for s in groups]
    dt_t = [dt[s].T for s in groups]

    def head_terms(s):
        cum_last = cum[s][cl - 1:cl, :]
        decay_last = jnp.exp(cum_last)
        ws, xrs, hdecay = [], [], []
        e_all = jnp.zeros((cl, xw), F32)
        tail_all = jnp.zeros((cl, xw), F32)
        for r in range(rep):
            ccol = cum[s][:, r:r + 1]
            decay = jnp.exp(jnp.where(tril, ccol - cum_t[s][r:r + 1, :], -jnp.inf))
            ws.append((g[s] * decay * dt_t[s][r:r + 1, :]).astype(BF16))
            xrs.append(jnp.where(in_head[r], xc[s], 0.0).astype(BF16))
            e_all = jnp.where(in_head[r], jnp.exp(ccol), e_all)
            tail_all = jnp.where(in_head[r], jnp.exp(cum_last[:, r:r + 1] - ccol) * dt[s][:, r:r + 1], tail_all)
            hdecay.append(jnp.broadcast_to(decay_last[:, r:r + 1], (hd, n)))
        return (jnp.concatenate(ws, axis=1), jnp.concatenate(xrs, axis=0), e_all, tail_all,
                jnp.concatenate(hdecay, axis=0))

    terms = [head_terms(s) for s in groups]
    intra = [jnp.dot(terms[s][0], terms[s][1], preferred_element_type=F32) for s in groups]
    y = [intra[s] + inter[s] * terms[s][2] + xc[s] * dskip_ref[s] for s in groups]

    def transposed(xt):
        return jnp.concatenate([xt[:, j * LANE:(j + 1) * LANE].T for j in range(xw // LANE)], axis=0)

    xt_t = [transposed(xc[s] * terms[s][3]).astype(BF16) for s in groups]
    h_new = [h[s] * terms[s][4] + jnp.dot(xt_t[s], bm16[s], preferred_element_type=F32) for s in groups]
    for s in groups:
        h_scr[s * xw:(s + 1) * xw, :] = h_new[s]

    for s in groups:
        lanes = slice(s * xw, (s + 1) * xw)
        gated = y[s] * _silu(z_ref[:, lanes])
        ms = jnp.mean(gated * gated, axis=-1, keepdims=True)
        y_ref[:, lanes] = (gated * lax.rsqrt(ms + RMS_EPS) * nw_ref[:, lanes]).astype(y_ref.dtype)

    @pl.when(c == n_chunks - 1)
    def _():
        hout_ref[0] = h_scr[...].reshape(hout_ref.shape[1:])


def _ssd(xbc, dt_raw, z, conv_prev, h0, valid_len, conv_w, conv_b, dt_bias, a_log, d_skip, norm_w, groups):
    nseq, heads, hd, n = h0.shape
    rows, ch = xbc.shape
    seq = rows // nseq
    rep = heads // groups
    xw = rep * hd
    ssm_w = heads * hd
    cl = CHUNK
    nc = seq // cl
    gp = SSD_GROUPS_PER_STEP
    assert seq % cl == 0 and n == LANE and xw % LANE == 0 and ch == ssm_w + 2 * groups * n
    assert groups % gp == 0 and ssm_w % (gp * n) == 0
    b_off = ssm_w // (gp * n)
    c_off = b_off + groups // gp

    prev = jnp.pad(conv_prev, ((0, 0), (SUBLANE - (CONV_TAPS - 1), 0), (0, 0)))
    dt_g = dt_raw.reshape(nseq, seq, groups, rep).transpose(0, 2, 1, 3)
    dt_g = jnp.pad(dt_g, ((0, 0), (0, 0), (0, 0), (0, LANE - rep)))

    def lane_pad(p):
        return jnp.pad(p.astype(F32).reshape(groups, 1, rep), ((0, 0), (0, 0), (0, LANE - rep)))

    dskip_rep = jnp.repeat(d_skip.astype(F32), hd).reshape(groups, 1, xw)
    cb = conv_b.reshape(1, ch)

    def rows_at(width, off):
        return pl.BlockSpec((cl, width), lambda b, g, c: (b * nc + c, off + g))

    def prev_at(width, off):
        return pl.BlockSpec((1, SUBLANE, width), lambda b, g, c: (b, 0, off + g))

    def taps_at(nrows, width, off):
        return pl.BlockSpec((nrows, width), lambda b, g, c: (0, off + g))

    par = pl.BlockSpec((gp, 1, LANE), lambda b, g, c: (g, 0, 0))
    state = pl.BlockSpec((1, gp * rep, hd, n), lambda b, g, c: (b, g, 0, 0))
    xs_w, bc_w = gp * xw, gp * n
    y, h = pl.pallas_call(
        functools.partial(_ssd_kernel, valid_len=valid_len, n_chunks=nc, rep=rep, hd=hd, gp=gp),
        grid=(nseq, groups // gp, nc),
        in_specs=[rows_at(xs_w, 0), rows_at(bc_w, b_off), rows_at(bc_w, c_off),
                  prev_at(xs_w, 0), prev_at(bc_w, b_off), prev_at(bc_w, c_off),
                  taps_at(CONV_TAPS, xs_w, 0), taps_at(CONV_TAPS, bc_w, b_off), taps_at(CONV_TAPS, bc_w, c_off),
                  taps_at(1, xs_w, 0), taps_at(1, bc_w, b_off), taps_at(1, bc_w, c_off),
                  pl.BlockSpec((1, gp, cl, LANE), lambda b, g, c: (b, g, c, 0)),
                  par, par,
                  pl.BlockSpec((gp, 1, xw), lambda b, g, c: (g, 0, 0)),
                  rows_at(xs_w, 0),
                  taps_at(1, xs_w, 0),
                  state],
        out_specs=[rows_at(xs_w, 0), state],
        out_shape=[jax.ShapeDtypeStruct((rows, ssm_w), BF16), jax.ShapeDtypeStruct(h0.shape, F32)],
        scratch_shapes=[pltpu.VMEM((gp * xw, n), F32),
                        pltpu.VMEM((cl + SUBLANE, xs_w), F32),
                        pltpu.VMEM((cl + SUBLANE, bc_w), F32),
                        pltpu.VMEM((cl + SUBLANE, bc_w), F32)],
        compiler_params=_params("arbitrary", "arbitrary", "arbitrary"),
        name="ssd",
    )(xbc, xbc, xbc, prev, prev, prev, conv_w, conv_w, conv_w, cb, cb, cb,
      dt_g, lane_pad(dt_bias), lane_pad(a_log), dskip_rep, z, norm_w.reshape(1, ssm_w), h0)
    return y, h


def _pool_kernel(u_ref, prev_ref, w_ref, s_ref, o_ref, ext_scr, *, pos0, gd):
    i = pl.program_id(1)
    tr = u_ref.shape[0]

    @pl.when(i == 0)
    def _():
        ext_scr[0:POOL_HIST, :] = prev_ref[0]

    ext_scr[POOL_HIST:POOL_HIST + tr, :] = u_ref[...]
    pos = (pos0 + i * tr + _iota((tr, 1), 0)).astype(F32)
    for g, win in enumerate(POOL_WINDOWS):
        lanes = slice(g * gd, (g + 1) * gd)
        total = ext_scr[POOL_HIST:POOL_HIST + tr, lanes]
        for j in range(1, win):
            total = total + ext_scr[POOL_HIST - j:POOL_HIST - j + tr, lanes]
        count = jnp.minimum(pos + 1.0, float(win))
        dlt = total / count - u_ref[:, lanes]
        y = jnp.dot(dlt.astype(BF16), w_ref[g], preferred_element_type=F32)
        o_ref[:, lanes] = (y * s_ref[:, lanes]).astype(o_ref.dtype)
    ext_scr[0:POOL_HIST, :] = ext_scr[tr:tr + POOL_HIST, :]


def _pool(u, prev, pos0, w_pool16, pool_scale, tr):
    nseq = prev.shape[0]
    rows, width = u.shape
    seq = rows // nseq
    tr = min(tr, seq)
    nt = seq // tr
    ng, gd, _ = w_pool16.shape
    assert seq % tr == 0 and ng == len(POOL_WINDOWS) and ng * gd == width
    prev16 = jnp.pad(prev, ((0, 0), (POOL_HIST - prev.shape[1], 0), (0, 0)))
    return pl.pallas_call(
        functools.partial(_pool_kernel, pos0=pos0, gd=gd),
        grid=(nseq, nt),
        in_specs=[pl.BlockSpec((tr, width), lambda b, i: (b * nt + i, 0)),
                  pl.BlockSpec((1, POOL_HIST, width), lambda b, i: (b, 0, 0)),
                  pl.BlockSpec((ng, gd, gd), lambda b, i: (0, 0, 0)),
                  pl.BlockSpec((1, width), lambda b, i: (0, 0))],
        out_specs=pl.BlockSpec((tr, width), lambda b, i: (b * nt + i, 0)),
        out_shape=jax.ShapeDtypeStruct((rows, width), BF16),
        scratch_shapes=[pltpu.VMEM((tr + POOL_HIST, width), F32)],
        compiler_params=_params("arbitrary", "arbitrary"),
        name="pool",
    )(u, prev16, w_pool16, pool_scale.reshape(1, width))


def _gla_kernel(q_ref, k_ref, v_ref, r_ref, glr_ref, w2_ref, b2_ref, nw_ref, s0_ref,
                o_ref, sout_ref, s_scr, *, valid_len, n_chunks, scale, hp):
    c = pl.program_id(2)
    cl = q_ref.shape[0]
    dk = s_scr.shape[1]
    dv = s_scr.shape[2]
    heads = range(hp)

    @pl.when(c == 0)
    def _():
        s_scr[...] = s0_ref[0]

    row = c * cl + _iota((cl, 1), 0)
    valid = row < valid_len
    pre = jnp.dot(glr_ref[...], w2_ref[...], preferred_element_type=F32) + b2_ref[...]
    lg_all = (jnp.minimum(pre, 0.0) - _softplus_neg_abs(pre)) / GLA_TAU
    lg_all = jnp.where(valid, lg_all, 0.0)
    qs_all = q_ref[...] * scale
    k_all = jnp.where(valid, k_ref[...], 0.0)
    kl = [slice(h * dk, (h + 1) * dk) for h in heads]
    vl = [slice(h * dv, (h + 1) * dv) for h in heads]
    lg = [lg_all[:, kl[h]] for h in heads]
    qs = [qs_all[:, kl[h]] for h in heads]
    k = [k_all[:, kl[h]] for h in heads]
    v16 = [v_ref[:, vl[h]].astype(BF16) for h in heads]

    t_idx = _iota((cl, cl), 0)
    s_idx = _iota((cl, cl), 1)
    r_idx = _iota((cl, 1), 0)
    def masked_qk(qd, kd, mask):
        return jnp.where(mask, lax.dot_general(qd.astype(BF16), kd.astype(BF16), NT_DIMS,
                                               preferred_element_type=F32), 0.0)

    att = [masked_qk(qs[h], k[h], t_idx == s_idx) for h in heads]
    pre_b = lg
    suf_b = [jnp.zeros_like(lg[h]) for h in heads]
    tot_b = lg
    bsz = 1
    while bsz < cl:
        siblings = ((_div(t_idx, 2 * bsz) == _div(s_idx, 2 * bsz))
                    & (_mod(_div(t_idx, bsz), 2) == 1) & (_mod(_div(s_idx, bsz), 2) == 0))
        att = [att[h] + masked_qk(qs[h] * jnp.exp(pre_b[h]), k[h] * jnp.exp(suf_b[h]), siblings) for h in heads]
        is_right = _mod(_div(r_idx, bsz), 2) == 1
        left_tot = [pltpu.roll(tot_b[h], bsz, axis=0) for h in heads]
        right_tot = [pltpu.roll(tot_b[h], cl - bsz, axis=0) for h in heads]
        pre_b = [pre_b[h] + jnp.where(is_right, left_tot[h], 0.0) for h in heads]
        suf_b = [suf_b[h] + jnp.where(is_right, 0.0, right_tot[h]) for h in heads]
        tot_b = [tot_b[h] + jnp.where(is_right, left_tot[h], right_tot[h]) for h in heads]
        bsz *= 2

    s = [s_scr[h] for h in heads]
    qd = [(qs[h] * jnp.exp(pre_b[h])).astype(BF16) for h in heads]
    kd_t = [(k[h] * jnp.exp(suf_b[h])).T.astype(BF16) for h in heads]
    o = [jnp.dot(att[h].astype(BF16), v16[h], preferred_element_type=F32)
         + jnp.dot(qd[h], s[h].astype(BF16), preferred_element_type=F32) for h in heads]
    chunk_decay = [jnp.exp(tot_b[h].T) for h in heads]
    s_new = [s[h] * jnp.concatenate([chunk_decay[h]] * (dv // cl), axis=1)
             + jnp.dot(kd_t[h], v16[h], preferred_element_type=F32) for h in heads]
    for h in heads:
        s_scr[h] = s_new[h]

    for h in heads:
        ms = jnp.mean(o[h] * o[h], axis=-1, keepdims=True)
        o_ref[:, vl[h]] = (o[h] * lax.rsqrt(ms + RMS_EPS) * nw_ref[:, vl[h]]
                           * _silu(r_ref[:, vl[h]])).astype(o_ref.dtype)

    @pl.when(c == n_chunks - 1)
    def _():
        sout_ref[0] = s_scr[...]


def _gla(q, k, v, r, glr, w2, b2, norm_w, s0, valid_len):
    nseq, heads, dk, dv = s0.shape
    rows = q.shape[0]
    seq = rows // nseq
    cl = CHUNK
    nc = seq // cl
    hp = GLA_HEADS_PER_STEP
    assert seq % cl == 0 and dk == cl and dv % cl == 0 and glr.shape[1] == LANE and heads % hp == 0

    def rows_at(width, col):
        return pl.BlockSpec((cl, width), (lambda b, h, c: (b * nc + c, h)) if col else (lambda b, h, c: (b * nc + c, 0)))

    def head_at(nrows, width):
        return pl.BlockSpec((nrows, width), lambda b, h, c: (0, h))

    state = pl.BlockSpec((1, hp, dk, dv), lambda b, h, c: (b, h, 0, 0))
    kw, vw = hp * dk, hp * dv
    return pl.pallas_call(
        functools.partial(_gla_kernel, valid_len=valid_len, n_chunks=nc, scale=dk ** -0.5, hp=hp),
        grid=(nseq, heads // hp, nc),
        in_specs=[rows_at(kw, True), rows_at(kw, True), rows_at(vw, True), rows_at(vw, True),
                  rows_at(LANE, False), head_at(LANE, kw), head_at(1, kw), head_at(1, vw), state],
        out_specs=[rows_at(vw, True), state],
        out_shape=[jax.ShapeDtypeStruct((rows, heads * dv), BF16), jax.ShapeDtypeStruct(s0.shape, F32)],
        scratch_shapes=[pltpu.VMEM((hp, dk, dv), F32)],
        compiler_params=_params("arbitrary", "arbitrary", "arbitrary"),
        name="gla",
    )(q, k, v, r, glr, w2, b2.reshape(1, heads * dk), norm_w.reshape(1, heads * dv), s0)


def _pad_seq(a, nseq, seq_pad):
    n = a.shape[0] // nseq
    a3 = jnp.pad(a.reshape(nseq, n, a.shape[1]), ((0, 0), (0, seq_pad - n), (0, 0)))
    return a3.reshape(nseq * seq_pad, a.shape[1])


def _unpad_seq(a, nseq, n):
    return a.reshape(nseq, -1, a.shape[1])[:, :n].reshape(nseq * n, a.shape[1])


def _project(xp16, xs16, w, sizes, name):
    outs_p, outs_s = [], []
    off = 0
    w_nk = w.T
    for idx, size in enumerate(sizes):
        op, os_ = _proj([xp16], [xs16], w_nk, True, off, size, F32, PROJ_TM, PROJ_TN, f"{name}_{idx}")
        off += size
        outs_p.append(op)
        outs_s.append(os_)
    return outs_p, outs_s


def kernel(x_prompt, x_sample, cache_k, cache_v, page_table, state_conv, state_ssm, state_pool, state_gla, w_in0, sb_bias, conv_w, conv_b, dt_bias, a_log, d_skip, ssm_norm_w, w_out0, w_in1, w_pool, pool_scale, gla_w2, gla_b2, gla_norm_w, w_out1, ln_g, ln_b, w_gu, w_down):
    bp, seq, dm = x_prompt.shape
    bs, n_new, _ = x_sample.shape
    depth = ln_g.shape[0]
    alpha = (2 * depth) ** 0.25
    sb_heads, sb_d = cache_k.shape[2], cache_k.shape[3]
    sb_w = sb_heads * sb_d
    ssm_heads, ssm_hd, ssm_n = state_ssm.shape[1:]
    ssm_w = ssm_heads * ssm_hd
    conv_ch = state_conv.shape[2]
    groups = (conv_ch - ssm_w) // (2 * ssm_n)
    pool_w = state_pool.shape[2]
    gla_heads, gla_dk, gla_dv = state_gla.shape[1:]
    gla_kw, gla_vw = gla_heads * gla_dk, gla_heads * gla_dv
    gla_rank = gla_w2.shape[0]
    past_len = page_table.shape[1] * cache_k.shape[1]
    seq_pad = CHUNK

    xp = x_prompt.reshape(bp * seq, dm)
    xs = x_sample.reshape(bs * n_new, dm)
    xp16, xs16 = xp.astype(BF16), xs.astype(BF16)

    def mix_ffn_ln(xp, xs, mixed_p, mixed_s, w_out, layer):
        yp, ys = _proj(mixed_p, mixed_s, w_out, False, 0, dm, F32, PROJ_TM, PROJ_TN, f"out{layer}",
                       res=(xp, xs), alpha=alpha)
        xp, xp16 = _ln(yp, ln_g[layer, 0], ln_b[layer, 0], LN_TM)
        xs, xs16 = _ln(ys, ln_g[layer, 0], ln_b[layer, 0], LN_TM)
        hp, hs, w_down16 = _gate_up(xp16, xs16, w_gu, w_down, layer, PROJ_TM, GU_TN)
        yp = _mm_res(hp, w_down16, xp, alpha, DOWN_TM, DOWN_TN)
        ys = _mm_res(hs, w_down16, xs, alpha, DOWN_TM, DOWN_TN)
        xp, xp16 = _ln(yp, ln_g[layer, 1], ln_b[layer, 1], LN_TM)
        xs, xs16 = _ln(ys, ln_g[layer, 1], ln_b[layer, 1], LN_TM)
        return xp, xp16, xs, xs16

    split0 = (sb_w, sb_w, sb_w, ssm_w, conv_ch, ssm_heads)
    (qp, kp, vp, zp, xbcp, dtp), (qs, ks, vs, zs, xbcs, dts) = _project(xp16, xs16, w_in0, split0, "in0")
    dtp, dts = dtp[:, :ssm_heads], dts[:, :ssm_heads]

    oap = _sb_prompt(qp, kp, vp, sb_bias, bp, SB_T, SB_HEADS_PER_STEP)
    oas = _sb_decode(qs, ks, vs, cache_k, cache_v, page_table, sb_bias)

    ssd_w = (conv_w, conv_b, dt_bias, a_log, d_skip, ssm_norm_w, groups)
    yp, ssm_p = _ssd(xbcp, dtp, zp, jnp.zeros((bp, CONV_TAPS - 1, conv_ch), F32),
                     jnp.zeros((bp,) + state_ssm.shape[1:], F32), seq, *ssd_w)
    ys, ssm_s = _ssd(_pad_seq(xbcs, bs, seq_pad), _pad_seq(dts, bs, seq_pad), _pad_seq(zs, bs, seq_pad),
                     state_conv, state_ssm, n_new, *ssd_w)
    ys = _unpad_seq(ys, bs, n_new)

    xp, xp16, xs, xs16 = mix_ffn_ln(xp, xs, [oap, yp], [oas, ys], w_out0, 0)

    split1 = (pool_w, gla_kw, gla_kw, gla_vw, gla_vw, gla_rank)
    (up, gqp, gkp, gvp, grp, glrp), (us, gqs, gks, gvs, grs, glrs) = _project(xp16, xs16, w_in1, split1, "in1")

    w_pool16 = w_pool.astype(BF16)
    ocp = _pool(up, jnp.zeros((bp, POOL_HIST - 1, pool_w), F32), 0, w_pool16, pool_scale, POOL_TM)
    us8 = _pad_seq(us, bs, SUBLANE)
    ocs = _unpad_seq(_pool(us8, state_pool, past_len, w_pool16, pool_scale, SUBLANE), bs, n_new)

    w2_16 = jnp.pad(gla_w2, ((0, LANE - gla_rank), (0, 0))).astype(BF16)
    gla_w = (w2_16, gla_b2, gla_norm_w)
    odp, gla_p = _gla(gqp, gkp, gvp, grp, glrp.astype(BF16), *gla_w,
                      jnp.zeros((bp,) + state_gla.shape[1:], F32), seq)
    pad = lambda a: _pad_seq(a, bs, seq_pad)
    ods, gla_s = _gla(pad(gqs), pad(gks), pad(gvs), pad(grs), pad(glrs).astype(BF16), *gla_w, state_gla, n_new)
    ods = _unpad_seq(ods, bs, n_new)

    xp, xp16, xs, xs16 = mix_ffn_ln(xp, xs, [ocp, odp], [ocs, ods], w_out1, 1)

    hd4 = lambda a, nseq: a.reshape(nseq, -1, sb_heads, sb_d)
    conv_p = xbcp.reshape(bp, seq, conv_ch)[:, seq - (CONV_TAPS - 1):]
    conv_s = jnp.concatenate([state_conv, xbcs.reshape(bs, n_new, conv_ch)], axis=1)[:, n_new:]
    pool_p = up.reshape(bp, seq, pool_w)[:, seq - (POOL_HIST - 1):]
    pool_s = jnp.concatenate([state_pool, us.reshape(bs, n_new, pool_w)], axis=1)[:, n_new:]
    return (xp.reshape(bp, seq, dm), xs.reshape(bs, n_new, dm),
            hd4(kp, bp), hd4(vp, bp), hd4(ks, bs), hd4(vs, bs),
            conv_p, conv_s, ssm_p, ssm_s, pool_p, pool_s, gla_p, gla_s)
```

```python
import functools

import jax
import jax.numpy as jnp
from jax import lax
from jax.experimental import pallas as pl
from jax.experimental.pallas import tpu as pltpu

F32 = jnp.float32
BF16 = jnp.bfloat16

LANE = 128
SUBLANE = 8
VMEM_LIMIT = 56 * 1024 * 1024

LN_EPS = 1e-5
RMS_EPS = 1e-5
GLA_TAU = 16.0
PAGE = 128
POOL_WINDOWS = (2, 4, 8, 16)
POOL_HIST = 16
CONV_TAPS = 4
CHUNK = 128

PROJ_TM, PROJ_TN = 1024, 512
GU_TN = 256
DOWN_TM, DOWN_TN = 512, 512
LN_TM = 256
SB_T, SB_HEADS_PER_STEP = 256, 8
POOL_TM = 256
DECODE_PAGES_PER_STEP = 8
SSD_GROUPS_PER_STEP = 4
GLA_HEADS_PER_STEP = 8

NT_DIMS = (((1,), (1,)), ((), ()))


def _params(*sem):
    return pltpu.CompilerParams(dimension_semantics=sem, vmem_limit_bytes=VMEM_LIMIT)


def _silu(x):
    return x * (0.5 * jnp.tanh(0.5 * x) + 0.5)


def _softplus_neg_abs(x):
    return jnp.log1p(jnp.exp(-jnp.abs(x)))


def _iota(shape, dim):
    return lax.broadcasted_iota(jnp.int32, shape, dim)


def _div(x, c):
    return lax.shift_right_logical(x, (c.bit_length() - 1)) if c & (c - 1) == 0 else x // c


def _mod(x, c):
    return (x & (c - 1)) if c & (c - 1) == 0 else x % c


def _bf16_pieces(x, terms):
    pieces = [x.astype(BF16)]
    for _ in range(terms - 1):
        x = x - pieces[-1].astype(F32)
        pieces.append(x.astype(BF16))
    return pieces


def _add_all(terms):
    return functools.reduce(lambda a, b: a + b, terms)


def _split_dot_left(m01, x, terms):
    return _add_all([jnp.dot(m01, p, preferred_element_type=F32) for p in _bf16_pieces(x, terms)])


def _residual_tile(refs):
    if len(refs) == 1:
        return refs[0][...]
    y_ref, mu_ref, rstd_ref, g_ref, b_ref = refs
    return (y_ref[...] - mu_ref[...]) * rstd_ref[...] * g_ref[...] + b_ref[...]


def _residual_specs(res, tm, tn, row_map, col_map):
    tile = pl.BlockSpec((tm, tn), lambda *ids: (row_map(*ids), col_map(*ids)))
    if len(res) == 1:
        return [tile]
    per_row = pl.BlockSpec((tm, 1), lambda *ids: (row_map(*ids), 0))
    per_col = pl.BlockSpec((1, tn), lambda *ids: (0, col_map(*ids)))
    return [tile, per_row, per_row, per_col, per_col]


def _mm_res_kernel(x_ref, w_ref, *refs, alpha):
    o_ref = refs[-1]
    o_ref[...] = alpha * _residual_tile(refs[:-1]) + jnp.dot(x_ref[...], w_ref[...], preferred_element_type=F32)


def _mm_res(x, w, res, alpha, tm, tn):
    m, k = x.shape
    n = w.shape[1]
    tm = min(tm, m)
    tn = min(tn, n)
    assert m % tm == 0 and n % tn == 0
    return pl.pallas_call(
        functools.partial(_mm_res_kernel, alpha=alpha),
        grid=(m // tm, n // tn),
        in_specs=[pl.BlockSpec((tm, k), lambda i, j: (i, 0)),
                  pl.BlockSpec((k, tn), lambda i, j: (0, j))]
                 + _residual_specs(res, tm, tn, lambda i, j: i, lambda i, j: j),
        out_specs=pl.BlockSpec((tm, tn), lambda i, j: (i, j)),
        out_shape=jax.ShapeDtypeStruct((m, n), F32),
        compiler_params=_params("arbitrary", "arbitrary"),
        name="down",
    )(x, w, *res)


def _dot_parts(x_refs, w16, w_is_nk):
    terms, off = [], 0
    for x_ref in x_refs:
        k = x_ref.shape[1]
        if w_is_nk:
            terms.append(lax.dot_general(x_ref[...], w16[:, off:off + k], NT_DIMS, preferred_element_type=F32))
        else:
            terms.append(jnp.dot(x_ref[...], w16[off:off + k, :], preferred_element_type=F32))
        off += k
    return _add_all(terms)


def _proj_kernel(*refs, n_parts, w_is_nk, col0, ncols, nb, alpha):
    xps, xss = refs[:n_parts], refs[n_parts:2 * n_parts]
    w_ref = refs[2 * n_parts]
    res = refs[2 * n_parts + 1:-5]
    res = (res[:len(res) // 2], res[len(res) // 2:])
    op_ref, os_ref, wstage, sem, w16 = refs[-5:]

    def finish(acc, res_refs, o_ref):
        if alpha is not None:
            acc = alpha * _residual_tile(res_refs) + acc
        o_ref[...] = acc.astype(o_ref.dtype)

    n = pl.program_id(0)
    m = pl.program_id(1)
    tn = op_ref.shape[1]
    nv = min(tn, ncols)

    def tile_copy(t, slot):
        cols = pl.ds(pl.multiple_of(col0 + t * tn, nv), nv)
        if w_is_nk:
            return pltpu.make_async_copy(w_ref.at[cols, :], wstage.at[slot, pl.ds(0, nv), :], sem.at[slot])
        return pltpu.make_async_copy(w_ref.at[:, cols], wstage.at[slot, :, pl.ds(0, nv)], sem.at[slot])

    @pl.when(m == 0)
    def _():
        slot = n % 2

        @pl.when(n == 0)
        def _():
            tile_copy(0, 0).start()

        tile_copy(n, slot).wait()
        if nv == tn:
            w16[...] = wstage[slot].astype(BF16)
        elif w_is_nk:
            w16[0:nv, :] = wstage[slot, 0:nv, :].astype(BF16)
            w16[nv:tn, :] = jnp.zeros((tn - nv, w16.shape[1]), BF16)
        else:
            w16[:, 0:nv] = wstage[slot, :, 0:nv].astype(BF16)
            w16[:, nv:tn] = jnp.zeros((w16.shape[0], tn - nv), BF16)

        @pl.when(n + 1 < nb)
        def _():
            tile_copy(n + 1, 1 - slot).start()

        finish(_dot_parts(xss, w16, w_is_nk), res[1], os_ref)

    finish(_dot_parts(xps, w16, w_is_nk), res[0], op_ref)


def _col_tile(col0, ncols, tn_max):
    if ncols < LANE:
        assert col0 % LANE == 0
        return LANE
    tn = tn_max
    while ncols % tn or col0 % tn:
        tn //= 2
    assert tn >= LANE
    return tn


def _proj(xps, xss, w, w_is_nk, col0, ncols, out_dtype, tm, tn_max, name, res=None, alpha=None):
    m = xps[0].shape[0]
    s = xss[0].shape[0]
    k = w.shape[1] if w_is_nk else w.shape[0]
    tm = min(tm, m)
    tn = _col_tile(col0, ncols, tn_max)
    nb = pl.cdiv(ncols, tn)
    assert m % tm == 0 and sum(x.shape[1] for x in xps) == k and min(tn, ncols) % SUBLANE == 0
    tile_shape = (tn, k) if w_is_nk else (k, tn)
    out_tiles = [pl.BlockSpec((tm, tn), lambda n, i: (i, n)), pl.BlockSpec((s, tn), lambda n, i: (0, n))]
    assert (res is None) == (alpha is None)
    res_arrays, res_specs = [], []
    if res is not None:
        res_arrays = [*res[0], *res[1]]
        res_specs = (_residual_specs(res[0], tm, tn, lambda n, i: i, lambda n, i: n)
                     + _residual_specs(res[1], s, tn, lambda n, i: 0, lambda n, i: n))
    return pl.pallas_call(
        functools.partial(_proj_kernel, n_parts=len(xps), w_is_nk=w_is_nk, col0=col0, ncols=ncols, nb=nb,
                          alpha=alpha),
        grid=(nb, m // tm),
        in_specs=([pl.BlockSpec((tm, x.shape[1]), lambda n, i: (i, 0)) for x in xps]
                  + [pl.BlockSpec((s, x.shape[1]), lambda n, i: (0, 0)) for x in xss]
                  + [pl.BlockSpec(memory_space=pl.ANY)]
                  + res_specs),
        out_specs=out_tiles,
        out_shape=[jax.ShapeDtypeStruct((m, nb * tn), out_dtype), jax.ShapeDtypeStruct((s, nb * tn), out_dtype)],
        scratch_shapes=[pltpu.VMEM((2,) + tile_shape, F32), pltpu.SemaphoreType.DMA((2,)),
                        pltpu.VMEM(tile_shape, BF16)],
        compiler_params=_params("arbitrary", "arbitrary"),
        name=name,
    )(*xps, *xss, w, *res_arrays)


def _gu_kernel(xp_ref, xs_ref, wgu_ref, wd_ref, op_ref, os_ref, wd16_ref, wstage, sem, wg16, wu16,
               *, layer, nb):
    n = pl.program_id(0)
    m = pl.program_id(1)
    tn = wg16.shape[1]

    def tile_copies(t, slot):
        cols = [pl.ds(pl.multiple_of((t + half * nb) * tn, tn), tn) for half in range(2)]
        return [pltpu.make_async_copy(wgu_ref.at[layer, :, cols[half]], wstage.at[slot, half], sem.at[slot, half])
                for half in range(2)]

    def act(x_ref, o_ref):
        x = x_ref[...]
        g = jnp.dot(x, wg16[...], preferred_element_type=F32)
        u = jnp.dot(x, wu16[...], preferred_element_type=F32)
        o_ref[...] = (_silu(g) * u).astype(o_ref.dtype)

    @pl.when(m == 0)
    def _():
        slot = n % 2

        @pl.when(n == 0)
        def _():
            for c in tile_copies(0, 0):
                c.start()

        for c in tile_copies(n, slot):
            c.wait()
        wg16[...] = wstage[slot, 0].astype(BF16)
        wu16[...] = wstage[slot, 1].astype(BF16)

        @pl.when(n + 1 < nb)
        def _():
            for c in tile_copies(n + 1, 1 - slot):
                c.start()

        act(xs_ref, os_ref)

    wd16_ref[...] = wd_ref[...].astype(BF16)
    act(xp_ref, op_ref)


def _gate_up(xp, xs, w_gu, w_down, layer, tm, tn):
    m, k = xp.shape
    s = xs.shape[0]
    f = w_gu.shape[2] // 2
    dm = w_down.shape[2]
    tm = min(tm, m)
    tn = min(tn, f)
    assert m % tm == 0 and f % tn == 0 and w_down.shape[1] == f
    nb = f // tn
    mt = m // tm
    td = f // (nb * mt)
    assert td * nb * mt == f and td % (2 * SUBLANE) == 0
    return pl.pallas_call(
        functools.partial(_gu_kernel, layer=layer, nb=nb),
        grid=(nb, mt),
        in_specs=[pl.BlockSpec((tm, k), lambda n, i: (i, 0)),
                  pl.BlockSpec((s, k), lambda n, i: (0, 0)),
                  pl.BlockSpec(memory_space=pl.ANY),
                  pl.BlockSpec((None, td, dm), lambda n, i: (layer, n * mt + i, 0))],
        out_specs=[pl.BlockSpec((tm, tn), lambda n, i: (i, n)),
                   pl.BlockSpec((s, tn), lambda n, i: (0, n)),
                   pl.BlockSpec((td, dm), lambda n, i: (n * mt + i, 0))],
        out_shape=[jax.ShapeDtypeStruct((m, f), BF16), jax.ShapeDtypeStruct((s, f), BF16),
                   jax.ShapeDtypeStruct((f, dm), BF16)],
        scratch_shapes=[pltpu.VMEM((2, 2, k, tn), F32), pltpu.SemaphoreType.DMA((2, 2)),
                        pltpu.VMEM((k, tn), BF16), pltpu.VMEM((k, tn), BF16)],
        compiler_params=_params("arbitrary", "arbitrary"),
        name="gate_up",
    )(xp, xs, w_gu, w_down)


def _ln_kernel(y_ref, g_ref, b_ref, *o_refs, final):
    y = y_ref[...]
    mu = jnp.mean(y, axis=-1, keepdims=True)
    yc = y - mu
    var = jnp.mean(yc * yc, axis=-1, keepdims=True)
    rstd = lax.rsqrt(var + LN_EPS)
    out = yc * rstd * g_ref[...] + b_ref[...]
    if final:
        o_refs[0][...] = out
    else:
        ob_ref, mu_ref, rstd_ref = o_refs
        ob_ref[...] = out.astype(BF16)
        mu_ref[...] = mu
        rstd_ref[...] = rstd


def _ln(y, g, b, tm, final=False):
    rows, d = y.shape
    tm = min(tm, rows)
    assert rows % tm == 0
    row = pl.BlockSpec((tm, d), lambda i: (i, 0))
    vec = pl.BlockSpec((1, d), lambda i: (0, 0))
    stat = pl.BlockSpec((tm, 1), lambda i: (i, 0))
    if final:
        out_specs, out_shape = row, jax.ShapeDtypeStruct((rows, d), F32)
    else:
        out_specs = [row, stat, stat]
        out_shape = [jax.ShapeDtypeStruct((rows, d), BF16), jax.ShapeDtypeStruct((rows, 1), F32),
                     jax.ShapeDtypeStruct((rows, 1), F32)]
    return pl.pallas_call(
        functools.partial(_ln_kernel, final=final),
        grid=(rows // tm,),
        in_specs=[row, vec, vec],
        out_specs=out_specs,
        out_shape=out_shape,
        compiler_params=_params("arbitrary"),
        name="layer_norm",
    )(y, g.reshape(1, d), b.reshape(1, d))


LOG2_E = 1.4426950408889634


def _sb_tile(z2, mask, u01, carry):
    neg_log_keep = jnp.maximum(z2, 0.0) + jnp.log2(1.0 + jnp.exp2(-jnp.abs(z2)))
    if mask is not None:
        neg_log_keep = jnp.where(mask, neg_log_keep, 0.0)
    right = jnp.dot(neg_log_keep.astype(BF16), u01, preferred_element_type=F32) + carry
    w = jnp.exp2(z2 - neg_log_keep - right)
    if mask is not None:
        w = jnp.where(mask, w, 0.0)
    return w, carry + jnp.sum(neg_log_keep, axis=1, keepdims=True)


def _sb_prompt_kernel(bias_ref, q_ref, k_ref, v_ref, o_ref, *, t, hp, scale):
    g = pl.program_id(1)
    i = pl.program_id(2)
    d = q_ref.shape[1] // hp
    rows = _iota((t, t), 0)
    cols = _iota((t, t), 1)
    u01 = (rows > cols).astype(BF16)
    lanes = [slice(s * d, (s + 1) * d) for s in range(hp)]
    qs = [q_ref[:, ln].astype(BF16) for ln in lanes]
    scale2 = scale * LOG2_E
    biases2 = [bias_ref[g * hp + s] * LOG2_E for s in range(hp)]

    def tiles(j, mask, state):
        keys = pl.ds(pl.multiple_of(j * t, t), t)
        zs = [lax.dot_general(qs[s], k_ref[keys, lanes[s]].astype(BF16), NT_DIMS, preferred_element_type=F32)
              * scale2 + biases2[s] for s in range(hp)]
        ws = [_sb_tile(zs[s], mask, u01, state[s][0]) for s in range(hp)]
        return tuple(
            (ws[s][1], state[s][1] + jnp.dot(ws[s][0].astype(BF16), v_ref[keys, lanes[s]].astype(BF16),
                                             preferred_element_type=F32))
            for s in range(hp))

    state = tuple((jnp.zeros((t, 1), F32), jnp.zeros((t, d), F32)) for _ in range(hp))
    state = tiles(i, cols < rows, state)
    state = lax.fori_loop(0, i, lambda it, st: tiles(i - 1 - it, None, st), state)
    for s in range(hp):
        o_ref[:, lanes[s]] = state[s][1].astype(o_ref.dtype)


def _sb_prompt(q, k, v, sb_bias, nseq, t, hp):
    rows, width = q.shape
    heads = sb_bias.shape[0]
    d = width // heads
    seq = rows // nseq
    t = min(t, seq)
    nq = seq // t
    assert seq % t == 0 and d == LANE and heads % hp == 0
    qspec = pl.BlockSpec((t, hp * d), lambda b, g, i, bias: (b * nq + i, g))
    kvspec = pl.BlockSpec((seq, hp * d), lambda b, g, i, bias: (b, g))
    return pl.pallas_call(
        functools.partial(_sb_prompt_kernel, t=t, hp=hp, scale=d ** -0.5),
        grid_spec=pltpu.PrefetchScalarGridSpec(
            num_scalar_prefetch=1,
            grid=(nseq, heads // hp, nq),
            in_specs=[qspec, kvspec, kvspec],
            out_specs=qspec),
        out_shape=jax.ShapeDtypeStruct((rows, width), BF16),
        compiler_params=_params("arbitrary", "arbitrary", "arbitrary"),
        name="sb_prompt",
    )(sb_bias, q, k, v)


def _sb_decode_kernel(pt_ref, qbd_ref, bias_ref, knew_ref, vnew_ref, kc_ref, vc_ref, o_ref,
                      carry_scr, acc_scr, kbuf, vbuf, sem, *, heads, steps, scale):
    b = pl.program_id(0)
    p = pl.program_id(1)
    nq = qbd_ref.shape[1]
    nq_per_head = nq // heads
    width = qbd_ref.shape[2]
    d = width // heads
    pps = kbuf.shape[1]
    n_pages = (steps - 1) * pps

    def page_copies(step, slot):
        copies = []
        for j in range(pps):
            page = pt_ref[b, n_pages - step * pps + j]
            for h in range(heads):
                copies.append(pltpu.make_async_copy(kc_ref.at[page, :, h, :], kbuf.at[slot, j, h], sem.at[0, slot]))
                copies.append(pltpu.make_async_copy(vc_ref.at[page, :, h, :], vbuf.at[slot, j, h], sem.at[1, slot]))
        return copies

    @pl.when(p == 0)
    def _():
        carry_scr[...] = jnp.zeros_like(carry_scr)
        acc_scr[...] = jnp.zeros_like(acc_scr)

    @pl.when(p + 1 < steps)
    def _():
        for c in page_copies(p + 1, (p + 1) % 2):
            c.start()

    def step(kcat, vcat, mask):
        nk = kcat.shape[0]
        u01 = (_iota((nk, nk), 0) > _iota((nk, nk), 1)).astype(BF16)
        z2 = (lax.dot_general(qbd_ref[0], kcat, NT_DIMS, preferred_element_type=F32) * (scale * LOG2_E)
              + bias_ref[...] * LOG2_E)
        w, carry = _sb_tile(z2, mask, u01, carry_scr[...])
        carry_scr[...] = carry
        acc_scr[...] += jnp.dot(w.astype(BF16), vcat, preferred_element_type=F32)

    @pl.when(p == 0)
    def _():
        qi = _div(_iota((nq, PAGE), 0), heads)
        kj = _iota((nq, PAGE), 1)
        step(knew_ref[0].astype(BF16), vnew_ref[0].astype(BF16), kj < qi)

    @pl.when(p > 0)
    def _():
        slot = p % 2
        for c in page_copies(p, slot):
            c.wait()

        def cat(buf):
            return jnp.concatenate(
                [jnp.concatenate([buf[slot, j, h].astype(BF16) for h in range(heads)], axis=1)
                 for j in range(pps)], axis=0)

        step(cat(kbuf), cat(vbuf), None)

    @pl.when(p == steps - 1)
    def _():
        acc = acc_scr[...]
        keep = _mod(_iota((nq, width), 0), heads) == _div(_iota((nq, width), 1), d)
        sel = jnp.where(keep, acc, 0.0)
        out_row = _iota((o_ref.shape[1], width), 0)
        out = jnp.zeros((o_ref.shape[1], width), F32)
        for t in range(nq_per_head):
            head_sum = jnp.sum(sel[t * heads:(t + 1) * heads], axis=0, keepdims=True)
            out = jnp.where(out_row == t, head_sum, out)
        o_ref[0] = out


def _sb_decode(q, k, v, cache_k, cache_v, page_table, sb_bias):
    nseq, n_pages = page_table.shape
    heads = sb_bias.shape[0]
    width = q.shape[1]
    d = width // heads
    n_new = q.shape[0] // nseq
    nq = n_new * heads
    assert cache_k.shape[1] == PAGE and d == LANE and n_new <= SUBLANE
    q4 = q.reshape(nseq, n_new, heads, 1, d)
    eye = jnp.eye(heads, dtype=F32).reshape(1, 1, heads, heads, 1)
    qbd = (q4 * eye).reshape(nseq, nq, width).astype(BF16)
    bias_col = jnp.tile(sb_bias.astype(F32), n_new).reshape(nq, 1)
    pad = ((0, 0), (0, PAGE - n_new), (0, 0))
    knew = jnp.pad(k.reshape(nseq, n_new, width), pad)
    vnew = jnp.pad(v.reshape(nseq, n_new, width), pad)
    pps = DECODE_PAGES_PER_STEP
    assert n_pages % pps == 0
    steps = n_pages // pps + 1

    seq3 = lambda b, p, pt: (b, 0, 0)
    out = pl.pallas_call(
        functools.partial(_sb_decode_kernel, heads=heads, steps=steps, scale=d ** -0.5),
        grid_spec=pltpu.PrefetchScalarGridSpec(
            num_scalar_prefetch=1,
            grid=(nseq, steps),
            in_specs=[pl.BlockSpec((1, nq, width), seq3),
                      pl.BlockSpec((nq, 1), lambda b, p, pt: (0, 0)),
                      pl.BlockSpec((1, PAGE, width), seq3),
                      pl.BlockSpec((1, PAGE, width), seq3),
                      pl.BlockSpec(memory_space=pl.ANY),
                      pl.BlockSpec(memory_space=pl.ANY)],
            out_specs=pl.BlockSpec((1, SUBLANE, width), seq3),
            scratch_shapes=[pltpu.VMEM((nq, 1), F32), pltpu.VMEM((nq, width), F32),
                            pltpu.VMEM((2, pps, heads, PAGE, d), F32), pltpu.VMEM((2, pps, heads, PAGE, d), F32),
                            pltpu.SemaphoreType.DMA((2, 2))]),
        out_shape=jax.ShapeDtypeStruct((nseq, SUBLANE, width), F32),
        compiler_params=_params("arbitrary", "arbitrary"),
        name="sb_decode",
    )(page_table, qbd, bias_col, knew, vnew, cache_k, cache_v)
    return out[:, :n_new].reshape(nseq * n_new, width).astype(BF16)


def _ssd_kernel(x_ref, b_ref, c_ref, px_ref, pb_ref, pc_ref, wx_ref, wb_ref, wc_ref,
                bx_ref, bb_ref, bc_ref, dt_ref, dtb_ref, alog_ref, dskip_ref, z_ref, nw_ref, h0_ref,
                y_ref, hout_ref, h_scr, ex_scr, eb_scr, ec_scr, *, valid_len, n_chunks, rep, hd, gp):
    c = pl.program_id(2)
    cl = x_ref.shape[0]
    xw = rep * hd
    n = h_scr.shape[1]

    @pl.when(c == 0)
    def _():
        h_scr[...] = h0_ref[0].reshape(gp * xw, n)
        ex_scr[0:SUBLANE, :] = px_ref[0]
        eb_scr[0:SUBLANE, :] = pb_ref[0]
        ec_scr[0:SUBLANE, :] = pc_ref[0]

    def conv_silu(e_scr, raw_ref, w_ref, bias_ref):
        e_scr[SUBLANE:SUBLANE + cl, :] = raw_ref[...]
        acc = bias_ref[...]
        for i in range(CONV_TAPS):
            lo = SUBLANE - (CONV_TAPS - 1) + i
            acc = acc + e_scr[lo:lo + cl, :] * w_ref[i:i + 1, :]
        e_scr[0:SUBLANE, :] = e_scr[cl:cl + SUBLANE, :]
        return _silu(acc)

    xc_all = conv_silu(ex_scr, x_ref, wx_ref, bx_ref)
    bm_all = conv_silu(eb_scr, b_ref, wb_ref, bb_ref)
    cm_all = conv_silu(ec_scr, c_ref, wc_ref, bc_ref)

    row = c * cl + _iota((cl, 1), 0)
    t_idx = _iota((cl, cl), 0)
    s_idx = _iota((cl, cl), 1)
    tril = t_idx >= s_idx
    tril16 = tril.astype(BF16)
    lane = _iota((1, xw), 1)
    in_head = [(lane >= r * hd) & (lane < (r + 1) * hd) for r in range(rep)]
    groups = range(gp)

    xc = [xc_all[:, s * xw:(s + 1) * xw] for s in groups]
    bm16 = [bm_all[:, s * n:(s + 1) * n].astype(BF16) for s in groups]
    cm16 = [cm_all[:, s * n:(s + 1) * n].astype(BF16) for s in groups]
    h = [h_scr[s * xw:(s + 1) * xw, :] for s in groups]

    def step_sizes(s):
        dtr = dt_ref[0, s] + dtb_ref[s]
        return jnp.where(row < valid_len, jnp.maximum(dtr, 0.0) + _softplus_neg_abs(dtr), 0.0)

    dt = [step_sizes(s) for s in groups]
    cum = [_split_dot_left(tril16, dt[s] * -jnp.exp(alog_ref[s]), 3) for s in groups]
    g = [lax.dot_general(cm16[s], bm16[s], NT_DIMS, preferred_element_type=F32) for s in groups]
    inter = [lax.dot_general(cm16[s], h[s].astype(BF16), NT_DIMS, preferred_element_type=F32) for s in groups]
    cum_t = [cum[s].T for s in groups]
    dt_t = [dt[s].T for s in groups]

    def head_terms(s):
        cum_last = cum[s][cl - 1:cl, :]
        decay_last = jnp.exp(cum_last)
        ws, xrs, hdecay = [], [], []
        e_all = jnp.zeros((cl, xw), F32)
        tail_all = jnp.zeros((cl, xw), F32)
        for r in range(rep):
            ccol = cum[s][:, r:r + 1]
            decay = jnp.exp(jnp.where(tril, ccol - cum_t[s][r:r + 1, :], -jnp.inf))
            ws.append((g[s] * decay * dt_t[s][r:r + 1, :]).astype(BF16))
            xrs.append(jnp.where(in_head[r], xc[s], 0.0).astype(BF16))
            e_all = jnp.where(in_head[r], jnp.exp(ccol), e_all)
            tail_all = jnp.where(in_head[r], jnp.exp(cum_last[:, r:r + 1] - ccol) * dt[s][:, r:r + 1], tail_all)
            hdecay.append(jnp.broadcast_to(decay_last[:, r:r + 1], (hd, n)))
        return (jnp.concatenate(ws, axis=1), jnp.concatenate(xrs, axis=0), e_all, tail_all,
                jnp.concatenate(hdecay, axis=0))

    terms = [head_terms(s) for s in groups]
    intra = [jnp.dot(terms[s][0], terms[s][1], preferred_element_type=F32) for s in groups]
    y = [intra[s] + inter[s] * terms[s][2] + xc[s] * dskip_ref[s] for s in groups]

    def transposed(xt):
        return jnp.concatenate([xt[:, j * LANE:(j + 1) * LANE].T for j in range(xw // LANE)], axis=0)

    xt_t = [transposed(xc[s] * terms[s][3]).astype(BF16) for s in groups]
    h_new = [h[s] * terms[s][4] + jnp.dot(xt_t[s], bm16[s], preferred_element_type=F32) for s in groups]
    for s in groups:
        h_scr[s * xw:(s + 1) * xw, :] = h_new[s]

    for s in groups:
        lanes = slice(s * xw, (s + 1) * xw)
        gated = y[s] * _silu(z_ref[:, lanes])
        ms = jnp.mean(gated * gated, axis=-1, keepdims=True)
        y_ref[:, lanes] = (gated * lax.rsqrt(ms + RMS_EPS) * nw_ref[:, lanes]).astype(y_ref.dtype)

    @pl.when(c == n_chunks - 1)
    def _():
        hout_ref[0] = h_scr[...].reshape(hout_ref.shape[1:])


def _ssd(xbc, dt_raw, z, conv_prev, h0, valid_len, conv_w, conv_b, dt_bias, a_log, d_skip, norm_w, groups):
    nseq, heads, hd, n = h0.shape
    rows, ch = xbc.shape
    seq = rows // nseq
    rep = heads // groups
    xw = rep * hd
    ssm_w = heads * hd
    cl = CHUNK
    nc = seq // cl
    gp = SSD_GROUPS_PER_STEP
    assert seq % cl == 0 and n == LANE and xw % LANE == 0 and ch == ssm_w + 2 * groups * n
    assert groups % gp == 0 and ssm_w % (gp * n) == 0
    b_off = ssm_w // (gp * n)
    c_off = b_off + groups // gp

    prev = jnp.pad(conv_prev, ((0, 0), (SUBLANE - (CONV_TAPS - 1), 0), (0, 0)))
    dt_g = dt_raw.reshape(nseq, seq, groups, rep).transpose(0, 2, 1, 3)
    dt_g = jnp.pad(dt_g, ((0, 0), (0, 0), (0, 0), (0, LANE - rep)))

    def lane_pad(p):
        return jnp.pad(p.astype(F32).reshape(groups, 1, rep), ((0, 0), (0, 0), (0, LANE - rep)))

    dskip_rep = jnp.repeat(d_skip.astype(F32), hd).reshape(groups, 1, xw)
    cb = conv_b.reshape(1, ch)

    def rows_at(width, off):
        return pl.BlockSpec((cl, width), lambda b, g, c: (b * nc + c, off + g))

    def prev_at(width, off):
        return pl.BlockSpec((1, SUBLANE, width), lambda b, g, c: (b, 0, off + g))

    def taps_at(nrows, width, off):
        return pl.BlockSpec((nrows, width), lambda b, g, c: (0, off + g))

    par = pl.BlockSpec((gp, 1, LANE), lambda b, g, c: (g, 0, 0))
    state = pl.BlockSpec((1, gp * rep, hd, n), lambda b, g, c: (b, g, 0, 0))
    xs_w, bc_w = gp * xw, gp * n
    y, h = pl.pallas_call(
        functools.partial(_ssd_kernel, valid_len=valid_len, n_chunks=nc, rep=rep, hd=hd, gp=gp),
        grid=(nseq, groups // gp, nc),
        in_specs=[rows_at(xs_w, 0), rows_at(bc_w, b_off), rows_at(bc_w, c_off),
                  prev_at(xs_w, 0), prev_at(bc_w, b_off), prev_at(bc_w, c_off),
                  taps_at(CONV_TAPS, xs_w, 0), taps_at(CONV_TAPS, bc_w, b_off), taps_at(CONV_TAPS, bc_w, c_off),
                  taps_at(1, xs_w, 0), taps_at(1, bc_w, b_off), taps_at(1, bc_w, c_off),
                  pl.BlockSpec((1, gp, cl, LANE), lambda b, g, c: (b, g, c, 0)),
                  par, par,
                  pl.BlockSpec((gp, 1, xw), lambda b, g, c: (g, 0, 0)),
                  rows_at(xs_w, 0),
                  taps_at(1, xs_w, 0),
                  state],
        out_specs=[rows_at(xs_w, 0), state],
        out_shape=[jax.ShapeDtypeStruct((rows, ssm_w), BF16), jax.ShapeDtypeStruct(h0.shape, F32)],
        scratch_shapes=[pltpu.VMEM((gp * xw, n), F32),
                        pltpu.VMEM((cl + SUBLANE, xs_w), F32),
                        pltpu.VMEM((cl + SUBLANE, bc_w), F32),
                        pltpu.VMEM((cl + SUBLANE, bc_w), F32)],
        compiler_params=_params("arbitrary", "arbitrary", "arbitrary"),
        name="ssd",
    )(xbc, xbc, xbc, prev, prev, prev, conv_w, conv_w, conv_w, cb, cb, cb,
      dt_g, lane_pad(dt_bias), lane_pad(a_log), dskip_rep, z, norm_w.reshape(1, ssm_w), h0)
    return y, h


def _pool_kernel(u_ref, prev_ref, w_ref, s_ref, o_ref, ext_scr, *, pos0, gd):
    i = pl.program_id(1)
    tr = u_ref.shape[0]

    @pl.when(i == 0)
    def _():
        ext_scr[0:POOL_HIST, :] = prev_ref[0]

    ext_scr[POOL_HIST:POOL_HIST + tr, :] = u_ref[...]
    pos = (pos0 + i * tr + _iota((tr, 1), 0)).astype(F32)
    for g, win in enumerate(POOL_WINDOWS):
        lanes = slice(g * gd, (g + 1) * gd)
        total = ext_scr[POOL_HIST:POOL_HIST + tr, lanes]
        for j in range(1, win):
            total = total + ext_scr[POOL_HIST - j:POOL_HIST - j + tr, lanes]
        count = jnp.minimum(pos + 1.0, float(win))
        dlt = total / count - u_ref[:, lanes]
        y = jnp.dot(dlt.astype(BF16), w_ref[g], preferred_element_type=F32)
        o_ref[:, lanes] = (y * s_ref[:, lanes]).astype(o_ref.dtype)
    ext_scr[0:POOL_HIST, :] = ext_scr[tr:tr + POOL_HIST, :]


def _pool(u, prev, pos0, w_pool16, pool_scale, tr):
    nseq = prev.shape[0]
    rows, width = u.shape
    seq = rows // nseq
    tr = min(tr, seq)
    nt = seq // tr
    ng, gd, _ = w_pool16.shape
    assert seq % tr == 0 and ng == len(POOL_WINDOWS) and ng * gd == width
    prev16 = jnp.pad(prev, ((0, 0), (POOL_HIST - prev.shape[1], 0), (0, 0)))
    return pl.pallas_call(
        functools.partial(_pool_kernel, pos0=pos0, gd=gd),
        grid=(nseq, nt),
        in_specs=[pl.BlockSpec((tr, width), lambda b, i: (b * nt + i, 0)),
                  pl.BlockSpec((1, POOL_HIST, width), lambda b, i: (b, 0, 0)),
                  pl.BlockSpec((ng, gd, gd), lambda b, i: (0, 0, 0)),
                  pl.BlockSpec((1, width), lambda b, i: (0, 0))],
        out_specs=pl.BlockSpec((tr, width), lambda b, i: (b * nt + i, 0)),
        out_shape=jax.ShapeDtypeStruct((rows, width), BF16),
        scratch_shapes=[pltpu.VMEM((tr + POOL_HIST, width), F32)],
        compiler_params=_params("arbitrary", "arbitrary"),
        name="pool",
    )(u, prev16, w_pool16, pool_scale.reshape(1, width))


def _gla_kernel(q_ref, k_ref, v_ref, r_ref, glr_ref, w2_ref, b2_ref, nw_ref, s0_ref,
                o_ref, sout_ref, s_scr, *, valid_len, n_chunks, scale, hp):
    c = pl.program_id(2)
    cl = q_ref.shape[0]
    dk = s_scr.shape[1]
    dv = s_scr.shape[2]
    heads = range(hp)

    @pl.when(c == 0)
    def _():
        s_scr[...] = s0_ref[0]

    row = c * cl + _iota((cl, 1), 0)
    valid = row < valid_len
    pre = jnp.dot(glr_ref[...], w2_ref[...], preferred_element_type=F32) + b2_ref[...]
    lg_all = (jnp.minimum(pre, 0.0) - _softplus_neg_abs(pre)) / GLA_TAU
    lg_all = jnp.where(valid, lg_all, 0.0)
    qs_all = q_ref[...] * scale
    k_all = jnp.where(valid, k_ref[...], 0.0)
    kl = [slice(h * dk, (h + 1) * dk) for h in heads]
    vl = [slice(h * dv, (h + 1) * dv) for h in heads]
    lg = [lg_all[:, kl[h]] for h in heads]
    qs = [qs_all[:, kl[h]] for h in heads]
    k = [k_all[:, kl[h]] for h in heads]
    v16 = [v_ref[:, vl[h]].astype(BF16) for h in heads]

    t_idx = _iota((cl, cl), 0)
    s_idx = _iota((cl, cl), 1)
    r_idx = _iota((cl, 1), 0)
    def masked_qk(qd, kd, mask):
        return jnp.where(mask, lax.dot_general(qd.astype(BF16), kd.astype(BF16), NT_DIMS,
                                               preferred_element_type=F32), 0.0)

    att = [masked_qk(qs[h], k[h], t_idx == s_idx) for h in heads]
    pre_b = lg
    suf_b = [jnp.zeros_like(lg[h]) for h in heads]
    tot_b = lg
    bsz = 1
    while bsz < cl:
        siblings = ((_div(t_idx, 2 * bsz) == _div(s_idx, 2 * bsz))
                    & (_mod(_div(t_idx, bsz), 2) == 1) & (_mod(_div(s_idx, bsz), 2) == 0))
        att = [att[h] + masked_qk(qs[h] * jnp.exp(pre_b[h]), k[h] * jnp.exp(suf_b[h]), siblings) for h in heads]
        is_right = _mod(_div(r_idx, bsz), 2) == 1
        left_tot = [pltpu.roll(tot_b[h], bsz, axis=0) for h in heads]
        right_tot = [pltpu.roll(tot_b[h], cl - bsz, axis=0) for h in heads]
        pre_b = [pre_b[h] + jnp.where(is_right, left_tot[h], 0.0) for h in heads]
        suf_b = [suf_b[h] + jnp.where(is_right, 0.0, right_tot[h]) for h in heads]
        tot_b = [tot_b[h] + jnp.where(is_right, left_tot[h], right_tot[h]) for h in heads]
        bsz *= 2

    s = [s_scr[h] for h in heads]
    qd = [(qs[h] * jnp.exp(pre_b[h])).astype(BF16) for h in heads]
    kd_t = [(k[h] * jnp.exp(suf_b[h])).T.astype(BF16) for h in heads]
    o = [jnp.dot(att[h].astype(BF16), v16[h], preferred_element_type=F32)
         + jnp.dot(qd[h], s[h].astype(BF16), preferred_element_type=F32) for h in heads]
    chunk_decay = [jnp.exp(tot_b[h].T) for h in heads]
    s_new = [s[h] * jnp.concatenate([chunk_decay[h]] * (dv // cl), axis=1)
             + jnp.dot(kd_t[h], v16[h], preferred_element_type=F32) for h in heads]
    for h in heads:
        s_scr[h] = s_new[h]

    for h in heads:
        ms = jnp.mean(o[h] * o[h], axis=-1, keepdims=True)
        o_ref[:, vl[h]] = (o[h] * lax.rsqrt(ms + RMS_EPS) * nw_ref[:, vl[h]]
                           * _silu(r_ref[:, vl[h]])).astype(o_ref.dtype)

    @pl.when(c == n_chunks - 1)
    def _():
        sout_ref[0] = s_scr[...]


def _gla(q, k, v, r, glr, w2, b2, norm_w, s0, valid_len):
    nseq, heads, dk, dv = s0.shape
    rows = q.shape[0]
    seq = rows // nseq
    cl = CHUNK
    nc = seq // cl
    hp = GLA_HEADS_PER_STEP
    assert seq % cl == 0 and dk == cl and dv % cl == 0 and glr.shape[1] == LANE and heads % hp == 0

    def rows_at(width, col):
        return pl.BlockSpec((cl, width), (lambda b, h, c: (b * nc + c, h)) if col else (lambda b, h, c: (b * nc + c, 0)))

    def head_at(nrows, width):
        return pl.BlockSpec((nrows, width), lambda b, h, c: (0, h))

    state = pl.BlockSpec((1, hp, dk, dv), lambda b, h, c: (b, h, 0, 0))
    kw, vw = hp * dk, hp * dv
    return pl.pallas_call(
        functools.partial(_gla_kernel, valid_len=valid_len, n_chunks=nc, scale=dk ** -0.5, hp=hp),
        grid=(nseq, heads // hp, nc),
        in_specs=[rows_at(kw, True), rows_at(kw, True), rows_at(vw, True), rows_at(vw, True),
                  rows_at(LANE, False), head_at(LANE, kw), head_at(1, kw), head_at(1, vw), state],
        out_specs=[rows_at(vw, True), state],
        out_shape=[jax.ShapeDtypeStruct((rows, heads * dv), BF16), jax.ShapeDtypeStruct(s0.shape, F32)],
        scratch_shapes=[pltpu.VMEM((hp, dk, dv), F32)],
        compiler_params=_params("arbitrary", "arbitrary", "arbitrary"),
        name="gla",
    )(q, k, v, r, glr, w2, b2.reshape(1, heads * dk), norm_w.reshape(1, heads * dv), s0)


def _pad_seq(a, nseq, seq_pad):
    n = a.shape[0] // nseq
    a3 = jnp.pad(a.reshape(nseq, n, a.shape[1]), ((0, 0), (0, seq_pad - n), (0, 0)))
    return a3.reshape(nseq * seq_pad, a.shape[1])


def _unpad_seq(a, nseq, n):
    return a.reshape(nseq, -1, a.shape[1])[:, :n].reshape(nseq * n, a.shape[1])


def _project(xp16, xs16, w, sizes, name, bf16_segments=()):
    outs_p, outs_s = [], []
    off = 0
    w_nk = w.T
    for idx, size in enumerate(sizes):
        dtype = BF16 if idx in bf16_segments else F32
        op, os_ = _proj([xp16], [xs16], w_nk, True, off, size, dtype, PROJ_TM, PROJ_TN, f"{name}_{idx}")
        off += size
        outs_p.append(op)
        outs_s.append(os_)
    return outs_p, outs_s


def kernel(x_prompt, x_sample, cache_k, cache_v, page_table, state_conv, state_ssm, state_pool, state_gla, w_in0, sb_bias, conv_w, conv_b, dt_bias, a_log, d_skip, ssm_norm_w, w_out0, w_in1, w_pool, pool_scale, gla_w2, gla_b2, gla_norm_w, w_out1, ln_g, ln_b, w_gu, w_down):
    bp, seq, dm = x_prompt.shape
    bs, n_new, _ = x_sample.shape
    depth = ln_g.shape[0]
    alpha = (2 * depth) ** 0.25
    sb_heads, sb_d = cache_k.shape[2], cache_k.shape[3]
    sb_w = sb_heads * sb_d
    ssm_heads, ssm_hd, ssm_n = state_ssm.shape[1:]
    ssm_w = ssm_heads * ssm_hd
    conv_ch = state_conv.shape[2]
    groups = (conv_ch - ssm_w) // (2 * ssm_n)
    pool_w = state_pool.shape[2]
    gla_heads, gla_dk, gla_dv = state_gla.shape[1:]
    gla_kw, gla_vw = gla_heads * gla_dk, gla_heads * gla_dv
    gla_rank = gla_w2.shape[0]
    past_len = page_table.shape[1] * cache_k.shape[1]
    seq_pad = CHUNK

    xp = x_prompt.reshape(bp * seq, dm)
    xs = x_sample.reshape(bs * n_new, dm)
    xp16, xs16 = xp.astype(BF16), xs.astype(BF16)

    def lazy_ln(y, sub, layer):
        g, b = ln_g[layer, sub], ln_b[layer, sub]
        x16, mu, rstd = _ln(y, g, b, LN_TM)
        return x16, (y, mu, rstd, g.reshape(1, dm), b.reshape(1, dm))

    def mix_ffn(rp, rs, mixed_p, mixed_s, w_out, layer):
        yp, ys = _proj(mixed_p, mixed_s, w_out, False, 0, dm, F32, PROJ_TM, PROJ_TN, f"out{layer}",
                       res=(rp, rs), alpha=alpha)
        xp16, rp = lazy_ln(yp, 0, layer)
        xs16, rs = lazy_ln(ys, 0, layer)
        hp, hs, w_down16 = _gate_up(xp16, xs16, w_gu, w_down, layer, PROJ_TM, GU_TN)
        return (_mm_res(hp, w_down16, rp, alpha, DOWN_TM, DOWN_TN),
                _mm_res(hs, w_down16, rs, alpha, DOWN_TM, DOWN_TN))

    split0 = (sb_w, sb_w, sb_w, ssm_w, conv_ch, ssm_heads)
    (qp, kp, vp, zp, xbcp, dtp), (qs, ks, vs, zs, xbcs, dts) = _project(xp16, xs16, w_in0, split0, "in0",
                                                                        bf16_segments=(0,))
    dtp, dts = dtp[:, :ssm_heads], dts[:, :ssm_heads]

    oap = _sb_prompt(qp, kp, vp, sb_bias, bp, SB_T, SB_HEADS_PER_STEP)
    oas = _sb_decode(qs, ks, vs, cache_k, cache_v, page_table, sb_bias)

    ssd_w = (conv_w, conv_b, dt_bias, a_log, d_skip, ssm_norm_w, groups)
    yp, ssm_p = _ssd(xbcp, dtp, zp, jnp.zeros((bp, CONV_TAPS - 1, conv_ch), F32),
                     jnp.zeros((bp,) + state_ssm.shape[1:], F32), seq, *ssd_w)
    ys, ssm_s = _ssd(_pad_seq(xbcs, bs, seq_pad), _pad_seq(dts, bs, seq_pad), _pad_seq(zs, bs, seq_pad),
                     state_conv, state_ssm, n_new, *ssd_w)
    ys = _unpad_seq(ys, bs, n_new)

    y1p, y1s = mix_ffn((xp,), (xs,), [oap, yp], [oas, ys], w_out0, 0)
    xp16, res_p = lazy_ln(y1p, 1, 0)
    xs16, res_s = lazy_ln(y1s, 1, 0)

    split1 = (pool_w, gla_kw, gla_kw, gla_vw, gla_vw, gla_rank)
    (up, gqp, gkp, gvp, grp, glrp), (us, gqs, gks, gvs, grs, glrs) = _project(xp16, xs16, w_in1, split1, "in1")

    w_pool16 = w_pool.astype(BF16)
    ocp = _pool(up, jnp.zeros((bp, POOL_HIST - 1, pool_w), F32), 0, w_pool16, pool_scale, POOL_TM)
    us8 = _pad_seq(us, bs, SUBLANE)
    ocs = _unpad_seq(_pool(us8, state_pool, past_len, w_pool16, pool_scale, SUBLANE), bs, n_new)

    w2_16 = jnp.pad(gla_w2, ((0, LANE - gla_rank), (0, 0))).astype(BF16)
    gla_w = (w2_16, gla_b2, gla_norm_w)
    odp, gla_p = _gla(gqp, gkp, gvp, grp, glrp.astype(BF16), *gla_w,
                      jnp.zeros((bp,) + state_gla.shape[1:], F32), seq)
    pad = lambda a: _pad_seq(a, bs, seq_pad)
    ods, gla_s = _gla(pad(gqs), pad(gks), pad(gvs), pad(grs), pad(glrs).astype(BF16), *gla_w, state_gla, n_new)
    ods = _unpad_seq(ods, bs, n_new)

    y2p, y2s = mix_ffn(res_p, res_s, [ocp, odp], [ocs, ods], w_out1, 1)
    xp = _ln(y2p, ln_g[1, 1], ln_b[1, 1], LN_TM, final=True)
    xs = _ln(y2s, ln_g[1, 1], ln_b[1, 1], LN_TM, final=True)

    hd4 = lambda a, nseq: a.reshape(nseq, -1, sb_heads, sb_d)
    conv_p = xbcp.reshape(bp, seq, conv_ch)[:, seq - (CONV_TAPS - 1):]
    conv_s = jnp.concatenate([state_conv, xbcs.reshape(bs, n_new, conv_ch)], axis=1)[:, n_new:]
    pool_p = up.reshape(bp, seq, pool_w)[:, seq - (POOL_HIST - 1):]
    pool_s = jnp.concatenate([state_pool, us.reshape(bs, n_new, pool_w)], axis=1)[:, n_new:]
    return (xp.reshape(bp, seq, dm), xs.reshape(bs, n_new, dm),
            hd4(kp, bp), hd4(vp, bp), hd4(ks, bs), hd4(vs, bs),
            conv_p, conv_s, ssm_p, ssm_s, pool_p, pool_s, gla_p, gla_s)
```

```python
import functools

import jax
import jax.numpy as jnp
from jax import lax
from jax.experimental import pallas as pl
from jax.experimental.pallas import tpu as pltpu

F32 = jnp.float32
BF16 = jnp.bfloat16

LANE = 128
SUBLANE = 8
VMEM_LIMIT = 56 * 1024 * 1024

LN_EPS = 1e-5
RMS_EPS = 1e-5
GLA_TAU = 16.0
PAGE = 128
POOL_WINDOWS = (2, 4, 8, 16)
POOL_HIST = 16
CONV_TAPS = 4
CHUNK = 128

PROJ_TM, PROJ_TN = 1024, 512
GU_TN = 256
DOWN_TM, DOWN_TN = 512, 512
LN_TM = 512
SB_T, SB_HEADS_PER_STEP = 256, 8
POOL_TM = 512
DECODE_PAGES_PER_STEP = 8
SSD_GROUPS_PER_STEP = 4
GLA_HEADS_PER_STEP = 8

NT_DIMS = (((1,), (1,)), ((), ()))


def _params(*sem):
    return pltpu.CompilerParams(dimension_semantics=sem, vmem_limit_bytes=VMEM_LIMIT)


def _silu(x):
    return x * (0.5 * jnp.tanh(0.5 * x) + 0.5)


def _softplus_neg_abs(x):
    return jnp.log1p(jnp.exp(-jnp.abs(x)))


def _iota(shape, dim):
    return lax.broadcasted_iota(jnp.int32, shape, dim)


def _div(x, c):
    return lax.shift_right_logical(x, (c.bit_length() - 1)) if c & (c - 1) == 0 else x // c


def _mod(x, c):
    return (x & (c - 1)) if c & (c - 1) == 0 else x % c


def _bf16_pieces(x, terms):
    pieces = [x.astype(BF16)]
    for _ in range(terms - 1):
        x = x - pieces[-1].astype(F32)
        pieces.append(x.astype(BF16))
    return pieces


def _add_all(terms):
    return functools.reduce(lambda a, b: a + b, terms)


def _split_dot_left(m01, x, terms):
    return _add_all([jnp.dot(m01, p, preferred_element_type=F32) for p in _bf16_pieces(x, terms)])


def _residual_tile(refs):
    if len(refs) == 1:
        return refs[0][...]
    y_ref, mu_ref, rstd_ref, g_ref, b_ref = refs
    return (y_ref[...] - mu_ref[...]) * rstd_ref[...] * g_ref[...] + b_ref[...]


def _residual_specs(res, tm, tn, row_map, col_map):
    tile = pl.BlockSpec((tm, tn), lambda *ids: (row_map(*ids), col_map(*ids)))
    if len(res) == 1:
        return [tile]
    per_row = pl.BlockSpec((tm, 1), lambda *ids: (row_map(*ids), 0))
    per_col = pl.BlockSpec((1, tn), lambda *ids: (0, col_map(*ids)))
    return [tile, per_row, per_row, per_col, per_col]


def _mm_res_kernel(x_ref, w_ref, *refs, alpha):
    o_ref = refs[-1]
    o_ref[...] = alpha * _residual_tile(refs[:-1]) + jnp.dot(x_ref[...], w_ref[...], preferred_element_type=F32)


def _mm_res(x, w, res, alpha, tm, tn):
    m, k = x.shape
    n = w.shape[1]
    tm = min(tm, m)
    tn = min(tn, n)
    assert m % tm == 0 and n % tn == 0
    return pl.pallas_call(
        functools.partial(_mm_res_kernel, alpha=alpha),
        grid=(m // tm, n // tn),
        in_specs=[pl.BlockSpec((tm, k), lambda i, j: (i, 0)),
                  pl.BlockSpec((k, tn), lambda i, j: (0, j))]
                 + _residual_specs(res, tm, tn, lambda i, j: i, lambda i, j: j),
        out_specs=pl.BlockSpec((tm, tn), lambda i, j: (i, j)),
        out_shape=jax.ShapeDtypeStruct((m, n), F32),
        compiler_params=_params("arbitrary", "arbitrary"),
        name="down",
    )(x, w, *res)


def _dot_parts(x_refs, w16, w_is_nk):
    terms, off = [], 0
    for x_ref in x_refs:
        k = x_ref.shape[1]
        if w_is_nk:
            terms.append(lax.dot_general(x_ref[...], w16[:, off:off + k], NT_DIMS, preferred_element_type=F32))
        else:
            terms.append(jnp.dot(x_ref[...], w16[off:off + k, :], preferred_element_type=F32))
        off += k
    return _add_all(terms)


def _proj_kernel(*refs, n_parts, w_is_nk, col0, ncols, nb, alpha):
    xps, xss = refs[:n_parts], refs[n_parts:2 * n_parts]
    w_ref = refs[2 * n_parts]
    res = refs[2 * n_parts + 1:-5]
    res = (res[:len(res) // 2], res[len(res) // 2:])
    op_ref, os_ref, wstage, sem, w16 = refs[-5:]

    def finish(acc, res_refs, o_ref):
        if alpha is not None:
            acc = alpha * _residual_tile(res_refs) + acc
        o_ref[...] = acc.astype(o_ref.dtype)

    n = pl.program_id(0)
    m = pl.program_id(1)
    tn = op_ref.shape[1]
    nv = min(tn, ncols)

    def tile_copy(t, slot):
        cols = pl.ds(pl.multiple_of(col0 + t * tn, nv), nv)
        if w_is_nk:
            return pltpu.make_async_copy(w_ref.at[cols, :], wstage.at[slot, pl.ds(0, nv), :], sem.at[slot])
        return pltpu.make_async_copy(w_ref.at[:, cols], wstage.at[slot, :, pl.ds(0, nv)], sem.at[slot])

    @pl.when(m == 0)
    def _():
        slot = n % 2

        @pl.when(n == 0)
        def _():
            tile_copy(0, 0).start()

        tile_copy(n, slot).wait()
        if nv == tn:
            w16[...] = wstage[slot].astype(BF16)
        elif w_is_nk:
            w16[0:nv, :] = wstage[slot, 0:nv, :].astype(BF16)
            w16[nv:tn, :] = jnp.zeros((tn - nv, w16.shape[1]), BF16)
        else:
            w16[:, 0:nv] = wstage[slot, :, 0:nv].astype(BF16)
            w16[:, nv:tn] = jnp.zeros((w16.shape[0], tn - nv), BF16)

        @pl.when(n + 1 < nb)
        def _():
            tile_copy(n + 1, 1 - slot).start()

        finish(_dot_parts(xss, w16, w_is_nk), res[1], os_ref)

    finish(_dot_parts(xps, w16, w_is_nk), res[0], op_ref)


def _col_tile(col0, ncols, tn_max):
    if ncols < LANE:
        assert col0 % LANE == 0
        return LANE
    tn = tn_max
    while ncols % tn or col0 % tn:
        tn //= 2
    assert tn >= LANE
    return tn


def _proj(xps, xss, w, w_is_nk, col0, ncols, out_dtype, tm, tn_max, name, res=None, alpha=None):
    m = xps[0].shape[0]
    s = xss[0].shape[0]
    k = w.shape[1] if w_is_nk else w.shape[0]
    tm = min(tm, m)
    tn = _col_tile(col0, ncols, tn_max)
    nb = pl.cdiv(ncols, tn)
    assert m % tm == 0 and sum(x.shape[1] for x in xps) == k and min(tn, ncols) % SUBLANE == 0
    tile_shape = (tn, k) if w_is_nk else (k, tn)
    out_tiles = [pl.BlockSpec((tm, tn), lambda n, i: (i, n)), pl.BlockSpec((s, tn), lambda n, i: (0, n))]
    assert (res is None) == (alpha is None)
    res_arrays, res_specs = [], []
    if res is not None:
        res_arrays = [*res[0], *res[1]]
        res_specs = (_residual_specs(res[0], tm, tn, lambda n, i: i, lambda n, i: n)
                     + _residual_specs(res[1], s, tn, lambda n, i: 0, lambda n, i: n))
    return pl.pallas_call(
        functools.partial(_proj_kernel, n_parts=len(xps), w_is_nk=w_is_nk, col0=col0, ncols=ncols, nb=nb,
                          alpha=alpha),
        grid=(nb, m // tm),
        in_specs=([pl.BlockSpec((tm, x.shape[1]), lambda n, i: (i, 0)) for x in xps]
                  + [pl.BlockSpec((s, x.shape[1]), lambda n, i: (0, 0)) for x in xss]
                  + [pl.BlockSpec(memory_space=pl.ANY)]
                  + res_specs),
        out_specs=out_tiles,
        out_shape=[jax.ShapeDtypeStruct((m, nb * tn), out_dtype), jax.ShapeDtypeStruct((s, nb * tn), out_dtype)],
        scratch_shapes=[pltpu.VMEM((2,) + tile_shape, F32), pltpu.SemaphoreType.DMA((2,)),
                        pltpu.VMEM(tile_shape, BF16)],
        compiler_params=_params("arbitrary", "arbitrary"),
        name=name,
    )(*xps, *xss, w, *res_arrays)


def _gu_kernel(xp_ref, xs_ref, wgu_ref, wd_ref, op_ref, os_ref, wd16_ref, wstage, sem, wg16, wu16,
               *, layer, nb):
    n = pl.program_id(0)
    m = pl.program_id(1)
    tn = wg16.shape[1]

    def tile_copies(t, slot):
        cols = [pl.ds(pl.multiple_of((t + half * nb) * tn, tn), tn) for half in range(2)]
        return [pltpu.make_async_copy(wgu_ref.at[layer, :, cols[half]], wstage.at[slot, half], sem.at[slot, half])
                for half in range(2)]

    def act(x_ref, o_ref):
        x = x_ref[...]
        g = jnp.dot(x, wg16[...], preferred_element_type=F32)
        u = jnp.dot(x, wu16[...], preferred_element_type=F32)
        o_ref[...] = (_silu(g) * u).astype(o_ref.dtype)

    @pl.when(m == 0)
    def _():
        slot = n % 2

        @pl.when(n == 0)
        def _():
            for c in tile_copies(0, 0):
                c.start()

        for c in tile_copies(n, slot):
            c.wait()
        wg16[...] = wstage[slot, 0].astype(BF16)
        wu16[...] = wstage[slot, 1].astype(BF16)

        @pl.when(n + 1 < nb)
        def _():
            for c in tile_copies(n + 1, 1 - slot):
                c.start()

        act(xs_ref, os_ref)

    wd16_ref[...] = wd_ref[...].astype(BF16)
    act(xp_ref, op_ref)


def _gate_up(xp, xs, w_gu, w_down, layer, tm, tn):
    m, k = xp.shape
    s = xs.shape[0]
    f = w_gu.shape[2] // 2
    dm = w_down.shape[2]
    tm = min(tm, m)
    tn = min(tn, f)
    assert m % tm == 0 and f % tn == 0 and w_down.shape[1] == f
    nb = f // tn
    mt = m // tm
    td = f // (nb * mt)
    assert td * nb * mt == f and td % (2 * SUBLANE) == 0
    return pl.pallas_call(
        functools.partial(_gu_kernel, layer=layer, nb=nb),
        grid=(nb, mt),
        in_specs=[pl.BlockSpec((tm, k), lambda n, i: (i, 0)),
                  pl.BlockSpec((s, k), lambda n, i: (0, 0)),
                  pl.BlockSpec(memory_space=pl.ANY),
                  pl.BlockSpec((None, td, dm), lambda n, i: (layer, n * mt + i, 0))],
        out_specs=[pl.BlockSpec((tm, tn), lambda n, i: (i, n)),
                   pl.BlockSpec((s, tn), lambda n, i: (0, n)),
                   pl.BlockSpec((td, dm), lambda n, i: (n * mt + i, 0))],
        out_shape=[jax.ShapeDtypeStruct((m, f), BF16), jax.ShapeDtypeStruct((s, f), BF16),
                   jax.ShapeDtypeStruct((f, dm), BF16)],
        scratch_shapes=[pltpu.VMEM((2, 2, k, tn), F32), pltpu.SemaphoreType.DMA((2, 2)),
                        pltpu.VMEM((k, tn), BF16), pltpu.VMEM((k, tn), BF16)],
        compiler_params=_params("arbitrary", "arbitrary"),
        name="gate_up",
    )(xp, xs, w_gu, w_down)


def _ln_kernel(y_ref, g_ref, b_ref, *o_refs, final):
    y = y_ref[...]
    mu = jnp.mean(y, axis=-1, keepdims=True)
    yc = y - mu
    var = jnp.mean(yc * yc, axis=-1, keepdims=True)
    rstd = lax.rsqrt(var + LN_EPS)
    out = yc * rstd * g_ref[...] + b_ref[...]
    if final:
        o_refs[0][...] = out
    else:
        ob_ref, mu_ref, rstd_ref = o_refs
        ob_ref[...] = out.astype(BF16)
        mu_ref[...] = mu
        rstd_ref[...] = rstd


def _ln(y, g, b, tm, final=False):
    rows, d = y.shape
    tm = min(tm, rows)
    assert rows % tm == 0
    row = pl.BlockSpec((tm, d), lambda i: (i, 0))
    vec = pl.BlockSpec((1, d), lambda i: (0, 0))
    stat = pl.BlockSpec((tm, 1), lambda i: (i, 0))
    if final:
        out_specs, out_shape = row, jax.ShapeDtypeStruct((rows, d), F32)
    else:
        out_specs = [row, stat, stat]
        out_shape = [jax.ShapeDtypeStruct((rows, d), BF16), jax.ShapeDtypeStruct((rows, 1), F32),
                     jax.ShapeDtypeStruct((rows, 1), F32)]
    return pl.pallas_call(
        functools.partial(_ln_kernel, final=final),
        grid=(rows // tm,),
        in_specs=[row, vec, vec],
        out_specs=out_specs,
        out_shape=out_shape,
        compiler_params=_params("arbitrary"),
        name="layer_norm",
    )(y, g.reshape(1, d), b.reshape(1, d))


LOG2_E = 1.4426950408889634


def _sb_tile(z2, mask, u01, carry):
    neg_log_keep = jnp.maximum(z2, 0.0) + jnp.log2(1.0 + jnp.exp2(-jnp.abs(z2)))
    if mask is not None:
        neg_log_keep = jnp.where(mask, neg_log_keep, 0.0)
    right = jnp.dot(neg_log_keep.astype(BF16), u01, preferred_element_type=F32) + carry
    w = jnp.exp2(z2 - neg_log_keep - right)
    if mask is not None:
        w = jnp.where(mask, w, 0.0)
    return w, carry + jnp.sum(neg_log_keep, axis=1, keepdims=True)


def _sb_prompt_kernel(bias_ref, q_ref, k_ref, v_ref, o_ref, *, t, hp, scale):
    g = pl.program_id(1)
    i = pl.program_id(2)
    d = q_ref.shape[1] // hp
    rows = _iota((t, t), 0)
    cols = _iota((t, t), 1)
    u01 = (rows > cols).astype(BF16)
    lanes = [slice(s * d, (s + 1) * d) for s in range(hp)]
    qs = [q_ref[:, ln].astype(BF16) for ln in lanes]
    scale2 = scale * LOG2_E
    biases2 = [bias_ref[g * hp + s] * LOG2_E for s in range(hp)]

    def tiles(j, mask, state):
        keys = pl.ds(pl.multiple_of(j * t, t), t)
        zs = [lax.dot_general(qs[s], k_ref[keys, lanes[s]].astype(BF16), NT_DIMS, preferred_element_type=F32)
              * scale2 + biases2[s] for s in range(hp)]
        ws = [_sb_tile(zs[s], mask, u01, state[s][0]) for s in range(hp)]
        return tuple(
            (ws[s][1], state[s][1] + jnp.dot(ws[s][0].astype(BF16), v_ref[keys, lanes[s]].astype(BF16),
                                             preferred_element_type=F32))
            for s in range(hp))

    state = tuple((jnp.zeros((t, 1), F32), jnp.zeros((t, d), F32)) for _ in range(hp))
    state = tiles(i, cols < rows, state)
    state = lax.fori_loop(0, i, lambda it, st: tiles(i - 1 - it, None, st), state)
    for s in range(hp):
        o_ref[:, lanes[s]] = state[s][1].astype(o_ref.dtype)


def _sb_prompt(q, k, v, sb_bias, nseq, t, hp):
    rows, width = q.shape
    heads = sb_bias.shape[0]
    d = width // heads
    seq = rows // nseq
    t = min(t, seq)
    nq = seq // t
    assert seq % t == 0 and d == LANE and heads % hp == 0
    qspec = pl.BlockSpec((t, hp * d), lambda b, g, i, bias: (b * nq + i, g))
    kvspec = pl.BlockSpec((seq, hp * d), lambda b, g, i, bias: (b, g))
    return pl.pallas_call(
        functools.partial(_sb_prompt_kernel, t=t, hp=hp, scale=d ** -0.5),
        grid_spec=pltpu.PrefetchScalarGridSpec(
            num_scalar_prefetch=1,
            grid=(nseq, heads // hp, nq),
            in_specs=[qspec, kvspec, kvspec],
            out_specs=qspec),
        out_shape=jax.ShapeDtypeStruct((rows, width), BF16),
        compiler_params=_params("arbitrary", "arbitrary", "arbitrary"),
        name="sb_prompt",
    )(sb_bias, q, k, v)


def _sb_decode_kernel(pt_ref, qbd_ref, bias_ref, knew_ref, vnew_ref, kc_ref, vc_ref, o_ref,
                      carry_scr, acc_scr, kbuf, vbuf, sem, *, heads, steps, scale):
    b = pl.program_id(0)
    p = pl.program_id(1)
    nq = qbd_ref.shape[1]
    nq_per_head = nq // heads
    width = qbd_ref.shape[2]
    d = width // heads
    pps = kbuf.shape[1]
    n_pages = (steps - 1) * pps

    def page_copies(step, slot):
        copies = []
        for j in range(pps):
            page = pt_ref[b, n_pages - step * pps + j]
            for h in range(heads):
                copies.append(pltpu.make_async_copy(kc_ref.at[page, :, h, :], kbuf.at[slot, j, h], sem.at[0, slot]))
                copies.append(pltpu.make_async_copy(vc_ref.at[page, :, h, :], vbuf.at[slot, j, h], sem.at[1, slot]))
        return copies

    @pl.when(p == 0)
    def _():
        carry_scr[...] = jnp.zeros_like(carry_scr)
        acc_scr[...] = jnp.zeros_like(acc_scr)

    @pl.when(p + 1 < steps)
    def _():
        for c in page_copies(p + 1, (p + 1) % 2):
            c.start()

    def step(kcat, vcat, mask):
        nk = kcat.shape[0]
        u01 = (_iota((nk, nk), 0) > _iota((nk, nk), 1)).astype(BF16)
        z2 = (lax.dot_general(qbd_ref[0], kcat, NT_DIMS, preferred_element_type=F32) * (scale * LOG2_E)
              + bias_ref[...] * LOG2_E)
        w, carry = _sb_tile(z2, mask, u01, carry_scr[...])
        carry_scr[...] = carry
        acc_scr[...] += jnp.dot(w.astype(BF16), vcat, preferred_element_type=F32)

    @pl.when(p == 0)
    def _():
        qi = _div(_iota((nq, PAGE), 0), heads)
        kj = _iota((nq, PAGE), 1)
        step(knew_ref[0].astype(BF16), vnew_ref[0].astype(BF16), kj < qi)

    @pl.when(p > 0)
    def _():
        slot = p % 2
        for c in page_copies(p, slot):
            c.wait()

        def cat(buf):
            return jnp.concatenate(
                [jnp.concatenate([buf[slot, j, h].astype(BF16) for h in range(heads)], axis=1)
                 for j in range(pps)], axis=0)

        step(cat(kbuf), cat(vbuf), None)

    @pl.when(p == steps - 1)
    def _():
        acc = acc_scr[...]
        keep = _mod(_iota((nq, width), 0), heads) == _div(_iota((nq, width), 1), d)
        sel = jnp.where(keep, acc, 0.0)
        out_row = _iota((o_ref.shape[1], width), 0)
        out = jnp.zeros((o_ref.shape[1], width), F32)
        for t in range(nq_per_head):
            head_sum = jnp.sum(sel[t * heads:(t + 1) * heads], axis=0, keepdims=True)
            out = jnp.where(out_row == t, head_sum, out)
        o_ref[0] = out


def _sb_decode(q, k, v, cache_k, cache_v, page_table, sb_bias):
    nseq, n_pages = page_table.shape
    heads = sb_bias.shape[0]
    width = q.shape[1]
    d = width // heads
    n_new = q.shape[0] // nseq
    nq = n_new * heads
    assert cache_k.shape[1] == PAGE and d == LANE and n_new <= SUBLANE
    q4 = q.reshape(nseq, n_new, heads, 1, d)
    eye = jnp.eye(heads, dtype=F32).reshape(1, 1, heads, heads, 1)
    qbd = (q4 * eye).reshape(nseq, nq, width).astype(BF16)
    bias_col = jnp.tile(sb_bias.astype(F32), n_new).reshape(nq, 1)
    pad = ((0, 0), (0, PAGE - n_new), (0, 0))
    knew = jnp.pad(k.reshape(nseq, n_new, width), pad)
    vnew = jnp.pad(v.reshape(nseq, n_new, width), pad)
    pps = DECODE_PAGES_PER_STEP
    assert n_pages % pps == 0
    steps = n_pages // pps + 1

    seq3 = lambda b, p, pt: (b, 0, 0)
    out = pl.pallas_call(
        functools.partial(_sb_decode_kernel, heads=heads, steps=steps, scale=d ** -0.5),
        grid_spec=pltpu.PrefetchScalarGridSpec(
            num_scalar_prefetch=1,
            grid=(nseq, steps),
            in_specs=[pl.BlockSpec((1, nq, width), seq3),
                      pl.BlockSpec((nq, 1), lambda b, p, pt: (0, 0)),
                      pl.BlockSpec((1, PAGE, width), seq3),
                      pl.BlockSpec((1, PAGE, width), seq3),
                      pl.BlockSpec(memory_space=pl.ANY),
                      pl.BlockSpec(memory_space=pl.ANY)],
            out_specs=pl.BlockSpec((1, SUBLANE, width), seq3),
            scratch_shapes=[pltpu.VMEM((nq, 1), F32), pltpu.VMEM((nq, width), F32),
                            pltpu.VMEM((2, pps, heads, PAGE, d), F32), pltpu.VMEM((2, pps, heads, PAGE, d), F32),
                            pltpu.SemaphoreType.DMA((2, 2))]),
        out_shape=jax.ShapeDtypeStruct((nseq, SUBLANE, width), F32),
        compiler_params=_params("arbitrary", "arbitrary"),
        name="sb_decode",
    )(page_table, qbd, bias_col, knew, vnew, cache_k, cache_v)
    return out[:, :n_new].reshape(nseq * n_new, width).astype(BF16)


def _ssd_kernel(x_ref, b_ref, c_ref, px_ref, pb_ref, pc_ref, wx_ref, wb_ref, wc_ref,
                bx_ref, bb_ref, bc_ref, dt_ref, dtb_ref, alog_ref, dskip_ref, z_ref, nw_ref, h0_ref,
                y_ref, hout_ref, h_scr, ex_scr, eb_scr, ec_scr, *, valid_len, n_chunks, rep, hd, gp):
    c = pl.program_id(2)
    cl = x_ref.shape[0]
    xw = rep * hd
    n = h_scr.shape[1]

    @pl.when(c == 0)
    def _():
        h_scr[...] = h0_ref[0].reshape(gp * xw, n)
        ex_scr[0:SUBLANE, :] = px_ref[0]
        eb_scr[0:SUBLANE, :] = pb_ref[0]
        ec_scr[0:SUBLANE, :] = pc_ref[0]

    def conv_silu(e_scr, raw_ref, w_ref, bias_ref):
        e_scr[SUBLANE:SUBLANE + cl, :] = raw_ref[...]
        acc = bias_ref[...]
        for i in range(CONV_TAPS):
            lo = SUBLANE - (CONV_TAPS - 1) + i
            acc = acc + e_scr[lo:lo + cl, :] * w_ref[i:i + 1, :]
        e_scr[0:SUBLANE, :] = e_scr[cl:cl + SUBLANE, :]
        return _silu(acc)

    xc_all = conv_silu(ex_scr, x_ref, wx_ref, bx_ref)
    bm_all = conv_silu(eb_scr, b_ref, wb_ref, bb_ref)
    cm_all = conv_silu(ec_scr, c_ref, wc_ref, bc_ref)

    row = c * cl + _iota((cl, 1), 0)
    t_idx = _iota((cl, cl), 0)
    s_idx = _iota((cl, cl), 1)
    tril = t_idx >= s_idx
    tril16 = tril.astype(BF16)
    lane = _iota((1, xw), 1)
    in_head = [(lane >= r * hd) & (lane < (r + 1) * hd) for r in range(rep)]
    groups = range(gp)

    xc = [xc_all[:, s * xw:(s + 1) * xw] for s in groups]
    bm16 = [bm_all[:, s * n:(s + 1) * n].astype(BF16) for s in groups]
    cm16 = [cm_all[:, s * n:(s + 1) * n].astype(BF16) for s in groups]
    h = [h_scr[s * xw:(s + 1) * xw, :] for s in groups]

    def step_sizes(s):
        dtr = dt_ref[0, s] + dtb_ref[s]
        return jnp.where(row < valid_len, jnp.maximum(dtr, 0.0) + _softplus_neg_abs(dtr), 0.0)

    dt = [step_sizes(s) for s in groups]
    cum = [_split_dot_left(tril16, dt[s] * -jnp.exp(alog_ref[s]), 3) for s in groups]
    g = [lax.dot_general(cm16[s], bm16[s], NT_DIMS, preferred_element_type=F32) for s in groups]
    inter = [lax.dot_general(cm16[s], h[s].astype(BF16), NT_DIMS, preferred_element_type=F32) for s in groups]
    cum_t = [cum[s].T for s in groups]
    dt_t = [dt[s].T for s in groups]

    def head_terms(s):
        cum_last = cum[s][cl - 1:cl, :]
        decay_last = jnp.exp(cum_last)
        ws, xrs, hdecay = [], [], []
        e_all = jnp.zeros((cl, xw), F32)
        tail_all = jnp.zeros((cl, xw), F32)
        for r in range(rep):
            ccol = cum[s][:, r:r + 1]
            decay = jnp.exp(jnp.where(tril, ccol - cum_t[s][r:r + 1, :], -jnp.inf))
            ws.append((g[s] * decay * dt_t[s][r:r + 1, :]).astype(BF16))
            xrs.append(jnp.where(in_head[r], xc[s], 0.0).astype(BF16))
            e_all = jnp.where(in_head[r], jnp.exp(ccol), e_all)
            tail_all = jnp.where(in_head[r], jnp.exp(cum_last[:, r:r + 1] - ccol) * dt[s][:, r:r + 1], tail_all)
            hdecay.append(jnp.broadcast_to(decay_last[:, r:r + 1], (hd, n)))
        return (jnp.concatenate(ws, axis=1), jnp.concatenate(xrs, axis=0), e_all, tail_all,
                jnp.concatenate(hdecay, axis=0))

    terms = [head_terms(s) for s in groups]
    intra = [jnp.dot(terms[s][0], terms[s][1], preferred_element_type=F32) for s in groups]
    y = [intra[s] + inter[s] * terms[s][2] + xc[s] * dskip_ref[s] for s in groups]

    def transposed(xt):
        return jnp.concatenate([xt[:, j * LANE:(j + 1) * LANE].T for j in range(xw // LANE)], axis=0)

    xt_t = [transposed(xc[s] * terms[s][3]).astype(BF16) for s in groups]
    h_new = [h[s] * terms[s][4] + jnp.dot(xt_t[s], bm16[s], preferred_element_type=F32) for s in groups]
    for s in groups:
        h_scr[s * xw:(s + 1) * xw, :] = h_new[s]

    for s in groups:
        lanes = slice(s * xw, (s + 1) * xw)
        gated = y[s] * _silu(z_ref[:, lanes])
        ms = jnp.mean(gated * gated, axis=-1, keepdims=True)
        y_ref[:, lanes] = (gated * lax.rsqrt(ms + RMS_EPS) * nw_ref[:, lanes]).astype(y_ref.dtype)

    @pl.when(c == n_chunks - 1)
    def _():
        hout_ref[0] = h_scr[...].reshape(hout_ref.shape[1:])


def _ssd(xbc, dt_raw, z, conv_prev, h0, valid_len, conv_w, conv_b, dt_bias, a_log, d_skip, norm_w, groups):
    nseq, heads, hd, n = h0.shape
    rows, ch = xbc.shape
    seq = rows // nseq
    rep = heads // groups
    xw = rep * hd
    ssm_w = heads * hd
    cl = CHUNK
    nc = seq // cl
    gp = SSD_GROUPS_PER_STEP
    assert seq % cl == 0 and n == LANE and xw % LANE == 0 and ch == ssm_w + 2 * groups * n
    assert groups % gp == 0 and ssm_w % (gp * n) == 0
    b_off = ssm_w // (gp * n)
    c_off = b_off + groups // gp

    prev = jnp.pad(conv_prev, ((0, 0), (SUBLANE - (CONV_TAPS - 1), 0), (0, 0)))
    dt_g = dt_raw.reshape(nseq, seq, groups, rep).transpose(0, 2, 1, 3)
    dt_g = jnp.pad(dt_g, ((0, 0), (0, 0), (0, 0), (0, LANE - rep)))

    def lane_pad(p):
        return jnp.pad(p.astype(F32).reshape(groups, 1, rep), ((0, 0), (0, 0), (0, LANE - rep)))

    dskip_rep = jnp.repeat(d_skip.astype(F32), hd).reshape(groups, 1, xw)
    cb = conv_b.reshape(1, ch)

    def rows_at(width, off):
        return pl.BlockSpec((cl, width), lambda b, g, c: (b * nc + c, off + g))

    def prev_at(width, off):
        return pl.BlockSpec((1, SUBLANE, width), lambda b, g, c: (b, 0, off + g))

    def taps_at(nrows, width, off):
        return pl.BlockSpec((nrows, width), lambda b, g, c: (0, off + g))

    par = pl.BlockSpec((gp, 1, LANE), lambda b, g, c: (g, 0, 0))
    state = pl.BlockSpec((1, gp * rep, hd, n), lambda b, g, c: (b, g, 0, 0))
    xs_w, bc_w = gp * xw, gp * n
    y, h = pl.pallas_call(
        functools.partial(_ssd_kernel, valid_len=valid_len, n_chunks=nc, rep=rep, hd=hd, gp=gp),
        grid=(nseq, groups // gp, nc),
        in_specs=[rows_at(xs_w, 0), rows_at(bc_w, b_off), rows_at(bc_w, c_off),
                  prev_at(xs_w, 0), prev_at(bc_w, b_off), prev_at(bc_w, c_off),
                  taps_at(CONV_TAPS, xs_w, 0), taps_at(CONV_TAPS, bc_w, b_off), taps_at(CONV_TAPS, bc_w, c_off),
                  taps_at(1, xs_w, 0), taps_at(1, bc_w, b_off), taps_at(1, bc_w, c_off),
                  pl.BlockSpec((1, gp, cl, LANE), lambda b, g, c: (b, g, c, 0)),
                  par, par,
                  pl.BlockSpec((gp, 1, xw), lambda b, g, c: (g, 0, 0)),
                  rows_at(xs_w, 0),
                  taps_at(1, xs_w, 0),
                  state],
        out_specs=[rows_at(xs_w, 0), state],
        out_shape=[jax.ShapeDtypeStruct((rows, ssm_w), BF16), jax.ShapeDtypeStruct(h0.shape, F32)],
        scratch_shapes=[pltpu.VMEM((gp * xw, n), F32),
                        pltpu.VMEM((cl + SUBLANE, xs_w), F32),
                        pltpu.VMEM((cl + SUBLANE, bc_w), F32),
                        pltpu.VMEM((cl + SUBLANE, bc_w), F32)],
        compiler_params=_params("arbitrary", "arbitrary", "arbitrary"),
        name="ssd",
    )(xbc, xbc, xbc, prev, prev, prev, conv_w, conv_w, conv_w, cb, cb, cb,
      dt_g, lane_pad(dt_bias), lane_pad(a_log), dskip_rep, z, norm_w.reshape(1, ssm_w), h0)
    return y, h


def _pool_kernel(u_ref, prev_ref, w_ref, s_ref, o_ref, ext_scr, *, pos0, gd):
    i = pl.program_id(1)
    tr = u_ref.shape[0]

    @pl.when(i == 0)
    def _():
        ext_scr[0:POOL_HIST, :] = prev_ref[0]

    ext_scr[POOL_HIST:POOL_HIST + tr, :] = u_ref[...]
    pos = (pos0 + i * tr + _iota((tr, 1), 0)).astype(F32)
    for g, win in enumerate(POOL_WINDOWS):
        lanes = slice(g * gd, (g + 1) * gd)
        total = ext_scr[POOL_HIST:POOL_HIST + tr, lanes]
        for j in range(1, win):
            total = total + ext_scr[POOL_HIST - j:POOL_HIST - j + tr, lanes]
        count = jnp.minimum(pos + 1.0, float(win))
        dlt = total / count - u_ref[:, lanes]
        y = jnp.dot(dlt.astype(BF16), w_ref[g], preferred_element_type=F32)
        o_ref[:, lanes] = (y * s_ref[:, lanes]).astype(o_ref.dtype)
    ext_scr[0:POOL_HIST, :] = ext_scr[tr:tr + POOL_HIST, :]


def _pool(u, prev, pos0, w_pool16, pool_scale, tr):
    nseq = prev.shape[0]
    rows, width = u.shape
    seq = rows // nseq
    tr = min(tr, seq)
    nt = seq // tr
    ng, gd, _ = w_pool16.shape
    assert seq % tr == 0 and ng == len(POOL_WINDOWS) and ng * gd == width
    prev16 = jnp.pad(prev, ((0, 0), (POOL_HIST - prev.shape[1], 0), (0, 0)))
    return pl.pallas_call(
        functools.partial(_pool_kernel, pos0=pos0, gd=gd),
        grid=(nseq, nt),
        in_specs=[pl.BlockSpec((tr, width), lambda b, i: (b * nt + i, 0)),
                  pl.BlockSpec((1, POOL_HIST, width), lambda b, i: (b, 0, 0)),
                  pl.BlockSpec((ng, gd, gd), lambda b, i: (0, 0, 0)),
                  pl.BlockSpec((1, width), lambda b, i: (0, 0))],
        out_specs=pl.BlockSpec((tr, width), lambda b, i: (b * nt + i, 0)),
        out_shape=jax.ShapeDtypeStruct((rows, width), BF16),
        scratch_shapes=[pltpu.VMEM((tr + POOL_HIST, width), F32)],
        compiler_params=_params("arbitrary", "arbitrary"),
        name="pool",
    )(u, prev16, w_pool16, pool_scale.reshape(1, width))


def _gla_kernel(q_ref, k_ref, v_ref, r_ref, glr_ref, w2_ref, b2_ref, nw_ref, s0_ref,
                o_ref, sout_ref, s_scr, *, valid_len, n_chunks, scale, hp):
    c = pl.program_id(2)
    cl = q_ref.shape[0]
    dk = s_scr.shape[1]
    dv = s_scr.shape[2]
    heads = range(hp)

    @pl.when(c == 0)
    def _():
        s_scr[...] = s0_ref[0]

    row = c * cl + _iota((cl, 1), 0)
    valid = row < valid_len
    pre = jnp.dot(glr_ref[...], w2_ref[...], preferred_element_type=F32) + b2_ref[...]
    lg_all = (jnp.minimum(pre, 0.0) - _softplus_neg_abs(pre)) / GLA_TAU
    lg_all = jnp.where(valid, lg_all, 0.0)
    qs_all = q_ref[...] * scale
    k_all = jnp.where(valid, k_ref[...], 0.0)
    kl = [slice(h * dk, (h + 1) * dk) for h in heads]
    vl = [slice(h * dv, (h + 1) * dv) for h in heads]
    lg = [lg_all[:, kl[h]] for h in heads]
    qs = [qs_all[:, kl[h]] for h in heads]
    k = [k_all[:, kl[h]] for h in heads]
    v16 = [v_ref[:, vl[h]].astype(BF16) for h in heads]

    t_idx = _iota((cl, cl), 0)
    s_idx = _iota((cl, cl), 1)
    r_idx = _iota((cl, 1), 0)
    def masked_qk(qd, kd, mask):
        return jnp.where(mask, lax.dot_general(qd.astype(BF16), kd.astype(BF16), NT_DIMS,
                                               preferred_element_type=F32), 0.0)

    att = [masked_qk(qs[h], k[h], t_idx == s_idx) for h in heads]
    pre_b = lg
    suf_b = [jnp.zeros_like(lg[h]) for h in heads]
    tot_b = lg
    bsz = 1
    while bsz < cl:
        siblings = ((_div(t_idx, 2 * bsz) == _div(s_idx, 2 * bsz))
                    & (_mod(_div(t_idx, bsz), 2) == 1) & (_mod(_div(s_idx, bsz), 2) == 0))
        att = [att[h] + masked_qk(qs[h] * jnp.exp(pre_b[h]), k[h] * jnp.exp(suf_b[h]), siblings) for h in heads]
        is_right = _mod(_div(r_idx, bsz), 2) == 1
        left_tot = [pltpu.roll(tot_b[h], bsz, axis=0) for h in heads]
        right_tot = [pltpu.roll(tot_b[h], cl - bsz, axis=0) for h in heads]
        pre_b = [pre_b[h] + jnp.where(is_right, left_tot[h], 0.0) for h in heads]
        suf_b = [suf_b[h] + jnp.where(is_right, 0.0, right_tot[h]) for h in heads]
        tot_b = [tot_b[h] + jnp.where(is_right, left_tot[h], right_tot[h]) for h in heads]
        bsz *= 2

    s = [s_scr[h] for h in heads]
    qd = [(qs[h] * jnp.exp(pre_b[h])).astype(BF16) for h in heads]
    kd_t = [(k[h] * jnp.exp(suf_b[h])).T.astype(BF16) for h in heads]
    o = [jnp.dot(att[h].astype(BF16), v16[h], preferred_element_type=F32)
         + jnp.dot(qd[h], s[h].astype(BF16), preferred_element_type=F32) for h in heads]
    chunk_decay = [jnp.exp(tot_b[h].T) for h in heads]
    s_new = [s[h] * jnp.concatenate([chunk_decay[h]] * (dv // cl), axis=1)
             + jnp.dot(kd_t[h], v16[h], preferred_element_type=F32) for h in heads]
    for h in heads:
        s_scr[h] = s_new[h]

    for h in heads:
        ms = jnp.mean(o[h] * o[h], axis=-1, keepdims=True)
        o_ref[:, vl[h]] = (o[h] * lax.rsqrt(ms + RMS_EPS) * nw_ref[:, vl[h]]
                           * _silu(r_ref[:, vl[h]])).astype(o_ref.dtype)

    @pl.when(c == n_chunks - 1)
    def _():
        sout_ref[0] = s_scr[...]


def _gla(q, k, v, r, glr, w2, b2, norm_w, s0, valid_len):
    nseq, heads, dk, dv = s0.shape
    rows = q.shape[0]
    seq = rows // nseq
    cl = CHUNK
    nc = seq // cl
    hp = GLA_HEADS_PER_STEP
    assert seq % cl == 0 and dk == cl and dv % cl == 0 and glr.shape[1] == LANE and heads % hp == 0

    def rows_at(width, col):
        return pl.BlockSpec((cl, width), (lambda b, h, c: (b * nc + c, h)) if col else (lambda b, h, c: (b * nc + c, 0)))

    def head_at(nrows, width):
        return pl.BlockSpec((nrows, width), lambda b, h, c: (0, h))

    state = pl.BlockSpec((1, hp, dk, dv), lambda b, h, c: (b, h, 0, 0))
    kw, vw = hp * dk, hp * dv
    return pl.pallas_call(
        functools.partial(_gla_kernel, valid_len=valid_len, n_chunks=nc, scale=dk ** -0.5, hp=hp),
        grid=(nseq, heads // hp, nc),
        in_specs=[rows_at(kw, True), rows_at(kw, True), rows_at(vw, True), rows_at(vw, True),
                  rows_at(LANE, False), head_at(LANE, kw), head_at(1, kw), head_at(1, vw), state],
        out_specs=[rows_at(vw, True), state],
        out_shape=[jax.ShapeDtypeStruct((rows, heads * dv), BF16), jax.ShapeDtypeStruct(s0.shape, F32)],
        scratch_shapes=[pltpu.VMEM((hp, dk, dv), F32)],
        compiler_params=_params("arbitrary", "arbitrary", "arbitrary"),
        name="gla",
    )(q, k, v, r, glr, w2, b2.reshape(1, heads * dk), norm_w.reshape(1, heads * dv), s0)


def _pad_seq(a, nseq, seq_pad):
    n = a.shape[0] // nseq
    a3 = jnp.pad(a.reshape(nseq, n, a.shape[1]), ((0, 0), (0, seq_pad - n), (0, 0)))
    return a3.reshape(nseq * seq_pad, a.shape[1])


def _unpad_seq(a, nseq, n):
    return a.reshape(nseq, -1, a.shape[1])[:, :n].reshape(nseq * n, a.shape[1])


def _project(xp16, xs16, w, sizes, name, bf16_segments=()):
    outs_p, outs_s = [], []
    off = 0
    w_nk = w.T
    for idx, size in enumerate(sizes):
        dtype = BF16 if idx in bf16_segments else F32
        op, os_ = _proj([xp16], [xs16], w_nk, True, off, size, dtype, PROJ_TM, PROJ_TN, f"{name}_{idx}")
        off += size
        outs_p.append(op)
        outs_s.append(os_)
    return outs_p, outs_s


def kernel(x_prompt, x_sample, cache_k, cache_v, page_table, state_conv, state_ssm, state_pool, state_gla, w_in0, sb_bias, conv_w, conv_b, dt_bias, a_log, d_skip, ssm_norm_w, w_out0, w_in1, w_pool, pool_scale, gla_w2, gla_b2, gla_norm_w, w_out1, ln_g, ln_b, w_gu, w_down):
    bp, seq, dm = x_prompt.shape
    bs, n_new, _ = x_sample.shape
    depth = ln_g.shape[0]
    alpha = (2 * depth) ** 0.25
    sb_heads, sb_d = cache_k.shape[2], cache_k.shape[3]
    sb_w = sb_heads * sb_d
    ssm_heads, ssm_hd, ssm_n = state_ssm.shape[1:]
    ssm_w = ssm_heads * ssm_hd
    conv_ch = state_conv.shape[2]
    groups = (conv_ch - ssm_w) // (2 * ssm_n)
    pool_w = state_pool.shape[2]
    gla_heads, gla_dk, gla_dv = state_gla.shape[1:]
    gla_kw, gla_vw = gla_heads * gla_dk, gla_heads * gla_dv
    gla_rank = gla_w2.shape[0]
    past_len = page_table.shape[1] * cache_k.shape[1]
    seq_pad = CHUNK

    xp = x_prompt.reshape(bp * seq, dm)
    xs = x_sample.reshape(bs * n_new, dm)
    xp16, xs16 = xp.astype(BF16), xs.astype(BF16)

    def lazy_ln(y, sub, layer):
        g, b = ln_g[layer, sub], ln_b[layer, sub]
        x16, mu, rstd = _ln(y, g, b, LN_TM)
        return x16, (y, mu, rstd, g.reshape(1, dm), b.reshape(1, dm))

    def mix_ffn(rp, rs, mixed_p, mixed_s, w_out, layer):
        yp, ys = _proj(mixed_p, mixed_s, w_out, False, 0, dm, F32, PROJ_TM, PROJ_TN, f"out{layer}",
                       res=(rp, rs), alpha=alpha)
        xp16, rp = lazy_ln(yp, 0, layer)
        xs16, rs = lazy_ln(ys, 0, layer)
        hp, hs, w_down16 = _gate_up(xp16, xs16, w_gu, w_down, layer, PROJ_TM, GU_TN)
        return (_mm_res(hp, w_down16, rp, alpha, DOWN_TM, DOWN_TN),
                _mm_res(hs, w_down16, rs, alpha, DOWN_TM, DOWN_TN))

    split0 = (sb_w, sb_w, sb_w, ssm_w, conv_ch, ssm_heads)
    (qp, kp, vp, zp, xbcp, dtp), (qs, ks, vs, zs, xbcs, dts) = _project(xp16, xs16, w_in0, split0, "in0",
                                                                        bf16_segments=(0,))
    dtp, dts = dtp[:, :ssm_heads], dts[:, :ssm_heads]

    oap = _sb_prompt(qp, kp, vp, sb_bias, bp, SB_T, SB_HEADS_PER_STEP)
    oas = _sb_decode(qs, ks, vs, cache_k, cache_v, page_table, sb_bias)

    ssd_w = (conv_w, conv_b, dt_bias, a_log, d_skip, ssm_norm_w, groups)
    yp, ssm_p = _ssd(xbcp, dtp, zp, jnp.zeros((bp, CONV_TAPS - 1, conv_ch), F32),
                     jnp.zeros((bp,) + state_ssm.shape[1:], F32), seq, *ssd_w)
    ys, ssm_s = _ssd(_pad_seq(xbcs, bs, seq_pad), _pad_seq(dts, bs, seq_pad), _pad_seq(zs, bs, seq_pad),
                     state_conv, state_ssm, n_new, *ssd_w)
    ys = _unpad_seq(ys, bs, n_new)

    y1p, y1s = mix_ffn((xp,), (xs,), [oap, yp], [oas, ys], w_out0, 0)
    xp16, res_p = lazy_ln(y1p, 1, 0)
    xs16, res_s = lazy_ln(y1s, 1, 0)

    split1 = (pool_w, gla_kw, gla_kw, gla_vw, gla_vw, gla_rank)
    (up, gqp, gkp, gvp, grp, glrp), (us, gqs, gks, gvs, grs, glrs) = _project(xp16, xs16, w_in1, split1, "in1")

    w_pool16 = w_pool.astype(BF16)
    ocp = _pool(up, jnp.zeros((bp, POOL_HIST - 1, pool_w), F32), 0, w_pool16, pool_scale, POOL_TM)
    us8 = _pad_seq(us, bs, SUBLANE)
    ocs = _unpad_seq(_pool(us8, state_pool, past_len, w_pool16, pool_scale, SUBLANE), bs, n_new)

    w2_16 = jnp.pad(gla_w2, ((0, LANE - gla_rank), (0, 0))).astype(BF16)
    gla_w = (w2_16, gla_b2, gla_norm_w)
    odp, gla_p = _gla(gqp, gkp, gvp, grp, glrp.astype(BF16), *gla_w,
                      jnp.zeros((bp,) + state_gla.shape[1:], F32), seq)
    pad = lambda a: _pad_seq(a, bs, seq_pad)
    ods, gla_s = _gla(pad(gqs), pad(gks), pad(gvs), pad(grs), pad(glrs).astype(BF16), *gla_w, state_gla, n_new)
    ods = _unpad_seq(ods, bs, n_new)

    y2p, y2s = mix_ffn(res_p, res_s, [ocp, odp], [ocs, ods], w_out1, 1)
    xp = _ln(y2p, ln_g[1, 1], ln_b[1, 1], LN_TM, final=True)
    xs = _ln(y2s, ln_g[1, 1], ln_b[1, 1], LN_TM, final=True)

    hd4 = lambda a, nseq: a.reshape(nseq, -1, sb_heads, sb_d)
    conv_p = xbcp.reshape(bp, seq, conv_ch)[:, seq - (CONV_TAPS - 1):]
    conv_s = jnp.concatenate([state_conv, xbcs.reshape(bs, n_new, conv_ch)], axis=1)[:, n_new:]
    pool_p = up.reshape(bp, seq, pool_w)[:, seq - (POOL_HIST - 1):]
    pool_s = jnp.concatenate([state_pool, us.reshape(bs, n_new, pool_w)], axis=1)[:, n_new:]
    return (xp.reshape(bp, seq, dm), xs.reshape(bs, n_new, dm),
            hd4(kp, bp), hd4(vp, bp), hd4(ks, bs), hd4(vs, bs),
            conv_p, conv_s, ssm_p, ssm_s, pool_p, pool_s, gla_p, gla_s)
```

```python
import functools

import jax
import jax.numpy as jnp
from jax import lax
from jax.experimental import pallas as pl
from jax.experimental.pallas import tpu as pltpu

F32 = jnp.float32
BF16 = jnp.bfloat16

LANE = 128
SUBLANE = 8
VMEM_LIMIT = 56 * 1024 * 1024

LN_EPS = 1e-5
RMS_EPS = 1e-5
GLA_TAU = 16.0
PAGE = 128
POOL_WINDOWS = (2, 4, 8, 16)
POOL_HIST = 16
CONV_TAPS = 4
CHUNK = 128

PROJ_TM, PROJ_TN = 1024, 512
GU_TN = 256
DOWN_TM, DOWN_TN = 512, 512
LN_TM = 512
SB_T, SB_HEADS_PER_STEP = 256, 8
POOL_TM = 512
DECODE_PAGES_PER_STEP = 8
SSD_GROUPS_PER_STEP = 4
GLA_HEADS_PER_STEP = 8

NT_DIMS = (((1,), (1,)), ((), ()))


def _params(*sem):
    return pltpu.CompilerParams(dimension_semantics=sem, vmem_limit_bytes=VMEM_LIMIT)


def _silu(x):
    return x * (0.5 * jnp.tanh(0.5 * x) + 0.5)


def _softplus_neg_abs(x):
    return jnp.log1p(jnp.exp(-jnp.abs(x)))


def _iota(shape, dim):
    return lax.broadcasted_iota(jnp.int32, shape, dim)


def _div(x, c):
    return lax.shift_right_logical(x, (c.bit_length() - 1)) if c & (c - 1) == 0 else x // c


def _mod(x, c):
    return (x & (c - 1)) if c & (c - 1) == 0 else x % c


def _bf16_pieces(x, terms):
    pieces = [x.astype(BF16)]
    for _ in range(terms - 1):
        x = x - pieces[-1].astype(F32)
        pieces.append(x.astype(BF16))
    return pieces


def _add_all(terms):
    return functools.reduce(lambda a, b: a + b, terms)


def _split_dot_left(m01, x, terms):
    return _add_all([jnp.dot(m01, p, preferred_element_type=F32) for p in _bf16_pieces(x, terms)])


def _residual_tile(refs):
    if len(refs) == 1:
        return refs[0][...]
    y_ref, mu_ref, rstd_ref, g_ref, b_ref = refs
    return (y_ref[...] - mu_ref[...]) * rstd_ref[...] * g_ref[...] + b_ref[...]


def _residual_specs(res, tm, tn, row_map, col_map):
    tile = pl.BlockSpec((tm, tn), lambda *ids: (row_map(*ids), col_map(*ids)))
    if len(res) == 1:
        return [tile]
    per_row = pl.BlockSpec((tm, 1), lambda *ids: (row_map(*ids), 0))
    per_col = pl.BlockSpec((1, tn), lambda *ids: (0, col_map(*ids)))
    return [tile, per_row, per_row, per_col, per_col]


def _mm_res_kernel(x_ref, w_ref, *refs, alpha):
    o_ref = refs[-1]
    o_ref[...] = alpha * _residual_tile(refs[:-1]) + jnp.dot(x_ref[...], w_ref[...], preferred_element_type=F32)


def _mm_res(x, w, res, alpha, tm, tn):
    m, k = x.shape
    n = w.shape[1]
    tm = min(tm, m)
    tn = min(tn, n)
    assert m % tm == 0 and n % tn == 0
    return pl.pallas_call(
        functools.partial(_mm_res_kernel, alpha=alpha),
        grid=(m // tm, n // tn),
        in_specs=[pl.BlockSpec((tm, k), lambda i, j: (i, 0)),
                  pl.BlockSpec((k, tn), lambda i, j: (0, j))]
                 + _residual_specs(res, tm, tn, lambda i, j: i, lambda i, j: j),
        out_specs=pl.BlockSpec((tm, tn), lambda i, j: (i, j)),
        out_shape=jax.ShapeDtypeStruct((m, n), F32),
        compiler_params=_params("arbitrary", "arbitrary"),
        name="down",
    )(x, w, *res)


def _dot_parts(x_refs, w16, w_is_nk):
    terms, off = [], 0
    for x_ref in x_refs:
        k = x_ref.shape[1]
        if w_is_nk:
            terms.append(lax.dot_general(x_ref[...], w16[:, off:off + k], NT_DIMS, preferred_element_type=F32))
        else:
            terms.append(jnp.dot(x_ref[...], w16[off:off + k, :], preferred_element_type=F32))
        off += k
    return _add_all(terms)


def _proj_kernel(*refs, n_parts, w_is_nk, col0, ncols, nb, alpha):
    xps, xss = refs[:n_parts], refs[n_parts:2 * n_parts]
    w_ref = refs[2 * n_parts]
    res = refs[2 * n_parts + 1:-5]
    res = (res[:len(res) // 2], res[len(res) // 2:])
    op_ref, os_ref, wstage, sem, w16 = refs[-5:]

    def finish(acc, res_refs, o_ref):
        if alpha is not None:
            acc = alpha * _residual_tile(res_refs) + acc
        o_ref[...] = acc.astype(o_ref.dtype)

    n = pl.program_id(0)
    m = pl.program_id(1)
    tn = op_ref.shape[1]
    nv = min(tn, ncols)

    def tile_copy(t, slot):
        cols = pl.ds(pl.multiple_of(col0 + t * tn, nv), nv)
        if w_is_nk:
            return pltpu.make_async_copy(w_ref.at[cols, :], wstage.at[slot, pl.ds(0, nv), :], sem.at[slot])
        return pltpu.make_async_copy(w_ref.at[:, cols], wstage.at[slot, :, pl.ds(0, nv)], sem.at[slot])

    @pl.when(m == 0)
    def _():
        slot = n % 2

        @pl.when(n == 0)
        def _():
            tile_copy(0, 0).start()

        tile_copy(n, slot).wait()
        if nv == tn:
            w16[...] = wstage[slot].astype(BF16)
        elif w_is_nk:
            w16[0:nv, :] = wstage[slot, 0:nv, :].astype(BF16)
            w16[nv:tn, :] = jnp.zeros((tn - nv, w16.shape[1]), BF16)
        else:
            w16[:, 0:nv] = wstage[slot, :, 0:nv].astype(BF16)
            w16[:, nv:tn] = jnp.zeros((w16.shape[0], tn - nv), BF16)

        @pl.when(n + 1 < nb)
        def _():
            tile_copy(n + 1, 1 - slot).start()

        finish(_dot_parts(xss, w16, w_is_nk), res[1], os_ref)

    finish(_dot_parts(xps, w16, w_is_nk), res[0], op_ref)


def _col_tile(col0, ncols, tn_max):
    if ncols < LANE:
        assert col0 % LANE == 0
        return LANE
    tn = tn_max
    while ncols % tn or col0 % tn:
        tn //= 2
    assert tn >= LANE
    return tn


def _proj(xps, xss, w, w_is_nk, col0, ncols, out_dtype, tm, tn_max, name, res=None, alpha=None):
    m = xps[0].shape[0]
    s = xss[0].shape[0]
    k = w.shape[1] if w_is_nk else w.shape[0]
    tm = min(tm, m)
    tn = _col_tile(col0, ncols, tn_max)
    nb = pl.cdiv(ncols, tn)
    assert m % tm == 0 and sum(x.shape[1] for x in xps) == k and min(tn, ncols) % SUBLANE == 0
    tile_shape = (tn, k) if w_is_nk else (k, tn)
    out_tiles = [pl.BlockSpec((tm, tn), lambda n, i: (i, n)), pl.BlockSpec((s, tn), lambda n, i: (0, n))]
    assert (res is None) == (alpha is None)
    res_arrays, res_specs = [], []
    if res is not None:
        res_arrays = [*res[0], *res[1]]
        res_specs = (_residual_specs(res[0], tm, tn, lambda n, i: i, lambda n, i: n)
                     + _residual_specs(res[1], s, tn, lambda n, i: 0, lambda n, i: n))
    return pl.pallas_call(
        functools.partial(_proj_kernel, n_parts=len(xps), w_is_nk=w_is_nk, col0=col0, ncols=ncols, nb=nb,
                          alpha=alpha),
        grid=(nb, m // tm),
        in_specs=([pl.BlockSpec((tm, x.shape[1]), lambda n, i: (i, 0)) for x in xps]
                  + [pl.BlockSpec((s, x.shape[1]), lambda n, i: (0, 0)) for x in xss]
                  + [pl.BlockSpec(memory_space=pl.ANY)]
                  + res_specs),
        out_specs=out_tiles,
        out_shape=[jax.ShapeDtypeStruct((m, nb * tn), out_dtype), jax.ShapeDtypeStruct((s, nb * tn), out_dtype)],
        scratch_shapes=[pltpu.VMEM((2,) + tile_shape, F32), pltpu.SemaphoreType.DMA((2,)),
                        pltpu.VMEM(tile_shape, BF16)],
        compiler_params=_params("arbitrary", "arbitrary"),
        name=name,
    )(*xps, *xss, w, *res_arrays)


def _gu_kernel(xp_ref, xs_ref, wgu_ref, wd_ref, op_ref, os_ref, wd16_ref, wstage, sem, wg16, wu16,
               *, layer, nb):
    n = pl.program_id(0)
    m = pl.program_id(1)
    tn = wg16.shape[1]

    def tile_copies(t, slot):
        cols = [pl.ds(pl.multiple_of((t + half * nb) * tn, tn), tn) for half in range(2)]
        return [pltpu.make_async_copy(wgu_ref.at[layer, :, cols[half]], wstage.at[slot, half], sem.at[slot, half])
                for half in range(2)]

    def act(x_ref, o_ref):
        x = x_ref[...]
        g = jnp.dot(x, wg16[...], preferred_element_type=F32)
        u = jnp.dot(x, wu16[...], preferred_element_type=F32)
        o_ref[...] = (_silu(g) * u).astype(o_ref.dtype)

    @pl.when(m == 0)
    def _():
        slot = n % 2

        @pl.when(n == 0)
        def _():
            for c in tile_copies(0, 0):
                c.start()

        for c in tile_copies(n, slot):
            c.wait()
        wg16[...] = wstage[slot, 0].astype(BF16)
        wu16[...] = wstage[slot, 1].astype(BF16)

        @pl.when(n + 1 < nb)
        def _():
            for c in tile_copies(n + 1, 1 - slot):
                c.start()

        act(xs_ref, os_ref)

    wd16_ref[...] = wd_ref[...].astype(BF16)
    act(xp_ref, op_ref)


def _gate_up(xp, xs, w_gu, w_down, layer, tm, tn):
    m, k = xp.shape
    s = xs.shape[0]
    f = w_gu.shape[2] // 2
    dm = w_down.shape[2]
    tm = min(tm, m)
    tn = min(tn, f)
    assert m % tm == 0 and f % tn == 0 and w_down.shape[1] == f
    nb = f // tn
    mt = m // tm
    td = f // (nb * mt)
    assert td * nb * mt == f and td % (2 * SUBLANE) == 0
    return pl.pallas_call(
        functools.partial(_gu_kernel, layer=layer, nb=nb),
        grid=(nb, mt),
        in_specs=[pl.BlockSpec((tm, k), lambda n, i: (i, 0)),
                  pl.BlockSpec((s, k), lambda n, i: (0, 0)),
                  pl.BlockSpec(memory_space=pl.ANY),
                  pl.BlockSpec((None, td, dm), lambda n, i: (layer, n * mt + i, 0))],
        out_specs=[pl.BlockSpec((tm, tn), lambda n, i: (i, n)),
                   pl.BlockSpec((s, tn), lambda n, i: (0, n)),
                   pl.BlockSpec((td, dm), lambda n, i: (n * mt + i, 0))],
        out_shape=[jax.ShapeDtypeStruct((m, f), BF16), jax.ShapeDtypeStruct((s, f), BF16),
                   jax.ShapeDtypeStruct((f, dm), BF16)],
        scratch_shapes=[pltpu.VMEM((2, 2, k, tn), F32), pltpu.SemaphoreType.DMA((2, 2)),
                        pltpu.VMEM((k, tn), BF16), pltpu.VMEM((k, tn), BF16)],
        compiler_params=_params("arbitrary", "arbitrary"),
        name="gate_up",
    )(xp, xs, w_gu, w_down)


def _ln_kernel(y_ref, g_ref, b_ref, *o_refs, final):
    y = y_ref[...]
    mu = jnp.mean(y, axis=-1, keepdims=True)
    yc = y - mu
    var = jnp.mean(yc * yc, axis=-1, keepdims=True)
    rstd = lax.rsqrt(var + LN_EPS)
    out = yc * rstd * g_ref[...] + b_ref[...]
    if final:
        o_refs[0][...] = out
    else:
        ob_ref, mu_ref, rstd_ref = o_refs
        ob_ref[...] = out.astype(BF16)
        mu_ref[...] = mu
        rstd_ref[...] = rstd


def _ln(y, g, b, tm, final=False):
    rows, d = y.shape
    tm = min(tm, rows)
    assert rows % tm == 0
    row = pl.BlockSpec((tm, d), lambda i: (i, 0))
    vec = pl.BlockSpec((1, d), lambda i: (0, 0))
    stat = pl.BlockSpec((tm, 1), lambda i: (i, 0))
    if final:
        out_specs, out_shape = row, jax.ShapeDtypeStruct((rows, d), F32)
    else:
        out_specs = [row, stat, stat]
        out_shape = [jax.ShapeDtypeStruct((rows, d), BF16), jax.ShapeDtypeStruct((rows, 1), F32),
                     jax.ShapeDtypeStruct((rows, 1), F32)]
    return pl.pallas_call(
        functools.partial(_ln_kernel, final=final),
        grid=(rows // tm,),
        in_specs=[row, vec, vec],
        out_specs=out_specs,
        out_shape=out_shape,
        compiler_params=_params("arbitrary"),
        name="layer_norm",
    )(y, g.reshape(1, d), b.reshape(1, d))


LOG2_E = 1.4426950408889634


def _sb_tile(z2, mask, u01, carry):
    neg_log_keep = jnp.maximum(z2, 0.0) + jnp.log2(1.0 + jnp.exp2(-jnp.abs(z2)))
    if mask is not None:
        neg_log_keep = jnp.where(mask, neg_log_keep, 0.0)
    right = jnp.dot(neg_log_keep.astype(BF16), u01, preferred_element_type=F32) + carry
    w = jnp.exp2(z2 - neg_log_keep - right)
    if mask is not None:
        w = jnp.where(mask, w, 0.0)
    return w, carry + jnp.sum(neg_log_keep, axis=1, keepdims=True)


def _sb_prompt_kernel(bias_ref, q_ref, k_ref, v_ref, o_ref, *, t, hp, scale):
    g = pl.program_id(1)
    i = pl.program_id(2)
    d = q_ref.shape[1] // hp
    rows = _iota((t, t), 0)
    cols = _iota((t, t), 1)
    u01 = (rows > cols).astype(BF16)
    lanes = [slice(s * d, (s + 1) * d) for s in range(hp)]
    qs = [q_ref[:, ln].astype(BF16) for ln in lanes]
    scale2 = scale * LOG2_E
    biases2 = [bias_ref[g * hp + s] * LOG2_E for s in range(hp)]

    def tiles(j, mask, state):
        keys = pl.ds(pl.multiple_of(j * t, t), t)
        zs = [lax.dot_general(qs[s], k_ref[keys, lanes[s]].astype(BF16), NT_DIMS, preferred_element_type=F32)
              * scale2 + biases2[s] for s in range(hp)]
        ws = [_sb_tile(zs[s], mask, u01, state[s][0]) for s in range(hp)]
        return tuple(
            (ws[s][1], state[s][1] + jnp.dot(ws[s][0].astype(BF16), v_ref[keys, lanes[s]].astype(BF16),
                                             preferred_element_type=F32))
            for s in range(hp))

    state = tuple((jnp.zeros((t, 1), F32), jnp.zeros((t, d), F32)) for _ in range(hp))
    state = tiles(i, cols < rows, state)
    state = lax.fori_loop(0, i, lambda it, st: tiles(i - 1 - it, None, st), state)
    for s in range(hp):
        o_ref[:, lanes[s]] = state[s][1].astype(o_ref.dtype)


def _sb_prompt(q, k, v, sb_bias, nseq, t, hp):
    rows, width = q.shape
    heads = sb_bias.shape[0]
    d = width // heads
    seq = rows // nseq
    t = min(t, seq)
    nq = seq // t
    assert seq % t == 0 and d == LANE and heads % hp == 0
    qspec = pl.BlockSpec((t, hp * d), lambda b, g, i, bias: (b * nq + i, g))
    kvspec = pl.BlockSpec((seq, hp * d), lambda b, g, i, bias: (b, g))
    return pl.pallas_call(
        functools.partial(_sb_prompt_kernel, t=t, hp=hp, scale=d ** -0.5),
        grid_spec=pltpu.PrefetchScalarGridSpec(
            num_scalar_prefetch=1,
            grid=(nseq, heads // hp, nq),
            in_specs=[qspec, kvspec, kvspec],
            out_specs=qspec),
        out_shape=jax.ShapeDtypeStruct((rows, width), BF16),
        compiler_params=_params("arbitrary", "arbitrary", "arbitrary"),
        name="sb_prompt",
    )(sb_bias, q, k, v)


def _sb_decode_kernel(pt_ref, qbd_ref, bias_ref, knew_ref, vnew_ref, kc_ref, vc_ref, o_ref,
                      carry_scr, acc_scr, kbuf, vbuf, sem, *, heads, steps, scale):
    b = pl.program_id(0)
    p = pl.program_id(1)
    nq = qbd_ref.shape[1]
    nq_per_head = nq // heads
    width = qbd_ref.shape[2]
    d = width // heads
    pps = kbuf.shape[1]
    n_pages = (steps - 1) * pps

    def page_copies(step, slot):
        copies = []
        for j in range(pps):
            page = pt_ref[b, n_pages - step * pps + j]
            for h in range(heads):
                copies.append(pltpu.make_async_copy(kc_ref.at[page, :, h, :], kbuf.at[slot, j, h], sem.at[0, slot]))
                copies.append(pltpu.make_async_copy(vc_ref.at[page, :, h, :], vbuf.at[slot, j, h], sem.at[1, slot]))
        return copies

    @pl.when(p == 0)
    def _():
        carry_scr[...] = jnp.zeros_like(carry_scr)
        acc_scr[...] = jnp.zeros_like(acc_scr)

    @pl.when(p + 1 < steps)
    def _():
        for c in page_copies(p + 1, (p + 1) % 2):
            c.start()

    def step(kcat, vcat, mask):
        nk = kcat.shape[0]
        u01 = (_iota((nk, nk), 0) > _iota((nk, nk), 1)).astype(BF16)
        z2 = (lax.dot_general(qbd_ref[0], kcat, NT_DIMS, preferred_element_type=F32) * (scale * LOG2_E)
              + bias_ref[...] * LOG2_E)
        w, carry = _sb_tile(z2, mask, u01, carry_scr[...])
        carry_scr[...] = carry
        acc_scr[...] += jnp.dot(w.astype(BF16), vcat, preferred_element_type=F32)

    @pl.when(p == 0)
    def _():
        qi = _div(_iota((nq, PAGE), 0), heads)
        kj = _iota((nq, PAGE), 1)
        step(knew_ref[0].astype(BF16), vnew_ref[0].astype(BF16), kj < qi)

    @pl.when(p > 0)
    def _():
        slot = p % 2
        for c in page_copies(p, slot):
            c.wait()

        def cat(buf):
            return jnp.concatenate(
                [jnp.concatenate([buf[slot, j, h].astype(BF16) for h in range(heads)], axis=1)
                 for j in range(pps)], axis=0)

        step(cat(kbuf), cat(vbuf), None)

    @pl.when(p == steps - 1)
    def _():
        acc = acc_scr[...]
        keep = _mod(_iota((nq, width), 0), heads) == _div(_iota((nq, width), 1), d)
        sel = jnp.where(keep, acc, 0.0)
        out_row = _iota((o_ref.shape[1], width), 0)
        out = jnp.zeros((o_ref.shape[1], width), F32)
        for t in range(nq_per_head):
            head_sum = jnp.sum(sel[t * heads:(t + 1) * heads], axis=0, keepdims=True)
            out = jnp.where(out_row == t, head_sum, out)
        o_ref[0] = out


def _sb_decode(q, k, v, cache_k, cache_v, page_table, sb_bias):
    nseq, n_pages = page_table.shape
    heads = sb_bias.shape[0]
    width = q.shape[1]
    d = width // heads
    n_new = q.shape[0] // nseq
    nq = n_new * heads
    assert cache_k.shape[1] == PAGE and d == LANE and n_new <= SUBLANE
    q4 = q.reshape(nseq, n_new, heads, 1, d)
    eye = jnp.eye(heads, dtype=F32).reshape(1, 1, heads, heads, 1)
    qbd = (q4 * eye).reshape(nseq, nq, width).astype(BF16)
    bias_col = jnp.tile(sb_bias.astype(F32), n_new).reshape(nq, 1)
    pad = ((0, 0), (0, PAGE - n_new), (0, 0))
    knew = jnp.pad(k.reshape(nseq, n_new, width), pad)
    vnew = jnp.pad(v.reshape(nseq, n_new, width), pad)
    pps = DECODE_PAGES_PER_STEP
    assert n_pages % pps == 0
    steps = n_pages // pps + 1

    seq3 = lambda b, p, pt: (b, 0, 0)
    out = pl.pallas_call(
        functools.partial(_sb_decode_kernel, heads=heads, steps=steps, scale=d ** -0.5),
        grid_spec=pltpu.PrefetchScalarGridSpec(
            num_scalar_prefetch=1,
            grid=(nseq, steps),
            in_specs=[pl.BlockSpec((1, nq, width), seq3),
                      pl.BlockSpec((nq, 1), lambda b, p, pt: (0, 0)),
                      pl.BlockSpec((1, PAGE, width), seq3),
                      pl.BlockSpec((1, PAGE, width), seq3),
                      pl.BlockSpec(memory_space=pl.ANY),
                      pl.BlockSpec(memory_space=pl.ANY)],
            out_specs=pl.BlockSpec((1, SUBLANE, width), seq3),
            scratch_shapes=[pltpu.VMEM((nq, 1), F32), pltpu.VMEM((nq, width), F32),
                            pltpu.VMEM((2, pps, heads, PAGE, d), F32), pltpu.VMEM((2, pps, heads, PAGE, d), F32),
                            pltpu.SemaphoreType.DMA((2, 2))]),
        out_shape=jax.ShapeDtypeStruct((nseq, SUBLANE, width), F32),
        compiler_params=_params("arbitrary", "arbitrary"),
        name="sb_decode",
    )(page_table, qbd, bias_col, knew, vnew, cache_k, cache_v)
    return out[:, :n_new].reshape(nseq * n_new, width).astype(BF16)


def _ssd_kernel(x_ref, b_ref, c_ref, px_ref, pb_ref, pc_ref, wx_ref, wb_ref, wc_ref,
                bx_ref, bb_ref, bc_ref, dt_ref, dtb_ref, alog_ref, dskip_ref, z_ref, nw_ref, h0_ref,
                y_ref, hout_ref, h_scr, ex_scr, eb_scr, ec_scr, *, valid_len, n_chunks, rep, hd, gp):
    c = pl.program_id(2)
    cl = x_ref.shape[0]
    xw = rep * hd
    n = h_scr.shape[1]

    @pl.when(c == 0)
    def _():
        h_scr[...] = h0_ref[0].reshape(gp * xw, n)
        ex_scr[0:SUBLANE, :] = px_ref[0]
        eb_scr[0:SUBLANE, :] = pb_ref[0]
        ec_scr[0:SUBLANE, :] = pc_ref[0]

    def conv_silu(e_scr, raw_ref, w_ref, bias_ref):
        e_scr[SUBLANE:SUBLANE + cl, :] = raw_ref[...]
        acc = bias_ref[...]
        for i in range(CONV_TAPS):
            lo = SUBLANE - (CONV_TAPS - 1) + i
            acc = acc + e_scr[lo:lo + cl, :] * w_ref[i:i + 1, :]
        e_scr[0:SUBLANE, :] = e_scr[cl:cl + SUBLANE, :]
        return _silu(acc)

    xc_all = conv_silu(ex_scr, x_ref, wx_ref, bx_ref)
    bm_all = conv_silu(eb_scr, b_ref, wb_ref, bb_ref)
    cm_all = conv_silu(ec_scr, c_ref, wc_ref, bc_ref)

    row = c * cl + _iota((cl, 1), 0)
    t_idx = _iota((cl, cl), 0)
    s_idx = _iota((cl, cl), 1)
    tril = t_idx >= s_idx
    tril16 = tril.astype(BF16)
    lane = _iota((1, xw), 1)
    in_head = [(lane >= r * hd) & (lane < (r + 1) * hd) for r in range(rep)]
    groups = range(gp)

    xc = [xc_all[:, s * xw:(s + 1) * xw] for s in groups]
    bm16 = [bm_all[:, s * n:(s + 1) * n].astype(BF16) for s in groups]
    cm16 = [cm_all[:, s * n:(s + 1) * n].astype(BF16) for s in groups]
    h = [h_scr[s * xw:(s + 1) * xw, :] for s in groups]

    def step_sizes(s):
        dtr = dt_ref[0, s] + dtb_ref[s]
        return jnp.where(row < valid_len, jnp.maximum(dtr, 0.0) + _softplus_neg_abs(dtr), 0.0)

    dt = [step_sizes(s) for s in groups]
    cum = [_split_dot_left(tril16, dt[s] * -jnp.exp(alog_ref[s]), 3) for s in groups]
    g = [lax.dot_general(cm16[s], bm16[s], NT_DIMS, preferred_element_type=F32) for s in groups]
    inter = [lax.dot_general(cm16[s], h[s].astype(BF16), NT_DIMS, preferred_element_type=F32) for s in groups]
    cum_t = [cum[s].T for s in groups]
    dt_t = [dt[s].T for s in groups]

    def head_terms(s):
        cum_last = cum[s][cl - 1:cl, :]
        decay_last = jnp.exp(cum_last)
        ws, xrs, hdecay = [], [], []
        e_all = jnp.zeros((cl, xw), F32)
        tail_all = jnp.zeros((cl, xw), F32)
        for r in range(rep):
            ccol = cum[s][:, r:r + 1]
            decay = jnp.exp(jnp.where(tril, ccol - cum_t[s][r:r + 1, :], -jnp.inf))
            ws.append((g[s] * decay * dt_t[s][r:r + 1, :]).astype(BF16))
            xrs.append(jnp.where(in_head[r], xc[s], 0.0).astype(BF16))
            e_all = jnp.where(in_head[r], jnp.exp(ccol), e_all)
            tail_all = jnp.where(in_head[r], jnp.exp(cum_last[:, r:r + 1] - ccol) * dt[s][:, r:r + 1], tail_all)
            hdecay.append(jnp.broadcast_to(decay_last[:, r:r + 1], (hd, n)))
        return (jnp.concatenate(ws, axis=1), jnp.concatenate(xrs, axis=0), e_all, tail_all,
                jnp.concatenate(hdecay, axis=0))

    terms = [head_terms(s) for s in groups]
    intra = [jnp.dot(terms[s][0], terms[s][1], preferred_element_type=F32) for s in groups]
    y = [intra[s] + inter[s] * terms[s][2] + xc[s] * dskip_ref[s] for s in groups]

    def transposed(xt):
        return jnp.concatenate([xt[:, j * LANE:(j + 1) * LANE].T for j in range(xw // LANE)], axis=0)

    xt_t = [transposed(xc[s] * terms[s][3]).astype(BF16) for s in groups]
    h_new = [h[s] * terms[s][4] + jnp.dot(xt_t[s], bm16[s], preferred_element_type=F32) for s in groups]
    for s in groups:
        h_scr[s * xw:(s + 1) * xw, :] = h_new[s]

    for s in groups:
        lanes = slice(s * xw, (s + 1) * xw)
        gated = y[s] * _silu(z_ref[:, lanes])
        ms = jnp.mean(gated * gated, axis=-1, keepdims=True)
        y_ref[:, lanes] = (gated * lax.rsqrt(ms + RMS_EPS) * nw_ref[:, lanes]).astype(y_ref.dtype)

    @pl.when(c == n_chunks - 1)
    def _():
        hout_ref[0] = h_scr[...].reshape(hout_ref.shape[1:])


def _ssd(xbc, dt_raw, z, conv_prev, h0, valid_len, conv_w, conv_b, dt_bias, a_log, d_skip, norm_w, groups):
    nseq, heads, hd, n = h0.shape
    rows, ch = xbc.shape
    seq = rows // nseq
    rep = heads // groups
    xw = rep * hd
    ssm_w = heads * hd
    cl = CHUNK
    nc = seq // cl
    gp = SSD_GROUPS_PER_STEP
    assert seq % cl == 0 and n == LANE and xw % LANE == 0 and ch == ssm_w + 2 * groups * n
    assert groups % gp == 0 and ssm_w % (gp * n) == 0
    b_off = ssm_w // (gp * n)
    c_off = b_off + groups // gp

    prev = jnp.pad(conv_prev, ((0, 0), (SUBLANE - (CONV_TAPS - 1), 0), (0, 0)))
    dt_g = dt_raw.reshape(nseq, seq, groups, rep).transpose(0, 2, 1, 3)
    dt_g = jnp.pad(dt_g, ((0, 0), (0, 0), (0, 0), (0, LANE - rep)))

    def lane_pad(p):
        return jnp.pad(p.astype(F32).reshape(groups, 1, rep), ((0, 0), (0, 0), (0, LANE - rep)))

    dskip_rep = jnp.repeat(d_skip.astype(F32), hd).reshape(groups, 1, xw)
    cb = conv_b.reshape(1, ch)

    def rows_at(width, off):
        return pl.BlockSpec((cl, width), lambda b, g, c: (b * nc + c, off + g))

    def prev_at(width, off):
        return pl.BlockSpec((1, SUBLANE, width), lambda b, g, c: (b, 0, off + g))

    def taps_at(nrows, width, off):
        return pl.BlockSpec((nrows, width), lambda b, g, c: (0, off + g))

    par = pl.BlockSpec((gp, 1, LANE), lambda b, g, c: (g, 0, 0))
    state = pl.BlockSpec((1, gp * rep, hd, n), lambda b, g, c: (b, g, 0, 0))
    xs_w, bc_w = gp * xw, gp * n
    y, h = pl.pallas_call(
        functools.partial(_ssd_kernel, valid_len=valid_len, n_chunks=nc, rep=rep, hd=hd, gp=gp),
        grid=(nseq, groups // gp, nc),
        in_specs=[rows_at(xs_w, 0), rows_at(bc_w, b_off), rows_at(bc_w, c_off),
                  prev_at(xs_w, 0), prev_at(bc_w, b_off), prev_at(bc_w, c_off),
                  taps_at(CONV_TAPS, xs_w, 0), taps_at(CONV_TAPS, bc_w, b_off), taps_at(CONV_TAPS, bc_w, c_off),
                  taps_at(1, xs_w, 0), taps_at(1, bc_w, b_off), taps_at(1, bc_w, c_off),
                  pl.BlockSpec((1, gp, cl, LANE), lambda b, g, c: (b, g, c, 0)),
                  par, par,
                  pl.BlockSpec((gp, 1, xw), lambda b, g, c: (g, 0, 0)),
                  rows_at(xs_w, 0),
                  taps_at(1, xs_w, 0),
                  state],
        out_specs=[rows_at(xs_w, 0), state],
        out_shape=[jax.ShapeDtypeStruct((rows, ssm_w), BF16), jax.ShapeDtypeStruct(h0.shape, F32)],
        scratch_shapes=[pltpu.VMEM((gp * xw, n), F32),
                        pltpu.VMEM((cl + SUBLANE, xs_w), F32),
                        pltpu.VMEM((cl + SUBLANE, bc_w), F32),
                        pltpu.VMEM((cl + SUBLANE, bc_w), F32)],
        compiler_params=_params("arbitrary", "arbitrary", "arbitrary"),
        name="ssd",
    )(xbc, xbc, xbc, prev, prev, prev, conv_w, conv_w, conv_w, cb, cb, cb,
      dt_g, lane_pad(dt_bias), lane_pad(a_log), dskip_rep, z, norm_w.reshape(1, ssm_w), h0)
    return y, h


def _pool_kernel(u_ref, prev_ref, w_ref, s_ref, o_ref, ext_scr, *, pos0, gd):
    i = pl.program_id(1)
    tr = u_ref.shape[0]

    @pl.when(i == 0)
    def _():
        ext_scr[0:POOL_HIST, :] = prev_ref[0]

    ext_scr[POOL_HIST:POOL_HIST + tr, :] = u_ref[...]
    pos = (pos0 + i * tr + _iota((tr, 1), 0)).astype(F32)
    for g, win in enumerate(POOL_WINDOWS):
        lanes = slice(g * gd, (g + 1) * gd)
        total = ext_scr[POOL_HIST:POOL_HIST + tr, lanes]
        for j in range(1, win):
            total = total + ext_scr[POOL_HIST - j:POOL_HIST - j + tr, lanes]
        count = jnp.minimum(pos + 1.0, float(win))
        dlt = total / count - u_ref[:, lanes]
        y = jnp.dot(dlt.astype(BF16), w_ref[g], preferred_element_type=F32)
        o_ref[:, lanes] = (y * s_ref[:, lanes]).astype(o_ref.dtype)
    ext_scr[0:POOL_HIST, :] = ext_scr[tr:tr + POOL_HIST, :]


def _pool(u, prev, pos0, w_pool16, pool_scale, tr):
    nseq = prev.shape[0]
    rows, width = u.shape
    seq = rows // nseq
    tr = min(tr, seq)
    nt = seq // tr
    ng, gd, _ = w_pool16.shape
    assert seq % tr == 0 and ng == len(POOL_WINDOWS) and ng * gd == width
    prev16 = jnp.pad(prev, ((0, 0), (POOL_HIST - prev.shape[1], 0), (0, 0)))
    return pl.pallas_call(
        functools.partial(_pool_kernel, pos0=pos0, gd=gd),
        grid=(nseq, nt),
        in_specs=[pl.BlockSpec((tr, width), lambda b, i: (b * nt + i, 0)),
                  pl.BlockSpec((1, POOL_HIST, width), lambda b, i: (b, 0, 0)),
                  pl.BlockSpec((ng, gd, gd), lambda b, i: (0, 0, 0)),
                  pl.BlockSpec((1, width), lambda b, i: (0, 0))],
        out_specs=pl.BlockSpec((tr, width), lambda b, i: (b * nt + i, 0)),
        out_shape=jax.ShapeDtypeStruct((rows, width), BF16),
        scratch_shapes=[pltpu.VMEM((tr + POOL_HIST, width), F32)],
        compiler_params=_params("arbitrary", "arbitrary"),
        name="pool",
    )(u, prev16, w_pool16, pool_scale.reshape(1, width))


def _gla_kernel(q_ref, k_ref, v_ref, r_ref, glr_ref, w2_ref, b2_ref, nw_ref, s0_ref,
                o_ref, sout_ref, s_scr, *, valid_len, n_chunks, scale, hp):
    c = pl.program_id(2)
    cl = q_ref.shape[0]
    dk = s_scr.shape[1]
    dv = s_scr.shape[2]
    heads = range(hp)

    @pl.when(c == 0)
    def _():
        s_scr[...] = s0_ref[0]

    row = c * cl + _iota((cl, 1), 0)
    valid = row < valid_len
    pre = jnp.dot(glr_ref[...], w2_ref[...], preferred_element_type=F32) + b2_ref[...]
    lg_all = (jnp.minimum(pre, 0.0) - _softplus_neg_abs(pre)) / GLA_TAU
    lg_all = jnp.where(valid, lg_all, 0.0)
    qs_all = q_ref[...] * scale
    k_all = jnp.where(valid, k_ref[...], 0.0)
    kl = [slice(h * dk, (h + 1) * dk) for h in heads]
    vl = [slice(h * dv, (h + 1) * dv) for h in heads]
    lg = [lg_all[:, kl[h]] for h in heads]
    qs = [qs_all[:, kl[h]] for h in heads]
    k = [k_all[:, kl[h]] for h in heads]
    v16 = [v_ref[:, vl[h]].astype(BF16) for h in heads]

    t_idx = _iota((cl, cl), 0)
    s_idx = _iota((cl, cl), 1)
    r_idx = _iota((cl, 1), 0)
    def masked_qk(qd, kd, mask):
        return jnp.where(mask, lax.dot_general(qd.astype(BF16), kd.astype(BF16), NT_DIMS,
                                               preferred_element_type=F32), 0.0)

    att = [masked_qk(qs[h], k[h], t_idx == s_idx) for h in heads]
    pre_b = lg
    suf_b = [jnp.zeros_like(lg[h]) for h in heads]
    tot_b = lg
    bsz = 1
    while bsz < cl:
        siblings = ((_div(t_idx, 2 * bsz) == _div(s_idx, 2 * bsz))
                    & (_mod(_div(t_idx, bsz), 2) == 1) & (_mod(_div(s_idx, bsz), 2) == 0))
        att = [att[h] + masked_qk(qs[h] * jnp.exp(pre_b[h]), k[h] * jnp.exp(suf_b[h]), siblings) for h in heads]
        is_right = _mod(_div(r_idx, bsz), 2) == 1
        left_tot = [pltpu.roll(tot_b[h], bsz, axis=0) for h in heads]
        right_tot = [pltpu.roll(tot_b[h], cl - bsz, axis=0) for h in heads]
        pre_b = [pre_b[h] + jnp.where(is_right, left_tot[h], 0.0) for h in heads]
        suf_b = [suf_b[h] + jnp.where(is_right, 0.0, right_tot[h]) for h in heads]
        tot_b = [tot_b[h] + jnp.where(is_right, left_tot[h], right_tot[h]) for h in heads]
        bsz *= 2

    s = [s_scr[h] for h in heads]
    qd = [(qs[h] * jnp.exp(pre_b[h])).astype(BF16) for h in heads]
    kd_t = [(k[h] * jnp.exp(suf_b[h])).T.astype(BF16) for h in heads]
    o = [jnp.dot(att[h].astype(BF16), v16[h], preferred_element_type=F32)
         + jnp.dot(qd[h], s[h].astype(BF16), preferred_element_type=F32) for h in heads]
    chunk_decay = [jnp.exp(tot_b[h].T) for h in heads]
    s_new = [s[h] * jnp.concatenate([chunk_decay[h]] * (dv // cl), axis=1)
             + jnp.dot(kd_t[h], v16[h], preferred_element_type=F32) for h in heads]
    for h in heads:
        s_scr[h] = s_new[h]

    for h in heads:
        ms = jnp.mean(o[h] * o[h], axis=-1, keepdims=True)
        o_ref[:, vl[h]] = (o[h] * lax.rsqrt(ms + RMS_EPS) * nw_ref[:, vl[h]]
                           * _silu(r_ref[:, vl[h]])).astype(o_ref.dtype)

    @pl.when(c == n_chunks - 1)
    def _():
        sout_ref[0] = s_scr[...]


def _gla(q, k, v, r, glr, w2, b2, norm_w, s0, valid_len):
    nseq, heads, dk, dv = s0.shape
    rows = q.shape[0]
    seq = rows // nseq
    cl = CHUNK
    nc = seq // cl
    hp = GLA_HEADS_PER_STEP
    assert seq % cl == 0 and dk == cl and dv % cl == 0 and glr.shape[1] == LANE and heads % hp == 0

    def rows_at(width, col):
        return pl.BlockSpec((cl, width), (lambda b, h, c: (b * nc + c, h)) if col else (lambda b, h, c: (b * nc + c, 0)))

    def head_at(nrows, width):
        return pl.BlockSpec((nrows, width), lambda b, h, c: (0, h))

    state = pl.BlockSpec((1, hp, dk, dv), lambda b, h, c: (b, h, 0, 0))
    kw, vw = hp * dk, hp * dv
    return pl.pallas_call(
        functools.partial(_gla_kernel, valid_len=valid_len, n_chunks=nc, scale=dk ** -0.5, hp=hp),
        grid=(nseq, heads // hp, nc),
        in_specs=[rows_at(kw, True), rows_at(kw, True), rows_at(vw, True), rows_at(vw, True),
                  rows_at(LANE, False), head_at(LANE, kw), head_at(1, kw), head_at(1, vw), state],
        out_specs=[rows_at(vw, True), state],
        out_shape=[jax.ShapeDtypeStruct((rows, heads * dv), BF16), jax.ShapeDtypeStruct(s0.shape, F32)],
        scratch_shapes=[pltpu.VMEM((hp, dk, dv), F32)],
        compiler_params=_params("arbitrary", "arbitrary", "arbitrary"),
        name="gla",
    )(q, k, v, r, glr, w2, b2.reshape(1, heads * dk), norm_w.reshape(1, heads * dv), s0)


def _pad_seq(a, nseq, seq_pad):
    n = a.shape[0] // nseq
    a3 = jnp.pad(a.reshape(nseq, n, a.shape[1]), ((0, 0), (0, seq_pad - n), (0, 0)))
    return a3.reshape(nseq * seq_pad, a.shape[1])


def _unpad_seq(a, nseq, n):
    return a.reshape(nseq, -1, a.shape[1])[:, :n].reshape(nseq * n, a.shape[1])


def _project(xp16, xs16, w, sizes, name, bf16_segments=(), last=()):
    w_nk = w.T
    offs = [sum(sizes[:idx]) for idx in range(len(sizes))]
    outs = {}
    for idx in [i for i in range(len(sizes)) if i not in last] + list(last):
        dtype = BF16 if idx in bf16_segments else F32
        outs[idx] = _proj([xp16], [xs16], w_nk, True, offs[idx], sizes[idx], dtype, PROJ_TM, PROJ_TN,
                          f"{name}_{idx}")
    return [outs[i][0] for i in range(len(sizes))], [outs[i][1] for i in range(len(sizes))]


def kernel(x_prompt, x_sample, cache_k, cache_v, page_table, state_conv, state_ssm, state_pool, state_gla, w_in0, sb_bias, conv_w, conv_b, dt_bias, a_log, d_skip, ssm_norm_w, w_out0, w_in1, w_pool, pool_scale, gla_w2, gla_b2, gla_norm_w, w_out1, ln_g, ln_b, w_gu, w_down):
    bp, seq, dm = x_prompt.shape
    bs, n_new, _ = x_sample.shape
    depth = ln_g.shape[0]
    alpha = (2 * depth) ** 0.25
    sb_heads, sb_d = cache_k.shape[2], cache_k.shape[3]
    sb_w = sb_heads * sb_d
    ssm_heads, ssm_hd, ssm_n = state_ssm.shape[1:]
    ssm_w = ssm_heads * ssm_hd
    conv_ch = state_conv.shape[2]
    groups = (conv_ch - ssm_w) // (2 * ssm_n)
    pool_w = state_pool.shape[2]
    gla_heads, gla_dk, gla_dv = state_gla.shape[1:]
    gla_kw, gla_vw = gla_heads * gla_dk, gla_heads * gla_dv
    gla_rank = gla_w2.shape[0]
    past_len = page_table.shape[1] * cache_k.shape[1]
    seq_pad = CHUNK

    xp = x_prompt.reshape(bp * seq, dm)
    xs = x_sample.reshape(bs * n_new, dm)
    xp16, xs16 = xp.astype(BF16), xs.astype(BF16)

    def lazy_ln(y, sub, layer):
        g, b = ln_g[layer, sub], ln_b[layer, sub]
        x16, mu, rstd = _ln(y, g, b, LN_TM)
        return x16, (y, mu, rstd, g.reshape(1, dm), b.reshape(1, dm))

    def mix_ffn(rp, rs, mixed_p, mixed_s, w_out, layer):
        yp, ys = _proj(mixed_p, mixed_s, w_out, False, 0, dm, F32, PROJ_TM, PROJ_TN, f"out{layer}",
                       res=(rp, rs), alpha=alpha)
        xp16, rp = lazy_ln(yp, 0, layer)
        xs16, rs = lazy_ln(ys, 0, layer)
        hp, hs, w_down16 = _gate_up(xp16, xs16, w_gu, w_down, layer, PROJ_TM, GU_TN)
        return (_mm_res(hp, w_down16, rp, alpha, DOWN_TM, DOWN_TN),
                _mm_res(hs, w_down16, rs, alpha, DOWN_TM, DOWN_TN))

    split0 = (sb_w, sb_w, sb_w, ssm_w, conv_ch, ssm_heads)
    (qp, kp, vp, zp, xbcp, dtp), (qs, ks, vs, zs, xbcs, dts) = _project(xp16, xs16, w_in0, split0, "in0",
                                                                        bf16_segments=(0,), last=(1, 2))
    dtp, dts = dtp[:, :ssm_heads], dts[:, :ssm_heads]

    oap = _sb_prompt(qp, kp, vp, sb_bias, bp, SB_T, SB_HEADS_PER_STEP)
    oas = _sb_decode(qs, ks, vs, cache_k, cache_v, page_table, sb_bias)

    ssd_w = (conv_w, conv_b, dt_bias, a_log, d_skip, ssm_norm_w, groups)
    yp, ssm_p = _ssd(xbcp, dtp, zp, jnp.zeros((bp, CONV_TAPS - 1, conv_ch), F32),
                     jnp.zeros((bp,) + state_ssm.shape[1:], F32), seq, *ssd_w)
    ys, ssm_s = _ssd(_pad_seq(xbcs, bs, seq_pad), _pad_seq(dts, bs, seq_pad), _pad_seq(zs, bs, seq_pad),
                     state_conv, state_ssm, n_new, *ssd_w)
    ys = _unpad_seq(ys, bs, n_new)

    y1p, y1s = mix_ffn((xp,), (xs,), [oap, yp], [oas, ys], w_out0, 0)
    xp16, res_p = lazy_ln(y1p, 1, 0)
    xs16, res_s = lazy_ln(y1s, 1, 0)

    split1 = (pool_w, gla_kw, gla_kw, gla_vw, gla_vw, gla_rank)
    (up, gqp, gkp, gvp, grp, glrp), (us, gqs, gks, gvs, grs, glrs) = _project(xp16, xs16, w_in1, split1, "in1")

    w_pool16 = w_pool.astype(BF16)
    ocp = _pool(up, jnp.zeros((bp, POOL_HIST - 1, pool_w), F32), 0, w_pool16, pool_scale, POOL_TM)
    us8 = _pad_seq(us, bs, SUBLANE)
    ocs = _unpad_seq(_pool(us8, state_pool, past_len, w_pool16, pool_scale, SUBLANE), bs, n_new)

    w2_16 = jnp.pad(gla_w2, ((0, LANE - gla_rank), (0, 0))).astype(BF16)
    gla_w = (w2_16, gla_b2, gla_norm_w)
    odp, gla_p = _gla(gqp, gkp, gvp, grp, glrp.astype(BF16), *gla_w,
                      jnp.zeros((bp,) + state_gla.shape[1:], F32), seq)
    pad = lambda a: _pad_seq(a, bs, seq_pad)
    ods, gla_s = _gla(pad(gqs), pad(gks), pad(gvs), pad(grs), pad(glrs).astype(BF16), *gla_w, state_gla, n_new)
    ods = _unpad_seq(ods, bs, n_new)

    y2p, y2s = mix_ffn(res_p, res_s, [ocp, odp], [ocs, ods], w_out1, 1)
    xp = _ln(y2p, ln_g[1, 1], ln_b[1, 1], LN_TM, final=True)
    xs = _ln(y2s, ln_g[1, 1], ln_b[1, 1], LN_TM, final=True)

    hd4 = lambda a, nseq: a.reshape(nseq, -1, sb_heads, sb_d)
    conv_p = xbcp.reshape(bp, seq, conv_ch)[:, seq - (CONV_TAPS - 1):]
    conv_s = jnp.concatenate([state_conv, xbcs.reshape(bs, n_new, conv_ch)], axis=1)[:, n_new:]
    pool_p = up.reshape(bp, seq, pool_w)[:, seq - (POOL_HIST - 1):]
    pool_s = jnp.concatenate([state_pool, us.reshape(bs, n_new, pool_w)], axis=1)[:, n_new:]
    return (xp.reshape(bp, seq, dm), xs.reshape(bs, n_new, dm),
            hd4(kp, bp), hd4(vp, bp), hd4(ks, bs), hd4(vs, bs),
            conv_p, conv_s, ssm_p, ssm_s, pool_p, pool_s, gla_p, gla_s)
```

```python
import functools

import jax
import jax.numpy as jnp
from jax import lax
from jax.experimental import pallas as pl
from jax.experimental.pallas import tpu as pltpu

F32 = jnp.float32
BF16 = jnp.bfloat16

LANE = 128
SUBLANE = 8
VMEM_LIMIT = 56 * 1024 * 1024

LN_EPS = 1e-5
RMS_EPS = 1e-5
GLA_TAU = 16.0
PAGE = 128
POOL_WINDOWS = (2, 4, 8, 16)
POOL_HIST = 16
CONV_TAPS = 4
CHUNK = 128

PROJ_TM, PROJ_TN = 1024, 512
GU_TN = 256
DOWN_TM, DOWN_TN = 512, 512
LN_TM = 512
SB_T, SB_HEADS_PER_STEP = 256, 8
POOL_TM = 512
DECODE_PAGES_PER_STEP = 8
SSD_GROUPS_PER_STEP = 4
GLA_HEADS_PER_STEP = 8

NT_DIMS = (((1,), (1,)), ((), ()))


def _params(*sem):
    return pltpu.CompilerParams(dimension_semantics=sem, vmem_limit_bytes=VMEM_LIMIT)


def _silu(x):
    return x * (0.5 * jnp.tanh(0.5 * x) + 0.5)


def _softplus_neg_abs(x):
    return jnp.log1p(jnp.exp(-jnp.abs(x)))


def _iota(shape, dim):
    return lax.broadcasted_iota(jnp.int32, shape, dim)


def _div(x, c):
    return lax.shift_right_logical(x, (c.bit_length() - 1)) if c & (c - 1) == 0 else x // c


def _mod(x, c):
    return (x & (c - 1)) if c & (c - 1) == 0 else x % c


def _bf16_pieces(x, terms):
    pieces = [x.astype(BF16)]
    for _ in range(terms - 1):
        x = x - pieces[-1].astype(F32)
        pieces.append(x.astype(BF16))
    return pieces


def _add_all(terms):
    return functools.reduce(lambda a, b: a + b, terms)


def _split_dot_left(m01, x, terms):
    return _add_all([jnp.dot(m01, p, preferred_element_type=F32) for p in _bf16_pieces(x, terms)])


def _residual_tile(refs):
    if len(refs) == 1:
        return refs[0][...]
    y_ref, mu_ref, rstd_ref, g_ref, b_ref = refs
    return (y_ref[...] - mu_ref[...]) * rstd_ref[...] * g_ref[...] + b_ref[...]


def _residual_specs(res, tm, tn, row_map, col_map):
    tile = pl.BlockSpec((tm, tn), lambda *ids: (row_map(*ids), col_map(*ids)))
    if len(res) == 1:
        return [tile]
    per_row = pl.BlockSpec((tm, 1), lambda *ids: (row_map(*ids), 0))
    per_col = pl.BlockSpec((1, tn), lambda *ids: (0, col_map(*ids)))
    return [tile, per_row, per_row, per_col, per_col]


def _mm_res_kernel(x_ref, w_ref, *refs, alpha):
    o_ref = refs[-1]
    o_ref[...] = alpha * _residual_tile(refs[:-1]) + jnp.dot(x_ref[...], w_ref[...], preferred_element_type=F32)


def _mm_res(x, w, res, alpha, tm, tn):
    m, k = x.shape
    n = w.shape[1]
    tm = min(tm, m)
    tn = min(tn, n)
    assert m % tm == 0 and n % tn == 0
    return pl.pallas_call(
        functools.partial(_mm_res_kernel, alpha=alpha),
        grid=(m // tm, n // tn),
        in_specs=[pl.BlockSpec((tm, k), lambda i, j: (i, 0)),
                  pl.BlockSpec((k, tn), lambda i, j: (0, j))]
                 + _residual_specs(res, tm, tn, lambda i, j: i, lambda i, j: j),
        out_specs=pl.BlockSpec((tm, tn), lambda i, j: (i, j)),
        out_shape=jax.ShapeDtypeStruct((m, n), F32),
        compiler_params=_params("arbitrary", "arbitrary"),
        name="down",
    )(x, w, *res)


def _dot_parts(x_refs, w16, w_is_nk):
    terms, off = [], 0
    for x_ref in x_refs:
        k = x_ref.shape[1]
        if w_is_nk:
            terms.append(lax.dot_general(x_ref[...], w16[:, off:off + k], NT_DIMS, preferred_element_type=F32))
        else:
            terms.append(jnp.dot(x_ref[...], w16[off:off + k, :], preferred_element_type=F32))
        off += k
    return _add_all(terms)


def _proj_kernel(*refs, n_parts, w_is_nk, col0, ncols, nb, alpha):
    xps, xss = refs[:n_parts], refs[n_parts:2 * n_parts]
    w_ref = refs[2 * n_parts]
    res = refs[2 * n_parts + 1:-5]
    res = (res[:len(res) // 2], res[len(res) // 2:])
    op_ref, os_ref, wstage, sem, w16 = refs[-5:]

    def finish(acc, res_refs, o_ref):
        if alpha is not None:
            acc = alpha * _residual_tile(res_refs) + acc
        o_ref[...] = acc.astype(o_ref.dtype)

    n = pl.program_id(0)
    m = pl.program_id(1)
    tn = op_ref.shape[1]
    nv = min(tn, ncols)

    def tile_copy(t, slot):
        cols = pl.ds(pl.multiple_of(col0 + t * tn, nv), nv)
        if w_is_nk:
            return pltpu.make_async_copy(w_ref.at[cols, :], wstage.at[slot, pl.ds(0, nv), :], sem.at[slot])
        return pltpu.make_async_copy(w_ref.at[:, cols], wstage.at[slot, :, pl.ds(0, nv)], sem.at[slot])

    @pl.when(m == 0)
    def _():
        slot = n % 2

        @pl.when(n == 0)
        def _():
            tile_copy(0, 0).start()

        tile_copy(n, slot).wait()
        if nv == tn:
            w16[...] = wstage[slot].astype(BF16)
        elif w_is_nk:
            w16[0:nv, :] = wstage[slot, 0:nv, :].astype(BF16)
            w16[nv:tn, :] = jnp.zeros((tn - nv, w16.shape[1]), BF16)
        else:
            w16[:, 0:nv] = wstage[slot, :, 0:nv].astype(BF16)
            w16[:, nv:tn] = jnp.zeros((w16.shape[0], tn - nv), BF16)

        @pl.when(n + 1 < nb)
        def _():
            tile_copy(n + 1, 1 - slot).start()

        finish(_dot_parts(xss, w16, w_is_nk), res[1], os_ref)

    finish(_dot_parts(xps, w16, w_is_nk), res[0], op_ref)


def _col_tile(col0, ncols, tn_max):
    if ncols < LANE:
        assert col0 % LANE == 0
        return LANE
    tn = tn_max
    while ncols % tn or col0 % tn:
        tn //= 2
    assert tn >= LANE
    return tn


def _proj(xps, xss, w, w_is_nk, col0, ncols, out_dtype, tm, tn_max, name, res=None, alpha=None):
    m = xps[0].shape[0]
    s = xss[0].shape[0]
    k = w.shape[1] if w_is_nk else w.shape[0]
    tm = min(tm, m)
    tn = _col_tile(col0, ncols, tn_max)
    nb = pl.cdiv(ncols, tn)
    assert m % tm == 0 and sum(x.shape[1] for x in xps) == k and min(tn, ncols) % SUBLANE == 0
    tile_shape = (tn, k) if w_is_nk else (k, tn)
    out_tiles = [pl.BlockSpec((tm, tn), lambda n, i: (i, n)), pl.BlockSpec((s, tn), lambda n, i: (0, n))]
    assert (res is None) == (alpha is None)
    res_arrays, res_specs = [], []
    if res is not None:
        res_arrays = [*res[0], *res[1]]
        res_specs = (_residual_specs(res[0], tm, tn, lambda n, i: i, lambda n, i: n)
                     + _residual_specs(res[1], s, tn, lambda n, i: 0, lambda n, i: n))
    return pl.pallas_call(
        functools.partial(_proj_kernel, n_parts=len(xps), w_is_nk=w_is_nk, col0=col0, ncols=ncols, nb=nb,
                          alpha=alpha),
        grid=(nb, m // tm),
        in_specs=([pl.BlockSpec((tm, x.shape[1]), lambda n, i: (i, 0)) for x in xps]
                  + [pl.BlockSpec((s, x.shape[1]), lambda n, i: (0, 0)) for x in xss]
                  + [pl.BlockSpec(memory_space=pl.ANY)]
                  + res_specs),
        out_specs=out_tiles,
        out_shape=[jax.ShapeDtypeStruct((m, nb * tn), out_dtype), jax.ShapeDtypeStruct((s, nb * tn), out_dtype)],
        scratch_shapes=[pltpu.VMEM((2,) + tile_shape, F32), pltpu.SemaphoreType.DMA((2,)),
                        pltpu.VMEM(tile_shape, BF16)],
        compiler_params=_params("arbitrary", "arbitrary"),
        name=name,
    )(*xps, *xss, w, *res_arrays)


def _gu_kernel(xp_ref, xs_ref, wgu_ref, wd_ref, op_ref, os_ref, wd16_ref, wstage, sem, wg16, wu16,
               *, layer, nb):
    n = pl.program_id(0)
    m = pl.program_id(1)
    tn = wg16.shape[1]

    def tile_copies(t, slot):
        cols = [pl.ds(pl.multiple_of((t + half * nb) * tn, tn), tn) for half in range(2)]
        return [pltpu.make_async_copy(wgu_ref.at[layer, :, cols[half]], wstage.at[slot, half], sem.at[slot, half])
                for half in range(2)]

    def act(x_ref, o_ref):
        x = x_ref[...]
        g = jnp.dot(x, wg16[...], preferred_element_type=F32)
        u = jnp.dot(x, wu16[...], preferred_element_type=F32)
        o_ref[...] = (_silu(g) * u).astype(o_ref.dtype)

    @pl.when(m == 0)
    def _():
        slot = n % 2

        @pl.when(n == 0)
        def _():
            for c in tile_copies(0, 0):
                c.start()

        for c in tile_copies(n, slot):
            c.wait()
        wg16[...] = wstage[slot, 0].astype(BF16)
        wu16[...] = wstage[slot, 1].astype(BF16)

        @pl.when(n + 1 < nb)
        def _():
            for c in tile_copies(n + 1, 1 - slot):
                c.start()

        act(xs_ref, os_ref)

    wd16_ref[...] = wd_ref[...].astype(BF16)
    act(xp_ref, op_ref)


def _gate_up(xp, xs, w_gu, w_down, layer, tm, tn):
    m, k = xp.shape
    s = xs.shape[0]
    f = w_gu.shape[2] // 2
    dm = w_down.shape[2]
    tm = min(tm, m)
    tn = min(tn, f)
    assert m % tm == 0 and f % tn == 0 and w_down.shape[1] == f
    nb = f // tn
    mt = m // tm
    td = f // (nb * mt)
    assert td * nb * mt == f and td % (2 * SUBLANE) == 0
    return pl.pallas_call(
        functools.partial(_gu_kernel, layer=layer, nb=nb),
        grid=(nb, mt),
        in_specs=[pl.BlockSpec((tm, k), lambda n, i: (i, 0)),
                  pl.BlockSpec((s, k), lambda n, i: (0, 0)),
                  pl.BlockSpec(memory_space=pl.ANY),
                  pl.BlockSpec((None, td, dm), lambda n, i: (layer, n * mt + i, 0))],
        out_specs=[pl.BlockSpec((tm, tn), lambda n, i: (i, n)),
                   pl.BlockSpec((s, tn), lambda n, i: (0, n)),
                   pl.BlockSpec((td, dm), lambda n, i: (n * mt + i, 0))],
        out_shape=[jax.ShapeDtypeStruct((m, f), BF16), jax.ShapeDtypeStruct((s, f), BF16),
                   jax.ShapeDtypeStruct((f, dm), BF16)],
        scratch_shapes=[pltpu.VMEM((2, 2, k, tn), F32), pltpu.SemaphoreType.DMA((2, 2)),
                        pltpu.VMEM((k, tn), BF16), pltpu.VMEM((k, tn), BF16)],
        compiler_params=_params("arbitrary", "arbitrary"),
        name="gate_up",
    )(xp, xs, w_gu, w_down)


def _ln_kernel(y_ref, g_ref, b_ref, *o_refs, final):
    y = y_ref[...]
    mu = jnp.mean(y, axis=-1, keepdims=True)
    yc = y - mu
    var = jnp.mean(yc * yc, axis=-1, keepdims=True)
    rstd = lax.rsqrt(var + LN_EPS)
    out = yc * rstd * g_ref[...] + b_ref[...]
    if final:
        o_refs[0][...] = out
    else:
        ob_ref, mu_ref, rstd_ref = o_refs
        ob_ref[...] = out.astype(BF16)
        mu_ref[...] = mu
        rstd_ref[...] = rstd


def _ln(y, g, b, tm, final=False):
    rows, d = y.shape
    tm = min(tm, rows)
    assert rows % tm == 0
    row = pl.BlockSpec((tm, d), lambda i: (i, 0))
    vec = pl.BlockSpec((1, d), lambda i: (0, 0))
    stat = pl.BlockSpec((tm, 1), lambda i: (i, 0))
    if final:
        out_specs, out_shape = row, jax.ShapeDtypeStruct((rows, d), F32)
    else:
        out_specs = [row, stat, stat]
        out_shape = [jax.ShapeDtypeStruct((rows, d), BF16), jax.ShapeDtypeStruct((rows, 1), F32),
                     jax.ShapeDtypeStruct((rows, 1), F32)]
    return pl.pallas_call(
        functools.partial(_ln_kernel, final=final),
        grid=(rows // tm,),
        in_specs=[row, vec, vec],
        out_specs=out_specs,
        out_shape=out_shape,
        compiler_params=_params("arbitrary"),
        name="layer_norm",
    )(y, g.reshape(1, d), b.reshape(1, d))


LOG2_E = 1.4426950408889634


def _sb_tile(z2, mask, u01, carry):
    neg_log_keep = jnp.maximum(z2, 0.0) + jnp.log2(1.0 + jnp.exp2(-jnp.abs(z2)))
    if mask is not None:
        neg_log_keep = jnp.where(mask, neg_log_keep, 0.0)
    right = jnp.dot(neg_log_keep.astype(BF16), u01, preferred_element_type=F32) + carry
    w = jnp.exp2(z2 - neg_log_keep - right)
    if mask is not None:
        w = jnp.where(mask, w, 0.0)
    return w, carry + jnp.sum(neg_log_keep, axis=1, keepdims=True)


def _sb_prompt_kernel(bias_ref, q_ref, k_ref, v_ref, o_ref, *, t, hp, scale):
    g = pl.program_id(1)
    i = pl.program_id(2)
    d = q_ref.shape[1] // hp
    rows = _iota((t, t), 0)
    cols = _iota((t, t), 1)
    u01 = (rows > cols).astype(BF16)
    lanes = [slice(s * d, (s + 1) * d) for s in range(hp)]
    qs = [q_ref[:, ln].astype(BF16) for ln in lanes]
    scale2 = scale * LOG2_E
    biases2 = [bias_ref[g * hp + s] * LOG2_E for s in range(hp)]

    def tiles(j, mask, state):
        keys = pl.ds(pl.multiple_of(j * t, t), t)
        zs = [lax.dot_general(qs[s], k_ref[keys, lanes[s]].astype(BF16), NT_DIMS, preferred_element_type=F32)
              * scale2 + biases2[s] for s in range(hp)]
        ws = [_sb_tile(zs[s], mask, u01, state[s][0]) for s in range(hp)]
        return tuple(
            (ws[s][1], state[s][1] + jnp.dot(ws[s][0].astype(BF16), v_ref[keys, lanes[s]].astype(BF16),
                                             preferred_element_type=F32))
            for s in range(hp))

    state = tuple((jnp.zeros((t, 1), F32), jnp.zeros((t, d), F32)) for _ in range(hp))
    state = tiles(i, cols < rows, state)
    state = lax.fori_loop(0, i, lambda it, st: tiles(i - 1 - it, None, st), state)
    for s in range(hp):
        o_ref[:, lanes[s]] = state[s][1].astype(o_ref.dtype)


def _sb_prompt(q, k, v, sb_bias, nseq, t, hp):
    rows, width = q.shape
    heads = sb_bias.shape[0]
    d = width // heads
    seq = rows // nseq
    t = min(t, seq)
    nq = seq // t
    assert seq % t == 0 and d == LANE and heads % hp == 0
    qspec = pl.BlockSpec((t, hp * d), lambda b, g, i, bias: (b * nq + i, g))
    kvspec = pl.BlockSpec((seq, hp * d), lambda b, g, i, bias: (b, g))
    return pl.pallas_call(
        functools.partial(_sb_prompt_kernel, t=t, hp=hp, scale=d ** -0.5),
        grid_spec=pltpu.PrefetchScalarGridSpec(
            num_scalar_prefetch=1,
            grid=(nseq, heads // hp, nq),
            in_specs=[qspec, kvspec, kvspec],
            out_specs=qspec),
        out_shape=jax.ShapeDtypeStruct((rows, width), BF16),
        compiler_params=_params("arbitrary", "arbitrary", "arbitrary"),
        name="sb_prompt",
    )(sb_bias, q, k, v)


def _sb_decode_kernel(pt_ref, qbd_ref, bias_ref, knew_ref, vnew_ref, kc_ref, vc_ref, o_ref,
                      carry_scr, acc_scr, kbuf, vbuf, sem, *, heads, steps, scale):
    b = pl.program_id(0)
    p = pl.program_id(1)
    nq = qbd_ref.shape[1]
    nq_per_head = nq // heads
    width = qbd_ref.shape[2]
    d = width // heads
    pps = kbuf.shape[1]
    n_pages = (steps - 1) * pps

    def page_copies(step, slot):
        copies = []
        for j in range(pps):
            page = pt_ref[b, n_pages - step * pps + j]
            for h in range(heads):
                copies.append(pltpu.make_async_copy(kc_ref.at[page, :, h, :], kbuf.at[slot, j, h], sem.at[0, slot]))
                copies.append(pltpu.make_async_copy(vc_ref.at[page, :, h, :], vbuf.at[slot, j, h], sem.at[1, slot]))
        return copies

    @pl.when(p == 0)
    def _():
        carry_scr[...] = jnp.zeros_like(carry_scr)
        acc_scr[...] = jnp.zeros_like(acc_scr)

    @pl.when(p + 1 < steps)
    def _():
        for idx, c in enumerate(page_copies(p + 1, (p + 1) % 2)):
            c.start(priority=idx % 2)

    def step(kcat, vcat, mask):
        nk = kcat.shape[0]
        u01 = (_iota((nk, nk), 0) > _iota((nk, nk), 1)).astype(BF16)
        z2 = (lax.dot_general(qbd_ref[0], kcat, NT_DIMS, preferred_element_type=F32) * (scale * LOG2_E)
              + bias_ref[...] * LOG2_E)
        w, carry = _sb_tile(z2, mask, u01, carry_scr[...])
        carry_scr[...] = carry
        acc_scr[...] += jnp.dot(w.astype(BF16), vcat, preferred_element_type=F32)

    @pl.when(p == 0)
    def _():
        qi = _div(_iota((nq, PAGE), 0), heads)
        kj = _iota((nq, PAGE), 1)
        step(knew_ref[0].astype(BF16), vnew_ref[0].astype(BF16), kj < qi)

    @pl.when(p > 0)
    def _():
        slot = p % 2
        for c in page_copies(p, slot):
            c.wait()

        def cat(buf):
            return jnp.concatenate(
                [jnp.concatenate([buf[slot, j, h].astype(BF16) for h in range(heads)], axis=1)
                 for j in range(pps)], axis=0)

        step(cat(kbuf), cat(vbuf), None)

    @pl.when(p == steps - 1)
    def _():
        acc = acc_scr[...]
        keep = _mod(_iota((nq, width), 0), heads) == _div(_iota((nq, width), 1), d)
        sel = jnp.where(keep, acc, 0.0)
        out_row = _iota((o_ref.shape[1], width), 0)
        out = jnp.zeros((o_ref.shape[1], width), F32)
        for t in range(nq_per_head):
            head_sum = jnp.sum(sel[t * heads:(t + 1) * heads], axis=0, keepdims=True)
            out = jnp.where(out_row == t, head_sum, out)
        o_ref[0] = out


def _sb_decode(q, k, v, cache_k, cache_v, page_table, sb_bias):
    nseq, n_pages = page_table.shape
    heads = sb_bias.shape[0]
    width = q.shape[1]
    d = width // heads
    n_new = q.shape[0] // nseq
    nq = n_new * heads
    assert cache_k.shape[1] == PAGE and d == LANE and n_new <= SUBLANE
    q4 = q.reshape(nseq, n_new, heads, 1, d)
    eye = jnp.eye(heads, dtype=F32).reshape(1, 1, heads, heads, 1)
    qbd = (q4 * eye).reshape(nseq, nq, width).astype(BF16)
    bias_col = jnp.tile(sb_bias.astype(F32), n_new).reshape(nq, 1)
    pad = ((0, 0), (0, PAGE - n_new), (0, 0))
    knew = jnp.pad(k.reshape(nseq, n_new, width), pad)
    vnew = jnp.pad(v.reshape(nseq, n_new, width), pad)
    pps = DECODE_PAGES_PER_STEP
    assert n_pages % pps == 0
    steps = n_pages // pps + 1

    seq3 = lambda b, p, pt: (b, 0, 0)
    out = pl.pallas_call(
        functools.partial(_sb_decode_kernel, heads=heads, steps=steps, scale=d ** -0.5),
        grid_spec=pltpu.PrefetchScalarGridSpec(
            num_scalar_prefetch=1,
            grid=(nseq, steps),
            in_specs=[pl.BlockSpec((1, nq, width), seq3),
                      pl.BlockSpec((nq, 1), lambda b, p, pt: (0, 0)),
                      pl.BlockSpec((1, PAGE, width), seq3),
                      pl.BlockSpec((1, PAGE, width), seq3),
                      pl.BlockSpec(memory_space=pl.ANY),
                      pl.BlockSpec(memory_space=pl.ANY)],
            out_specs=pl.BlockSpec((1, SUBLANE, width), seq3),
            scratch_shapes=[pltpu.VMEM((nq, 1), F32), pltpu.VMEM((nq, width), F32),
                            pltpu.VMEM((2, pps, heads, PAGE, d), F32), pltpu.VMEM((2, pps, heads, PAGE, d), F32),
                            pltpu.SemaphoreType.DMA((2, 2))]),
        out_shape=jax.ShapeDtypeStruct((nseq, SUBLANE, width), F32),
        compiler_params=_params("arbitrary", "arbitrary"),
        name="sb_decode",
    )(page_table, qbd, bias_col, knew, vnew, cache_k, cache_v)
    return out[:, :n_new].reshape(nseq * n_new, width).astype(BF16)


def _ssd_kernel(x_ref, b_ref, c_ref, px_ref, pb_ref, pc_ref, wx_ref, wb_ref, wc_ref,
                bx_ref, bb_ref, bc_ref, dt_ref, dtb_ref, alog_ref, dskip_ref, z_ref, nw_ref, h0_ref,
                y_ref, hout_ref, h_scr, ex_scr, eb_scr, ec_scr, *, valid_len, n_chunks, rep, hd, gp):
    c = pl.program_id(2)
    cl = x_ref.shape[0]
    xw = rep * hd
    n = h_scr.shape[1]

    @pl.when(c == 0)
    def _():
        h_scr[...] = h0_ref[0].reshape(gp * xw, n)
        ex_scr[0:SUBLANE, :] = px_ref[0]
        eb_scr[0:SUBLANE, :] = pb_ref[0]
        ec_scr[0:SUBLANE, :] = pc_ref[0]

    def conv_silu(e_scr, raw_ref, w_ref, bias_ref):
        e_scr[SUBLANE:SUBLANE + cl, :] = raw_ref[...]
        acc = bias_ref[...]
        for i in range(CONV_TAPS):
            lo = SUBLANE - (CONV_TAPS - 1) + i
            acc = acc + e_scr[lo:lo + cl, :] * w_ref[i:i + 1, :]
        e_scr[0:SUBLANE, :] = e_scr[cl:cl + SUBLANE, :]
        return _silu(acc)

    xc_all = conv_silu(ex_scr, x_ref, wx_ref, bx_ref)
    bm_all = conv_silu(eb_scr, b_ref, wb_ref, bb_ref)
    cm_all = conv_silu(ec_scr, c_ref, wc_ref, bc_ref)

    row = c * cl + _iota((cl, 1), 0)
    t_idx = _iota((cl, cl), 0)
    s_idx = _iota((cl, cl), 1)
    tril = t_idx >= s_idx
    tril16 = tril.astype(BF16)
    lane = _iota((1, xw), 1)
    in_head = [(lane >= r * hd) & (lane < (r + 1) * hd) for r in range(rep)]
    groups = range(gp)

    xc = [xc_all[:, s * xw:(s + 1) * xw] for s in groups]
    bm16 = [bm_all[:, s * n:(s + 1) * n].astype(BF16) for s in groups]
    cm16 = [cm_all[:, s * n:(s + 1) * n].astype(BF16) for s in groups]
    h = [h_scr[s * xw:(s + 1) * xw, :] for s in groups]

    def step_sizes(s):
        dtr = dt_ref[0, s] + dtb_ref[s]
        return jnp.where(row < valid_len, jnp.maximum(dtr, 0.0) + _softplus_neg_abs(dtr), 0.0)

    dt = [step_sizes(s) for s in groups]
    cum = [_split_dot_left(tril16, dt[s] * -jnp.exp(alog_ref[s]), 3) for s in groups]
    g = [lax.dot_general(cm16[s], bm16[s], NT_DIMS, preferred_element_type=F32) for s in groups]
    inter = [lax.dot_general(cm16[s], h[s].astype(BF16), NT_DIMS, preferred_element_type=F32) for s in groups]
    cum_t = [cum[s].T for s in groups]
    dt_t = [dt[s].T for s in groups]

    def head_terms(s):
        cum_last = cum[s][cl - 1:cl, :]
        decay_last = jnp.exp(cum_last)
        ws, xrs, hdecay = [], [], []
        e_all = jnp.zeros((cl, xw), F32)
        tail_all = jnp.zeros((cl, xw), F32)
        for r in range(rep):
            ccol = cum[s][:, r:r + 1]
            decay = jnp.exp(jnp.where(tril, ccol - cum_t[s][r:r + 1, :], -jnp.inf))
            ws.append((g[s] * decay * dt_t[s][r:r + 1, :]).astype(BF16))
            xrs.append(jnp.where(in_head[r], xc[s], 0.0).astype(BF16))
            e_all = jnp.where(in_head[r], jnp.exp(ccol), e_all)
            tail_all = jnp.where(in_head[r], jnp.exp(cum_last[:, r:r + 1] - ccol) * dt[s][:, r:r + 1], tail_all)
            hdecay.append(jnp.broadcast_to(decay_last[:, r:r + 1], (hd, n)))
        return (jnp.concatenate(ws, axis=1), jnp.concatenate(xrs, axis=0), e_all, tail_all,
                jnp.concatenate(hdecay, axis=0))

    terms = [head_terms(s) for s in groups]
    intra = [jnp.dot(terms[s][0], terms[s][1], preferred_element_type=F32) for s in groups]
    y = [intra[s] + inter[s] * terms[s][2] + xc[s] * dskip_ref[s] for s in groups]

    def transposed(xt):
        return jnp.concatenate([xt[:, j * LANE:(j + 1) * LANE].T for j in range(xw // LANE)], axis=0)

    xt_t = [transposed(xc[s] * terms[s][3]).astype(BF16) for s in groups]
    h_new = [h[s] * terms[s][4] + jnp.dot(xt_t[s], bm16[s], preferred_element_type=F32) for s in groups]
    for s in groups:
        h_scr[s * xw:(s + 1) * xw, :] = h_new[s]

    for s in groups:
        lanes = slice(s * xw, (s + 1) * xw)
        gated = y[s] * _silu(z_ref[:, lanes])
        ms = jnp.mean(gated * gated, axis=-1, keepdims=True)
        y_ref[:, lanes] = (gated * lax.rsqrt(ms + RMS_EPS) * nw_ref[:, lanes]).astype(y_ref.dtype)

    @pl.when(c == n_chunks - 1)
    def _():
        hout_ref[0] = h_scr[...].reshape(hout_ref.shape[1:])


def _ssd(xbc, dt_raw, z, conv_prev, h0, valid_len, conv_w, conv_b, dt_bias, a_log, d_skip, norm_w, groups):
    nseq, heads, hd, n = h0.shape
    rows, ch = xbc.shape
    seq = rows // nseq
    rep = heads // groups
    xw = rep * hd
    ssm_w = heads * hd
    cl = CHUNK
    nc = seq // cl
    gp = SSD_GROUPS_PER_STEP
    assert seq % cl == 0 and n == LANE and xw % LANE == 0 and ch == ssm_w + 2 * groups * n
    assert groups % gp == 0 and ssm_w % (gp * n) == 0
    b_off = ssm_w // (gp * n)
    c_off = b_off + groups // gp

    prev = jnp.pad(conv_prev, ((0, 0), (SUBLANE - (CONV_TAPS - 1), 0), (0, 0)))
    dt_g = dt_raw.reshape(nseq, seq, groups, rep).transpose(0, 2, 1, 3)
    dt_g = jnp.pad(dt_g, ((0, 0), (0, 0), (0, 0), (0, LANE - rep)))

    def lane_pad(p):
        return jnp.pad(p.astype(F32).reshape(groups, 1, rep), ((0, 0), (0, 0), (0, LANE - rep)))

    dskip_rep = jnp.repeat(d_skip.astype(F32), hd).reshape(groups, 1, xw)
    cb = conv_b.reshape(1, ch)

    def rows_at(width, off):
        return pl.BlockSpec((cl, width), lambda b, g, c: (b * nc + c, off + g))

    def prev_at(width, off):
        return pl.BlockSpec((1, SUBLANE, width), lambda b, g, c: (b, 0, off + g))

    def taps_at(nrows, width, off):
        return pl.BlockSpec((nrows, width), lambda b, g, c: (0, off + g))

    par = pl.BlockSpec((gp, 1, LANE), lambda b, g, c: (g, 0, 0))
    state = pl.BlockSpec((1, gp * rep, hd, n), lambda b, g, c: (b, g, 0, 0))
    xs_w, bc_w = gp * xw, gp * n
    y, h = pl.pallas_call(
        functools.partial(_ssd_kernel, valid_len=valid_len, n_chunks=nc, rep=rep, hd=hd, gp=gp),
        grid=(nseq, groups // gp, nc),
        in_specs=[rows_at(xs_w, 0), rows_at(bc_w, b_off), rows_at(bc_w, c_off),
                  prev_at(xs_w, 0), prev_at(bc_w, b_off), prev_at(bc_w, c_off),
                  taps_at(CONV_TAPS, xs_w, 0), taps_at(CONV_TAPS, bc_w, b_off), taps_at(CONV_TAPS, bc_w, c_off),
                  taps_at(1, xs_w, 0), taps_at(1, bc_w, b_off), taps_at(1, bc_w, c_off),
                  pl.BlockSpec((1, gp, cl, LANE), lambda b, g, c: (b, g, c, 0)),
                  par, par,
                  pl.BlockSpec((gp, 1, xw), lambda b, g, c: (g, 0, 0)),
                  rows_at(xs_w, 0),
                  taps_at(1, xs_w, 0),
                  state],
        out_specs=[rows_at(xs_w, 0), state],
        out_shape=[jax.ShapeDtypeStruct((rows, ssm_w), BF16), jax.ShapeDtypeStruct(h0.shape, F32)],
        scratch_shapes=[pltpu.VMEM((gp * xw, n), F32),
                        pltpu.VMEM((cl + SUBLANE, xs_w), F32),
                        pltpu.VMEM((cl + SUBLANE, bc_w), F32),
                        pltpu.VMEM((cl + SUBLANE, bc_w), F32)],
        compiler_params=_params("arbitrary", "arbitrary", "arbitrary"),
        name="ssd",
    )(xbc, xbc, xbc, prev, prev, prev, conv_w, conv_w, conv_w, cb, cb, cb,
      dt_g, lane_pad(dt_bias), lane_pad(a_log), dskip_rep, z, norm_w.reshape(1, ssm_w), h0)
    return y, h


def _pool_kernel(u_ref, prev_ref, w_ref, s_ref, o_ref, ext_scr, *, pos0, gd):
    i = pl.program_id(1)
    tr = u_ref.shape[0]

    @pl.when(i == 0)
    def _():
        ext_scr[0:POOL_HIST, :] = prev_ref[0]

    ext_scr[POOL_HIST:POOL_HIST + tr, :] = u_ref[...]
    pos = (pos0 + i * tr + _iota((tr, 1), 0)).astype(F32)
    for g, win in enumerate(POOL_WINDOWS):
        lanes = slice(g * gd, (g + 1) * gd)
        total = ext_scr[POOL_HIST:POOL_HIST + tr, lanes]
        for j in range(1, win):
            total = total + ext_scr[POOL_HIST - j:POOL_HIST - j + tr, lanes]
        count = jnp.minimum(pos + 1.0, float(win))
        dlt = total / count - u_ref[:, lanes]
        y = jnp.dot(dlt.astype(BF16), w_ref[g], preferred_element_type=F32)
        o_ref[:, lanes] = (y * s_ref[:, lanes]).astype(o_ref.dtype)
    ext_scr[0:POOL_HIST, :] = ext_scr[tr:tr + POOL_HIST, :]


def _pool(u, prev, pos0, w_pool16, pool_scale, tr):
    nseq = prev.shape[0]
    rows, width = u.shape
    seq = rows // nseq
    tr = min(tr, seq)
    nt = seq // tr
    ng, gd, _ = w_pool16.shape
    assert seq % tr == 0 and ng == len(POOL_WINDOWS) and ng * gd == width
    prev16 = jnp.pad(prev, ((0, 0), (POOL_HIST - prev.shape[1], 0), (0, 0)))
    return pl.pallas_call(
        functools.partial(_pool_kernel, pos0=pos0, gd=gd),
        grid=(nseq, nt),
        in_specs=[pl.BlockSpec((tr, width), lambda b, i: (b * nt + i, 0)),
                  pl.BlockSpec((1, POOL_HIST, width), lambda b, i: (b, 0, 0)),
                  pl.BlockSpec((ng, gd, gd), lambda b, i: (0, 0, 0)),
                  pl.BlockSpec((1, width), lambda b, i: (0, 0))],
        out_specs=pl.BlockSpec((tr, width), lambda b, i: (b * nt + i, 0)),
        out_shape=jax.ShapeDtypeStruct((rows, width), BF16),
        scratch_shapes=[pltpu.VMEM((tr + POOL_HIST, width), F32)],
        compiler_params=_params("arbitrary", "arbitrary"),
        name="pool",
    )(u, prev16, w_pool16, pool_scale.reshape(1, width))


def _gla_kernel(q_ref, k_ref, v_ref, r_ref, glr_ref, w2_ref, b2_ref, nw_ref, s0_ref,
                o_ref, sout_ref, s_scr, *, valid_len, n_chunks, scale, hp):
    c = pl.program_id(2)
    cl = q_ref.shape[0]
    dk = s_scr.shape[1]
    dv = s_scr.shape[2]
    heads = range(hp)

    @pl.when(c == 0)
    def _():
        s_scr[...] = s0_ref[0]

    row = c * cl + _iota((cl, 1), 0)
    valid = row < valid_len
    pre = jnp.dot(glr_ref[...], w2_ref[...], preferred_element_type=F32) + b2_ref[...]
    lg_all = (jnp.minimum(pre, 0.0) - _softplus_neg_abs(pre)) / GLA_TAU
    lg_all = jnp.where(valid, lg_all, 0.0)
    qs_all = q_ref[...] * scale
    k_all = jnp.where(valid, k_ref[...], 0.0)
    kl = [slice(h * dk, (h + 1) * dk) for h in heads]
    vl = [slice(h * dv, (h + 1) * dv) for h in heads]
    lg = [lg_all[:, kl[h]] for h in heads]
    qs = [qs_all[:, kl[h]] for h in heads]
    k = [k_all[:, kl[h]] for h in heads]
    v16 = [v_ref[:, vl[h]].astype(BF16) for h in heads]

    t_idx = _iota((cl, cl), 0)
    s_idx = _iota((cl, cl), 1)
    r_idx = _iota((cl, 1), 0)
    def masked_qk(qd, kd, mask):
        return jnp.where(mask, lax.dot_general(qd.astype(BF16), kd.astype(BF16), NT_DIMS,
                                               preferred_element_type=F32), 0.0)

    att = [masked_qk(qs[h], k[h], t_idx == s_idx) for h in heads]
    pre_b = lg
    suf_b = [jnp.zeros_like(lg[h]) for h in heads]
    tot_b = lg
    bsz = 1
    while bsz < cl:
        siblings = ((_div(t_idx, 2 * bsz) == _div(s_idx, 2 * bsz))
                    & (_mod(_div(t_idx, bsz), 2) == 1) & (_mod(_div(s_idx, bsz), 2) == 0))
        att = [att[h] + masked_qk(qs[h] * jnp.exp(pre_b[h]), k[h] * jnp.exp(suf_b[h]), siblings) for h in heads]
        is_right = _mod(_div(r_idx, bsz), 2) == 1
        left_tot = [pltpu.roll(tot_b[h], bsz, axis=0) for h in heads]
        right_tot = [pltpu.roll(tot_b[h], cl - bsz, axis=0) for h in heads]
        pre_b = [pre_b[h] + jnp.where(is_right, left_tot[h], 0.0) for h in heads]
        suf_b = [suf_b[h] + jnp.where(is_right, 0.0, right_tot[h]) for h in heads]
        tot_b = [tot_b[h] + jnp.where(is_right, left_tot[h], right_tot[h]) for h in heads]
        bsz *= 2

    s = [s_scr[h] for h in heads]
    qd = [(qs[h] * jnp.exp(pre_b[h])).astype(BF16) for h in heads]
    kd_t = [(k[h] * jnp.exp(suf_b[h])).T.astype(BF16) for h in heads]
    o = [jnp.dot(att[h].astype(BF16), v16[h], preferred_element_type=F32)
         + jnp.dot(qd[h], s[h].astype(BF16), preferred_element_type=F32) for h in heads]
    chunk_decay = [jnp.exp(tot_b[h].T) for h in heads]
    s_new = [s[h] * jnp.concatenate([chunk_decay[h]] * (dv // cl), axis=1)
             + jnp.dot(kd_t[h], v16[h], preferred_element_type=F32) for h in heads]
    for h in heads:
        s_scr[h] = s_new[h]

    for h in heads:
        ms = jnp.mean(o[h] * o[h], axis=-1, keepdims=True)
        o_ref[:, vl[h]] = (o[h] * lax.rsqrt(ms + RMS_EPS) * nw_ref[:, vl[h]]
                           * _silu(r_ref[:, vl[h]])).astype(o_ref.dtype)

    @pl.when(c == n_chunks - 1)
    def _():
        sout_ref[0] = s_scr[...]


def _gla(q, k, v, r, glr, w2, b2, norm_w, s0, valid_len):
    nseq, heads, dk, dv = s0.shape
    rows = q.shape[0]
    seq = rows // nseq
    cl = CHUNK
    nc = seq // cl
    hp = GLA_HEADS_PER_STEP
    assert seq % cl == 0 and dk == cl and dv % cl == 0 and glr.shape[1] == LANE and heads % hp == 0

    def rows_at(width, col):
        return pl.BlockSpec((cl, width), (lambda b, h, c: (b * nc + c, h)) if col else (lambda b, h, c: (b * nc + c, 0)))

    def head_at(nrows, width):
        return pl.BlockSpec((nrows, width), lambda b, h, c: (0, h))

    state = pl.BlockSpec((1, hp, dk, dv), lambda b, h, c: (b, h, 0, 0))
    kw, vw = hp * dk, hp * dv
    return pl.pallas_call(
        functools.partial(_gla_kernel, valid_len=valid_len, n_chunks=nc, scale=dk ** -0.5, hp=hp),
        grid=(nseq, heads // hp, nc),
        in_specs=[rows_at(kw, True), rows_at(kw, True), rows_at(vw, True), rows_at(vw, True),
                  rows_at(LANE, False), head_at(LANE, kw), head_at(1, kw), head_at(1, vw), state],
        out_specs=[rows_at(vw, True), state],
        out_shape=[jax.ShapeDtypeStruct((rows, heads * dv), BF16), jax.ShapeDtypeStruct(s0.shape, F32)],
        scratch_shapes=[pltpu.VMEM((hp, dk, dv), F32)],
        compiler_params=_params("arbitrary", "arbitrary", "arbitrary"),
        name="gla",
    )(q, k, v, r, glr, w2, b2.reshape(1, heads * dk), norm_w.reshape(1, heads * dv), s0)


def _pad_seq(a, nseq, seq_pad):
    n = a.shape[0] // nseq
    a3 = jnp.pad(a.reshape(nseq, n, a.shape[1]), ((0, 0), (0, seq_pad - n), (0, 0)))
    return a3.reshape(nseq * seq_pad, a.shape[1])


def _unpad_seq(a, nseq, n):
    return a.reshape(nseq, -1, a.shape[1])[:, :n].reshape(nseq * n, a.shape[1])


def _project(xp16, xs16, w, sizes, name, bf16_segments=(), last=()):
    w_nk = w.T
    offs = [sum(sizes[:idx]) for idx in range(len(sizes))]
    outs = {}
    for idx in [i for i in range(len(sizes)) if i not in last] + list(last):
        dtype = BF16 if idx in bf16_segments else F32
        outs[idx] = _proj([xp16], [xs16], w_nk, True, offs[idx], sizes[idx], dtype, PROJ_TM, PROJ_TN,
                          f"{name}_{idx}")
    return [outs[i][0] for i in range(len(sizes))], [outs[i][1] for i in range(len(sizes))]


def kernel(x_prompt, x_sample, cache_k, cache_v, page_table, state_conv, state_ssm, state_pool, state_gla, w_in0, sb_bias, conv_w, conv_b, dt_bias, a_log, d_skip, ssm_norm_w, w_out0, w_in1, w_pool, pool_scale, gla_w2, gla_b2, gla_norm_w, w_out1, ln_g, ln_b, w_gu, w_down):
    bp, seq, dm = x_prompt.shape
    bs, n_new, _ = x_sample.shape
    depth = ln_g.shape[0]
    alpha = (2 * depth) ** 0.25
    sb_heads, sb_d = cache_k.shape[2], cache_k.shape[3]
    sb_w = sb_heads * sb_d
    ssm_heads, ssm_hd, ssm_n = state_ssm.shape[1:]
    ssm_w = ssm_heads * ssm_hd
    conv_ch = state_conv.shape[2]
    groups = (conv_ch - ssm_w) // (2 * ssm_n)
    pool_w = state_pool.shape[2]
    gla_heads, gla_dk, gla_dv = state_gla.shape[1:]
    gla_kw, gla_vw = gla_heads * gla_dk, gla_heads * gla_dv
    gla_rank = gla_w2.shape[0]
    past_len = page_table.shape[1] * cache_k.shape[1]
    seq_pad = CHUNK

    xp = x_prompt.reshape(bp * seq, dm)
    xs = x_sample.reshape(bs * n_new, dm)
    xp16, xs16 = xp.astype(BF16), xs.astype(BF16)

    def lazy_ln(y, sub, layer):
        g, b = ln_g[layer, sub], ln_b[layer, sub]
        x16, mu, rstd = _ln(y, g, b, LN_TM)
        return x16, (y, mu, rstd, g.reshape(1, dm), b.reshape(1, dm))

    def mix_ffn(rp, rs, mixed_p, mixed_s, w_out, layer):
        yp, ys = _proj(mixed_p, mixed_s, w_out, False, 0, dm, F32, PROJ_TM, PROJ_TN, f"out{layer}",
                       res=(rp, rs), alpha=alpha)
        xp16, rp = lazy_ln(yp, 0, layer)
        xs16, rs = lazy_ln(ys, 0, layer)
        hp, hs, w_down16 = _gate_up(xp16, xs16, w_gu, w_down, layer, PROJ_TM, GU_TN)
        return (_mm_res(hp, w_down16, rp, alpha, DOWN_TM, DOWN_TN),
                _mm_res(hs, w_down16, rs, alpha, DOWN_TM, DOWN_TN))

    split0 = (sb_w, sb_w, sb_w, ssm_w, conv_ch, ssm_heads)
    (qp, kp, vp, zp, xbcp, dtp), (qs, ks, vs, zs, xbcs, dts) = _project(xp16, xs16, w_in0, split0, "in0",
                                                                        bf16_segments=(0,), last=(1, 2))
    dtp, dts = dtp[:, :ssm_heads], dts[:, :ssm_heads]

    oap = _sb_prompt(qp, kp, vp, sb_bias, bp, SB_T, SB_HEADS_PER_STEP)
    oas = _sb_decode(qs, ks, vs, cache_k, cache_v, page_table, sb_bias)

    ssd_w = (conv_w, conv_b, dt_bias, a_log, d_skip, ssm_norm_w, groups)
    yp, ssm_p = _ssd(xbcp, dtp, zp, jnp.zeros((bp, CONV_TAPS - 1, conv_ch), F32),
                     jnp.zeros((bp,) + state_ssm.shape[1:], F32), seq, *ssd_w)
    ys, ssm_s = _ssd(_pad_seq(xbcs, bs, seq_pad), _pad_seq(dts, bs, seq_pad), _pad_seq(zs, bs, seq_pad),
                     state_conv, state_ssm, n_new, *ssd_w)
    ys = _unpad_seq(ys, bs, n_new)

    y1p, y1s = mix_ffn((xp,), (xs,), [oap, yp], [oas, ys], w_out0, 0)
    xp16, res_p = lazy_ln(y1p, 1, 0)
    xs16, res_s = lazy_ln(y1s, 1, 0)

    split1 = (pool_w, gla_kw, gla_kw, gla_vw, gla_vw, gla_rank)
    (up, gqp, gkp, gvp, grp, glrp), (us, gqs, gks, gvs, grs, glrs) = _project(xp16, xs16, w_in1, split1, "in1")

    w_pool16 = w_pool.astype(BF16)
    ocp = _pool(up, jnp.zeros((bp, POOL_HIST - 1, pool_w), F32), 0, w_pool16, pool_scale, POOL_TM)
    us8 = _pad_seq(us, bs, SUBLANE)
    ocs = _unpad_seq(_pool(us8, state_pool, past_len, w_pool16, pool_scale, SUBLANE), bs, n_new)

    w2_16 = jnp.pad(gla_w2, ((0, LANE - gla_rank), (0, 0))).astype(BF16)
    gla_w = (w2_16, gla_b2, gla_norm_w)
    odp, gla_p = _gla(gqp, gkp, gvp, grp, glrp.astype(BF16), *gla_w,
                      jnp.zeros((bp,) + state_gla.shape[1:], F32), seq)
    pad = lambda a: _pad_seq(a, bs, seq_pad)
    ods, gla_s = _gla(pad(gqs), pad(gks), pad(gvs), pad(grs), pad(glrs).astype(BF16), *gla_w, state_gla, n_new)
    ods = _unpad_seq(ods, bs, n_new)

    y2p, y2s = mix_ffn(res_p, res_s, [ocp, odp], [ocs, ods], w_out1, 1)
    xp = _ln(y2p, ln_g[1, 1], ln_b[1, 1], LN_TM, final=True)
    xs = _ln(y2s, ln_g[1, 1], ln_b[1, 1], LN_TM, final=True)

    hd4 = lambda a, nseq: a.reshape(nseq, -1, sb_heads, sb_d)
    conv_p = xbcp.reshape(bp, seq, conv_ch)[:, seq - (CONV_TAPS - 1):]
    conv_s = jnp.concatenate([state_conv, xbcs.reshape(bs, n_new, conv_ch)], axis=1)[:, n_new:]
    pool_p = up.reshape(bp, seq, pool_w)[:, seq - (POOL_HIST - 1):]
    pool_s = jnp.concatenate([state_pool, us.reshape(bs, n_new, pool_w)], axis=1)[:, n_new:]
    return (xp.reshape(bp, seq, dm), xs.reshape(bs, n_new, dm),
            hd4(kp, bp), hd4(vp, bp), hd4(ks, bs), hd4(vs, bs),
            conv_p, conv_s, ssm_p, ssm_s, pool_p, pool_s, gla_p, gla_s)
```
